```python
import jax, jax.numpy as jnp
from jax import lax
import numpy as np

D_MODEL = 2048
BATCH = 2
SEQ = 4096
DEPTH = 1

CHUNK = 64
EPS = 1e-6
A_WIDTH = D_MODEL
A_GROUPS = 8
A_GROUP_W = A_WIDTH // A_GROUPS
A_BLOCK = 128
B_HEAD_DIM = 128
B_HEADS = D_MODEL // B_HEAD_DIM
KV_LATENT = D_MODEL // 4
IDX_HEADS = D_MODEL // 128
IDX_DIM = 64
TOPK_MAX = 256
Q_BLOCK = 128
D_FF = ((8 * D_MODEL + 3 * 256 - 1) // (3 * 256)) * 256
SPLIT_SIZES = (2 * A_WIDTH, B_HEADS * B_HEAD_DIM, KV_LATENT, IDX_HEADS * IDX_DIM, IDX_DIM, IDX_HEADS, 2 * D_MODEL)
N_IN = 2 * A_WIDTH + B_HEADS * B_HEAD_DIM + KV_LATENT + IDX_HEADS * IDX_DIM + IDX_DIM + IDX_HEADS + 2 * D_MODEL

kernel_name = "hybrid_gmlp_dsa_block"


def rms_norm(x, g):
    xf = x.astype(jnp.float32)
    y = xf * lax.rsqrt(jnp.mean(xf * xf, axis=-1, keepdims=True) + EPS)
    return (y * g.astype(jnp.float32)).astype(x.dtype)


def layer_norm(x, g, b):
    xf = x.astype(jnp.float32)
    xc = xf - jnp.mean(xf, axis=-1, keepdims=True)
    y = xc * lax.rsqrt(jnp.mean(xc * xc, axis=-1, keepdims=True) + EPS)
    return (y * g.astype(jnp.float32) + b.astype(jnp.float32)).astype(x.dtype)


def chunk_causal(q_pos, k_pos):
    return (k_pos[None, :] // CHUNK) <= (q_pos[:, None] // CHUNK)


def spatial_gating_unit(z, ln_g, ln_b, w_s, b_s):
    bsz, seq, _ = z.shape
    u, v = jnp.split(z, 2, axis=-1)
    v = layer_norm(v, ln_g, ln_b).reshape(bsz, seq // A_BLOCK, A_BLOCK, A_GROUPS, A_GROUP_W)
    pos = jnp.arange(A_BLOCK)
    w = jnp.where(chunk_causal(pos, pos)[None], w_s, jnp.zeros((), w_s.dtype))
    sv = jnp.einsum("gpq,bnqgc->bnpgc", w, v) + b_s.T[None, None, :, :, None]
    return u * sv.reshape(bsz, seq, A_WIDTH)


def indexer_sparse_attention(q, c_kv, q_idx, k_idx, w_idx, w_uk, w_uv):
    bsz, seq = q.shape[0], q.shape[1]
    n_keys = c_kv.shape[1]
    k_sel = min(TOPK_MAX, n_keys // 4)
    n_blk = seq // Q_BLOCK
    key_pos = jnp.arange(n_keys)

    def to_blocks(a):
        return a.reshape((bsz, n_blk, Q_BLOCK) + a.shape[2:]).swapaxes(0, 1)

    def one_block(args):
        qb, qib, wib, start = args
        q_pos = start + jnp.arange(Q_BLOCK)
        adm = chunk_causal(q_pos, key_pos)
        logits = jnp.einsum("bthd,bsd->bths", qib, k_idx).astype(jnp.float32) * IDX_DIM ** -0.5
        score = jnp.einsum("bth,bths->bts", wib.astype(jnp.float32) * IDX_HEADS ** -0.5, jax.nn.relu(logits))
        score = jnp.where(adm[None], score, -jnp.inf)
        _, sel = lax.top_k(score, k_sel)
        valid = (sel // CHUNK) <= (q_pos[None, :, None] // CHUNK)
        c_sel = jax.vmap(lambda c, i: c[i])(c_kv, sel)
        q_lat = jnp.einsum("bthd,chd->bthc", qb, w_uk)
        att = jnp.einsum("bthc,btkc->bthk", q_lat, c_sel).astype(jnp.float32) * B_HEAD_DIM ** -0.5
        att = jnp.where(valid[:, :, None, :], att, -jnp.inf)
        p = jax.nn.softmax(att, axis=-1).astype(c_sel.dtype)
        o_lat = jnp.einsum("bthk,btkc->bthc", p, c_sel)
        return jnp.einsum("bthc,chd->bthd", o_lat, w_uv)

    starts = jnp.arange(n_blk, dtype=jnp.int32) * Q_BLOCK
    out = lax.map(one_block, (to_blocks(q), to_blocks(q_idx), to_blocks(w_idx), starts))
    return out.swapaxes(0, 1).reshape(bsz, seq, B_HEADS * B_HEAD_DIM)


def setup_inputs(seed: int = 0) -> dict:
    key = jax.random.key(seed)
    ks = jax.random.split(key, 18)
    f32 = jnp.float32

    def nrm(k, shape, scale):
        return jax.random.normal(k, shape, f32) * scale

    def gain(k, shape):
        return 1.0 + 0.02 * jax.random.normal(k, shape, f32)

    L = DEPTH
    return {
        "x": nrm(ks[0], (BATCH, SEQ, D_MODEL), 1.0),
        "norm1_g": gain(ks[1], (L, D_MODEL)),
        "w_in": nrm(ks[2], (L, D_MODEL, N_IN), D_MODEL ** -0.5),
        "a_ln_g": gain(ks[3], (L, A_WIDTH)),
        "a_ln_b": nrm(ks[4], (L, A_WIDTH), 0.02),
        "a_w_s": nrm(ks[5], (L, A_GROUPS, A_BLOCK, A_BLOCK), A_BLOCK ** -0.5),
        "a_b_s": gain(ks[6], (L, A_GROUPS, A_BLOCK)),
        "kv_norm_g": gain(ks[7], (L, KV_LATENT)),
        "w_uk": nrm(ks[8], (L, KV_LATENT, B_HEADS, B_HEAD_DIM), KV_LATENT ** -0.5),
        "w_uv": nrm(ks[9], (L, KV_LATENT, B_HEADS, B_HEAD_DIM), KV_LATENT ** -0.5),
        "w_oa": nrm(ks[10], (L, A_WIDTH, D_MODEL), A_WIDTH ** -0.5),
        "w_ob": nrm(ks[11], (L, B_HEADS * B_HEAD_DIM, D_MODEL), (B_HEADS * B_HEAD_DIM) ** -0.5),
        "w_out": nrm(ks[12], (L, D_MODEL, D_MODEL), D_MODEL ** -0.5),
        "norm2_g": gain(ks[13], (L, D_MODEL)),
        "w_ff_gate": nrm(ks[14], (L, D_MODEL, D_FF), D_MODEL ** -0.5),
        "w_ff_up": nrm(ks[15], (L, D_MODEL, D_FF), D_MODEL ** -0.5),
        "w_ff_down": nrm(ks[16], (L, D_FF, D_MODEL), D_FF ** -0.5),
        "final_g": gain(ks[17], (D_MODEL,)),
    }


def reference(x, norm1_g, w_in, a_ln_g, a_ln_b, a_w_s, a_b_s, kv_norm_g, w_uk, w_uv, w_oa, w_ob, w_out, norm2_g, w_ff_gate, w_ff_up, w_ff_down, final_g):
    bsz, seq, _ = x.shape
    cuts = []
    offset = 0
    for n in SPLIT_SIZES[:-1]:
        offset += n
        cuts.append(offset)
    h = x
    for l in range(DEPTH):
        xn = rms_norm(h, norm1_g[l])
        proj = xn @ w_in[l]
        z_a, q, c_kv, q_idx, k_idx, w_idx, gates = jnp.split(proj, cuts, axis=-1)
        y_a = spatial_gating_unit(jax.nn.gelu(z_a), a_ln_g[l], a_ln_b[l], a_w_s[l], a_b_s[l])
        y_b = indexer_sparse_attention(
            q.reshape(bsz, seq, B_HEADS, B_HEAD_DIM),
            rms_norm(c_kv, kv_norm_g[l]),
            q_idx.reshape(bsz, seq, IDX_HEADS, IDX_DIM),
            k_idx, w_idx, w_uk[l], w_uv[l])
        g_a, g_b = jnp.split(jax.nn.sigmoid(gates), 2, axis=-1)
        mixed = g_a * (y_a @ w_oa[l]) + g_b * (y_b @ w_ob[l])
        h = h + mixed @ w_out[l]
        hn = rms_norm(h, norm2_g[l])
        h = h + (jax.nn.silu(hn @ w_ff_gate[l]) * (hn @ w_ff_up[l])) @ w_ff_down[l]
    return rms_norm(h, final_g)
```

```python
import functools

import jax
import jax.numpy as jnp
from jax import lax
from jax.experimental import pallas as pl
from jax.experimental.pallas import tpu as pltpu

EPS = 1e-6
CHUNK = 64
A_GROUPS = 8
A_BLOCK = 128
HEAD_DIM = 128
IDX_DIM = 64
IDX_HEADS = 16
TOPK_MAX = 256

VMEM_LIMIT_BYTES = 56 * 1024 * 1024
INT_MIN = -2147483648
NEG_BIG = -1e30

F32 = jnp.float32
BF16 = jnp.bfloat16


def _params(*sem):
    return pltpu.CompilerParams(dimension_semantics=sem, vmem_limit_bytes=VMEM_LIMIT_BYTES)


def _nt_dot(a, b):
    return lax.dot_general(a, b, (((1,), (1,)), ((), ())), preferred_element_type=F32)


def _dot(a, b):
    return jnp.dot(a, b, preferred_element_type=F32)


def _rms(x, g):
    return x * lax.rsqrt(jnp.mean(x * x, axis=-1, keepdims=True) + EPS) * g


def _rmsnorm_kernel(x_ref, g_ref, o_ref):
    o_ref[...] = _rms(x_ref[...], g_ref[...]).astype(o_ref.dtype)


def _rmsnorm(x, g, tm=512):
    m, d = x.shape
    return pl.pallas_call(
        _rmsnorm_kernel,
        out_shape=jax.ShapeDtypeStruct((m, d), BF16),
        grid=(m // tm,),
        in_specs=[pl.BlockSpec((tm, d), lambda i: (i, 0)), pl.BlockSpec((1, d), lambda i: (0, 0))],
        out_specs=pl.BlockSpec((tm, d), lambda i: (i, 0)),
        compiler_params=_params("parallel"),
        name="rmsnorm",
    )(x, g.reshape(1, d))


def _mm_kernel(a_ref, w_ref, o_ref):
    o_ref[...] = _dot(a_ref[...], w_ref[...]).astype(o_ref.dtype)


def _matmul(a, w, out_dtype, tm, tn, name):
    m, k = a.shape
    n = w.shape[1]
    return pl.pallas_call(
        _mm_kernel,
        out_shape=jax.ShapeDtypeStruct((m, n), out_dtype),
        grid=(m // tm, n // tn),
        in_specs=[pl.BlockSpec((tm, k), lambda i, j: (i, 0)), pl.BlockSpec((k, tn), lambda i, j: (0, j))],
        out_specs=pl.BlockSpec((tm, tn), lambda i, j: (i, j)),
        compiler_params=_params("parallel", "arbitrary"),
        name=name,
    )(a, w)


def _branch_a_kernel(xn_ref, w_ref, lng_ref, lnb_ref, ws_ref, bst_ref, o_ref, z_ref, *, tm, tn, width):
    j = pl.program_id(1)
    z_ref[j] = _dot(xn_ref[...], w_ref[...])

    @pl.when(j == pl.num_programs(1) - 1)
    def _():
        per_half = width // tn
        gw = width // A_GROUPS
        row = lax.broadcasted_iota(jnp.int32, (A_BLOCK, A_BLOCK), 0)
        col = lax.broadcasted_iota(jnp.int32, (A_BLOCK, A_BLOCK), 1)
        causal = (col // CHUNK) <= (row // CHUNK)
        for r in range(tm // A_BLOCK):
            rows = slice(r * A_BLOCK, (r + 1) * A_BLOCK)
            v = jnp.concatenate([z_ref[per_half + c, rows, :] for c in range(per_half)], axis=-1)
            v = jax.nn.gelu(v)
            vc = v - jnp.mean(v, axis=-1, keepdims=True)
            vn = vc * lax.rsqrt(jnp.mean(vc * vc, axis=-1, keepdims=True) + EPS)
            vn = (vn * lng_ref[...] + lnb_ref[...]).astype(BF16)
            for g in range(A_GROUPS):
                wm = jnp.where(causal, ws_ref[g], 0.0).astype(BF16)
                sv = _dot(wm, vn[:, g * gw:(g + 1) * gw]) + bst_ref[:, g:g + 1]
                c, off = divmod(g * gw, tn)
                u = jax.nn.gelu(z_ref[c, rows, off:off + gw])
                o_ref[rows, g * gw:(g + 1) * gw] = (u * sv).astype(o_ref.dtype)


def _branch_a(xn, w_za, ln_g, ln_b, w_s, b_s, tm=512, tn=512):
    m, d = xn.shape
    width = w_za.shape[1] // 2
    assert (width // A_GROUPS) <= tn and tn % (width // A_GROUPS) == 0
    kern = functools.partial(_branch_a_kernel, tm=tm, tn=tn, width=width)
    return pl.pallas_call(
        kern,
        out_shape=jax.ShapeDtypeStruct((m, width), BF16),
        grid=(m // tm, 2 * width // tn),
        in_specs=[
            pl.BlockSpec((tm, d), lambda i, j: (i, 0)),
            pl.BlockSpec((d, tn), lambda i, j: (0, j)),
            pl.BlockSpec((1, width), lambda i, j: (0, 0)),
            pl.BlockSpec((1, width), lambda i, j: (0, 0)),
            pl.BlockSpec((A_GROUPS, A_BLOCK, A_BLOCK), lambda i, j: (0, 0, 0)),
            pl.BlockSpec((A_BLOCK, A_GROUPS), lambda i, j: (0, 0)),
        ],
        out_specs=pl.BlockSpec((tm, width), lambda i, j: (i, 0)),
        scratch_shapes=[pltpu.VMEM((2 * width // tn, tm, tn), F32)],
        compiler_params=_params("parallel", "arbitrary"),
        name="branch_a",
    )(xn, w_za, ln_g.reshape(1, width), ln_b.reshape(1, width), w_s, b_s.T)


def _kv_kernel(xn_ref, wc_ref, g_ref, wuk_ref, wuv_ref, k_ref, v_ref, *, heads):
    c = _dot(xn_ref[...], wc_ref[...])
    cn = _rms(c, g_ref[...]).astype(BF16)
    k = _dot(cn, wuk_ref[...])
    v = _dot(cn, wuv_ref[...])
    for h in range(heads):
        k_ref[h] = k[:, h * HEAD_DIM:(h + 1) * HEAD_DIM].astype(k_ref.dtype)
        v_ref[h] = v[:, h * HEAD_DIM:(h + 1) * HEAD_DIM].astype(v_ref.dtype)


def _kv_proj(xn, w_c, g, w_uk, w_uv, bsz, seq, tm=512):
    m, d = xn.shape
    lat = w_c.shape[1]
    hd = w_uk.shape[1]
    heads = hd // HEAD_DIM
    nt = seq // tm
    out = jax.ShapeDtypeStruct((bsz, heads, seq, HEAD_DIM), BF16)
    ospec = pl.BlockSpec((None, heads, tm, HEAD_DIM), lambda i: (i // nt, 0, i % nt, 0))
    return pl.pallas_call(
        functools.partial(_kv_kernel, heads=heads),
        out_shape=(out, out),
        grid=(m // tm,),
        in_specs=[
            pl.BlockSpec((tm, d), lambda i: (i, 0)),
            pl.BlockSpec((d, lat), lambda i: (0, 0)),
            pl.BlockSpec((1, lat), lambda i: (0, 0)),
            pl.BlockSpec((lat, hd), lambda i: (0, 0)),
            pl.BlockSpec((lat, hd), lambda i: (0, 0)),
        ],
        out_specs=(ospec, ospec),
        compiler_params=_params("parallel"),
        name="kv_proj",
    )(xn, w_c, g.reshape(1, lat), w_uk, w_uv)


def _attention_kernel(q_ref, qi_ref, kpe_ref, kpo_ref, w_ref, k_ref, v_ref, o_ref, key_ref, bias_ref, *, tq, k_sel):
    qt = pl.program_id(1)
    h = pl.program_id(2)
    n_kt = qt + 1
    lanes = 128

    def count(pred):
        def body(kt, cnt):
            tile = key_ref[kt]
            for c in range(tq // lanes):
                kpos = kt * tq + c * lanes + lax.broadcasted_iota(jnp.int32, (tq, lanes), 1)
                cnt = cnt + jnp.where(pred(tile[:, c * lanes:(c + 1) * lanes], kpos), 1.0, 0.0)
            return cnt
        cnt = lax.fori_loop(0, n_kt, body, jnp.zeros((tq, lanes), F32))
        return jnp.sum(cnt, axis=1, keepdims=True)

    @pl.when(h == 0)
    def _select():
        wv = w_ref[...] * (IDX_DIM ** -0.5 * IDX_HEADS ** -0.5)
        qpos = qt * tq + lax.broadcasted_iota(jnp.int32, (tq, tq), 0)

        def score_body(kt, carry):
            k0 = pl.multiple_of(kt * tq, tq)
            ke = kpe_ref[pl.ds(k0, tq), :]
            ko = kpo_ref[pl.ds(k0, tq), :]
            acc = jnp.zeros((tq, tq), F32)
            for j in range(IDX_HEADS // 2):
                qp = qi_ref[:, j * lanes:(j + 1) * lanes]
                acc = acc + wv[:, 2 * j:2 * j + 1] * jnp.maximum(_nt_dot(qp, ke), 0.0)
                acc = acc + wv[:, 2 * j + 1:2 * j + 2] * jnp.maximum(_nt_dot(qp, ko), 0.0)
            kpos = k0 + lax.broadcasted_iota(jnp.int32, (tq, tq), 1)
            adm = (kpos // CHUNK) <= (qpos // CHUNK)
            bits = lax.bitcast_convert_type(acc, jnp.int32)
            key = bits ^ ((bits >> 31) & 0x7FFFFFFF)
            key_ref[kt] = jnp.where(adm, key, INT_MIN)
            return carry
        lax.fori_loop(0, n_kt, score_body, 0)

        def bit_body(i, tau):
            cand = tau ^ jnp.left_shift(jnp.int32(1), 31 - i)
            candb = jnp.broadcast_to(cand, (tq, lanes))
            n = count(lambda t, kp: t >= candb)
            return jnp.where(n >= k_sel, cand, tau)
        tau = lax.fori_loop(0, 32, bit_body, jnp.full((tq, 1), INT_MIN, jnp.int32))
        taub = jnp.broadcast_to(tau, (tq, lanes))
        n_ge = count(lambda t, kp: t >= taub)

        def write_bias(sel):
            def body(kt, carry):
                tile = key_ref[kt]
                kpos = kt * tq + lax.broadcasted_iota(jnp.int32, (tq, tq), 1)
                keep = sel(tile, kpos) & (tile != INT_MIN)
                bias_ref[kt] = jnp.where(keep, 0.0, NEG_BIG)
                return carry
            lax.fori_loop(0, n_kt, body, 0)

        write_bias(lambda t, kp: t >= tau)

        @pl.when(jnp.max(n_ge) > k_sel)
        def _ties():
            n_gt = count(lambda t, kp: t > taub)
            need = k_sel - n_gt

            def idx_body(i, cut):
                cand = cut + jnp.left_shift(jnp.int32(1), 12 - i)
                candb = jnp.broadcast_to(cand, (tq, lanes))
                n = count(lambda t, kp: (t == taub) & (kp < candb))
                return jnp.where(n <= need, cand, cut)
            cut = lax.fori_loop(0, 13, idx_body, jnp.zeros((tq, 1), jnp.int32))
            write_bias(lambda t, kp: (t > tau) | ((t == tau) & (kp < cut)))

    qh = q_ref[...]
    scale = HEAD_DIM ** -0.5

    def att_body(kt, carry):
        m, l, acc = carry
        k0 = pl.multiple_of(kt * tq, tq)
        s = _nt_dot(qh, k_ref[pl.ds(k0, tq), :]) * scale + bias_ref[kt]
        m_new = jnp.maximum(m, jnp.max(s, axis=1, keepdims=True))
        alpha = jnp.exp(m - m_new)
        p = jnp.exp(s - m_new)
        l = alpha * l + jnp.sum(p, axis=1, keepdims=True)
        acc = alpha * acc + _dot(p.astype(BF16), v_ref[pl.ds(k0, tq), :])
        return m_new, l, acc

    init = (jnp.full((tq, 1), NEG_BIG, F32), jnp.zeros((tq, 1), F32), jnp.zeros((tq, HEAD_DIM), F32))
    _, l, acc = lax.fori_loop(0, n_kt, att_body, init)
    o_ref[...] = (acc / l).astype(o_ref.dtype)


def _attention(qq, kpe, kpo, w_idx, k_heads, v_heads, tq=256):
    bsz, heads, seq, _ = k_heads.shape
    m = bsz * seq
    nq = seq // tq
    k_sel = min(TOPK_MAX, seq // 4)
    assert tq >= k_sel and tq % CHUNK == 0
    q_cols = heads * HEAD_DIM
    qi_cols = IDX_HEADS * IDX_DIM
    assert q_cols % qi_cols == 0
    kern = functools.partial(_attention_kernel, tq=tq, k_sel=k_sel)
    return pl.pallas_call(
        kern,
        out_shape=jax.ShapeDtypeStruct((m, q_cols), BF16),
        grid=(bsz, nq, heads),
        in_specs=[
            pl.BlockSpec((tq, HEAD_DIM), lambda b, t, h: (b * nq + t, h)),
            pl.BlockSpec((tq, qi_cols), lambda b, t, h: (b * nq + t, q_cols // qi_cols)),
            pl.BlockSpec((None, seq, 2 * IDX_DIM), lambda b, t, h: (b, 0, 0)),
            pl.BlockSpec((None, seq, 2 * IDX_DIM), lambda b, t, h: (b, 0, 0)),
            pl.BlockSpec((tq, IDX_HEADS), lambda b, t, h: (b * nq + t, 0)),
            pl.BlockSpec((None, None, seq, HEAD_DIM), lambda b, t, h: (b, h, 0, 0)),
            pl.BlockSpec((None, None, seq, HEAD_DIM), lambda b, t, h: (b, h, 0, 0)),
        ],
        out_specs=pl.BlockSpec((tq, HEAD_DIM), lambda b, t, h: (b * nq + t, h)),
        scratch_shapes=[pltpu.VMEM((nq, tq, tq), jnp.int32), pltpu.VMEM((nq, tq, tq), F32)],
        compiler_params=_params("parallel", "parallel", "arbitrary"),
        name="attention",
    )(qq, qq, kpe, kpo, w_idx, k_heads, v_heads)


def _merge_kernel(xn_ref, ya_ref, yb_ref, wga_ref, wgb_ref, woa_ref, wob_ref, o_ref):
    xn = xn_ref[...]
    ga = jax.nn.sigmoid(_dot(xn, wga_ref[...]))
    gb = jax.nn.sigmoid(_dot(xn, wgb_ref[...]))
    o_ref[...] = (ga * _dot(ya_ref[...], woa_ref[...]) + gb * _dot(yb_ref[...], wob_ref[...])).astype(o_ref.dtype)


def _merge(xn, ya, yb, w_ga, w_gb, w_oa, w_ob, tm=512, tn=512):
    m, d = xn.shape
    row = lambda k: pl.BlockSpec((tm, k), lambda i, j: (i, 0))
    colw = lambda k: pl.BlockSpec((k, tn), lambda i, j: (0, j))
    return pl.pallas_call(
        _merge_kernel,
        out_shape=jax.ShapeDtypeStruct((m, d), BF16),
        grid=(m // tm, d // tn),
        in_specs=[row(d), row(ya.shape[1]), row(yb.shape[1]),
                  colw(d), colw(d), colw(ya.shape[1]), colw(yb.shape[1])],
        out_specs=pl.BlockSpec((tm, tn), lambda i, j: (i, j)),
        compiler_params=_params("parallel", "arbitrary"),
        name="merge",
    )(xn, ya, yb, w_ga, w_gb, w_oa, w_ob)


def _out_kernel(x_ref, mix_ref, w_ref, g_ref, h_ref, hn_ref):
    h = x_ref[...] + _dot(mix_ref[...], w_ref[...])
    h_ref[...] = h
    hn_ref[...] = _rms(h, g_ref[...]).astype(hn_ref.dtype)


def _out_proj(x, mixed, w_out, g, tm=256):
    m, d = x.shape
    row = pl.BlockSpec((tm, d), lambda i: (i, 0))
    return pl.pallas_call(
        _out_kernel,
        out_shape=(jax.ShapeDtypeStruct((m, d), F32), jax.ShapeDtypeStruct((m, d), BF16)),
        grid=(m // tm,),
        in_specs=[row, row, pl.BlockSpec((d, d), lambda i: (0, 0)), pl.BlockSpec((1, d), lambda i: (0, 0))],
        out_specs=(row, row),
        compiler_params=_params("parallel"),
        name="out_proj",
    )(x, mixed, w_out, g.reshape(1, d))


def _ffn_kernel(hn_ref, h_ref, wg_ref, wu_ref, wd_ref, g_ref, o_ref, acc_ref, *, final_norm):
    j = pl.program_id(1)
    hn = hn_ref[...]
    t = (jax.nn.silu(_dot(hn, wg_ref[...])) * _dot(hn, wu_ref[...])).astype(BF16)
    part = _dot(t, wd_ref[...])

    @pl.when(j == 0)
    def _():
        acc_ref[...] = part

    @pl.when(j > 0)
    def _():
        acc_ref[...] += part

    @pl.when(j == pl.num_programs(1) - 1)
    def _():
        h2 = h_ref[...] + acc_ref[...]
        o_ref[...] = _rms(h2, g_ref[...]) if final_norm else h2


def _ffn(hn, h, w_gate, w_up, w_down, g, final_norm, tm=512, tf=512):
    m, d = hn.shape
    f = w_gate.shape[1]
    row = pl.BlockSpec((tm, d), lambda i, j: (i, 0))
    return pl.pallas_call(
        functools.partial(_ffn_kernel, final_norm=final_norm),
        out_shape=jax.ShapeDtypeStruct((m, d), F32),
        grid=(m // tm, f // tf),
        in_specs=[row, row,
                  pl.BlockSpec((d, tf), lambda i, j: (0, j)),
                  pl.BlockSpec((d, tf), lambda i, j: (0, j)),
                  pl.BlockSpec((tf, d), lambda i, j: (j, 0)),
                  pl.BlockSpec((1, d), lambda i, j: (0, 0))],
        out_specs=row,
        scratch_shapes=[pltpu.VMEM((tm, d), F32)],
        compiler_params=_params("parallel", "arbitrary"),
        name="ffn",
    )(hn, h, w_gate, w_up, w_down, g.reshape(1, d))


def kernel(x, norm1_g, w_in, a_ln_g, a_ln_b, a_w_s, a_b_s, kv_norm_g, w_uk, w_uv, w_oa, w_ob, w_out, norm2_g,
           w_ff_gate, w_ff_up, w_ff_down, final_g):
    bsz, seq, d = x.shape
    m = bsz * seq
    depth = norm1_g.shape[0]
    a_width = a_ln_g.shape[1]
    lat = kv_norm_g.shape[1]
    heads, head_dim = w_uk.shape[2], w_uk.shape[3]
    assert head_dim == HEAD_DIM
    hd = heads * head_dim
    qi = IDX_HEADS * IDX_DIM
    c_q = 2 * a_width
    c_kv = c_q + hd
    c_qi = c_kv + lat
    c_ki = c_qi + qi
    c_wi = c_ki + IDX_DIM
    c_g = c_wi + IDX_HEADS

    h = x.reshape(m, d)
    for l in range(depth):
        w = w_in[l]
        w_za = w[:, :c_q].astype(BF16)
        w_qq = jnp.concatenate([w[:, c_q:c_kv], w[:, c_qi:c_ki]], axis=1).astype(BF16)
        w_c = w[:, c_kv:c_qi].astype(BF16)
        w_kw = jnp.pad(w[:, c_ki:c_g], ((0, 0), (0, 128 - IDX_DIM - IDX_HEADS))).astype(BF16)
        w_ga = w[:, c_g:c_g + d].astype(BF16)
        w_gb = w[:, c_g + d:].astype(BF16)

        xn = _rmsnorm(h, norm1_g[l])
        y_a = _branch_a(xn, w_za, a_ln_g[l], a_ln_b[l], a_w_s[l], a_b_s[l])
        qq = _matmul(xn, w_qq, BF16, 1024, 512, "q_proj")
        kw = _matmul(xn, w_kw, F32, 1024, 128, "kw_proj")
        k_heads, v_heads = _kv_proj(xn, w_c, kv_norm_g[l], w_uk[l].reshape(lat, hd).astype(BF16),
                                    w_uv[l].reshape(lat, hd).astype(BF16), bsz, seq)
        k_idx = kw[:, :IDX_DIM].astype(BF16).reshape(bsz, seq, IDX_DIM)
        zeros = jnp.zeros_like(k_idx)
        kpe = jnp.concatenate([k_idx, zeros], axis=-1)
        kpo = jnp.concatenate([zeros, k_idx], axis=-1)
        w_idx = kw[:, IDX_DIM:IDX_DIM + IDX_HEADS]
        y_b = _attention(qq, kpe, kpo, w_idx, k_heads, v_heads)
        mixed = _merge(xn, y_a, y_b, w_ga, w_gb, w_oa[l].astype(BF16), w_ob[l].astype(BF16))
        h, hn = _out_proj(h, mixed, w_out[l].astype(BF16), norm2_g[l])
        h = _ffn(hn, h, w_ff_gate[l].astype(BF16), w_ff_up[l].astype(BF16), w_ff_down[l].astype(BF16), final_g,
                 final_norm=(l == depth - 1))
    return h.reshape(bsz, seq, d)
```

```python
import functools

import jax
import jax.numpy as jnp
from jax import lax
from jax.experimental import pallas as pl
from jax.experimental.pallas import tpu as pltpu

EPS = 1e-6
CHUNK = 64
A_GROUPS = 8
A_BLOCK = 128
HEAD_DIM = 128
IDX_DIM = 64
IDX_HEADS = 16
TOPK_MAX = 256

VMEM_LIMIT_BYTES = 56 * 1024 * 1024
INT_MIN = -2147483648
ATT_TQ = 256
ATT_TK = 512
NEG_BIG = -1e30

F32 = jnp.float32
BF16 = jnp.bfloat16


def _params(*sem):
    return pltpu.CompilerParams(dimension_semantics=sem, vmem_limit_bytes=VMEM_LIMIT_BYTES)


def _nt_dot(a, b):
    return lax.dot_general(a, b, (((1,), (1,)), ((), ())), preferred_element_type=F32)


def _dot(a, b):
    return jnp.dot(a, b, preferred_element_type=F32)


def _rms(x, g):
    return x * lax.rsqrt(jnp.mean(x * x, axis=-1, keepdims=True) + EPS) * g


def _rmsnorm_kernel(x_ref, g_ref, o_ref):
    o_ref[...] = _rms(x_ref[...], g_ref[...]).astype(o_ref.dtype)


def _rmsnorm(x, g, tm=512):
    m, d = x.shape
    return pl.pallas_call(
        _rmsnorm_kernel,
        out_shape=jax.ShapeDtypeStruct((m, d), BF16),
        grid=(m // tm,),
        in_specs=[pl.BlockSpec((tm, d), lambda i: (i, 0)), pl.BlockSpec((1, d), lambda i: (0, 0))],
        out_specs=pl.BlockSpec((tm, d), lambda i: (i, 0)),
        compiler_params=_params("parallel"),
        name="rmsnorm",
    )(x, g.reshape(1, d))


def _mm_kernel(a_ref, w_ref, o_ref, *, scale):
    o_ref[...] = (_dot(a_ref[...], w_ref[...]) * scale).astype(o_ref.dtype)


def _matmul(a, w, out_dtype, tm, tn, name, scale=1.0):
    m, k = a.shape
    n = w.shape[1]
    return pl.pallas_call(
        functools.partial(_mm_kernel, scale=scale),
        out_shape=jax.ShapeDtypeStruct((m, n), out_dtype),
        grid=(m // tm, n // tn),
        in_specs=[pl.BlockSpec((tm, k), lambda i, j: (i, 0)), pl.BlockSpec((k, tn), lambda i, j: (0, j))],
        out_specs=pl.BlockSpec((tm, tn), lambda i, j: (i, j)),
        compiler_params=_params("parallel", "arbitrary"),
        name=name,
    )(a, w)


def _branch_a_kernel(xn_ref, w_ref, lng_ref, lnb_ref, ws_ref, bst_ref, o_ref, z_ref, *, tm, tn, width):
    j = pl.program_id(1)
    z_ref[j] = _dot(xn_ref[...], w_ref[...])

    @pl.when(j == pl.num_programs(1) - 1)
    def _():
        per_half = width // tn
        gw = width // A_GROUPS
        row = lax.broadcasted_iota(jnp.int32, (A_BLOCK, A_BLOCK), 0)
        col = lax.broadcasted_iota(jnp.int32, (A_BLOCK, A_BLOCK), 1)
        causal = (col // CHUNK) <= (row // CHUNK)
        for r in range(tm // A_BLOCK):
            rows = slice(r * A_BLOCK, (r + 1) * A_BLOCK)
            v = jnp.concatenate([z_ref[per_half + c, rows, :] for c in range(per_half)], axis=-1)
            v = jax.nn.gelu(v)
            vc = v - jnp.mean(v, axis=-1, keepdims=True)
            vn = vc * lax.rsqrt(jnp.mean(vc * vc, axis=-1, keepdims=True) + EPS)
            vn = (vn * lng_ref[...] + lnb_ref[...]).astype(BF16)
            for g in range(A_GROUPS):
                wm = jnp.where(causal, ws_ref[g], 0.0).astype(BF16)
                sv = _dot(wm, vn[:, g * gw:(g + 1) * gw]) + bst_ref[:, g:g + 1]
                c, off = divmod(g * gw, tn)
                u = jax.nn.gelu(z_ref[c, rows, off:off + gw])
                o_ref[rows, g * gw:(g + 1) * gw] = (u * sv).astype(o_ref.dtype)


def _branch_a(xn, w_za, ln_g, ln_b, w_s, b_s, tm=512, tn=512):
    m, d = xn.shape
    width = w_za.shape[1] // 2
    assert (width // A_GROUPS) <= tn and tn % (width // A_GROUPS) == 0
    kern = functools.partial(_branch_a_kernel, tm=tm, tn=tn, width=width)
    return pl.pallas_call(
        kern,
        out_shape=jax.ShapeDtypeStruct((m, width), BF16),
        grid=(m // tm, 2 * width // tn),
        in_specs=[
            pl.BlockSpec((tm, d), lambda i, j: (i, 0)),
            pl.BlockSpec((d, tn), lambda i, j: (0, j)),
            pl.BlockSpec((1, width), lambda i, j: (0, 0)),
            pl.BlockSpec((1, width), lambda i, j: (0, 0)),
            pl.BlockSpec((A_GROUPS, A_BLOCK, A_BLOCK), lambda i, j: (0, 0, 0)),
            pl.BlockSpec((A_BLOCK, A_GROUPS), lambda i, j: (0, 0)),
        ],
        out_specs=pl.BlockSpec((tm, width), lambda i, j: (i, 0)),
        scratch_shapes=[pltpu.VMEM((2 * width // tn, tm, tn), F32)],
        compiler_params=_params("parallel", "arbitrary"),
        name="branch_a",
    )(xn, w_za, ln_g.reshape(1, width), ln_b.reshape(1, width), w_s, b_s.T)


def _kv_kernel(xn_ref, wc_ref, g_ref, wuk_ref, wuvt_ref, k_ref, vt_ref):
    c = _dot(xn_ref[...], wc_ref[...])
    cn = _rms(c, g_ref[...]).astype(BF16)
    k_ref[...] = _dot(cn, wuk_ref[...]).astype(k_ref.dtype)
    vt_ref[...] = _nt_dot(wuvt_ref[...], cn).astype(vt_ref.dtype)


def _kv_proj(xn, w_c, g, w_uk, w_uv_t, bsz, seq, tk):
    m, d = xn.shape
    lat = w_c.shape[1]
    hd = w_uk.shape[1]
    nt = seq // tk
    return pl.pallas_call(
        _kv_kernel,
        out_shape=(jax.ShapeDtypeStruct((m, hd), BF16), jax.ShapeDtypeStruct((bsz, nt, hd, tk), BF16)),
        grid=(m // tk,),
        in_specs=[
            pl.BlockSpec((tk, d), lambda i: (i, 0)),
            pl.BlockSpec((d, lat), lambda i: (0, 0)),
            pl.BlockSpec((1, lat), lambda i: (0, 0)),
            pl.BlockSpec((lat, hd), lambda i: (0, 0)),
            pl.BlockSpec((hd, lat), lambda i: (0, 0)),
        ],
        out_specs=(pl.BlockSpec((tk, hd), lambda i: (i, 0)),
                   pl.BlockSpec((None, None, hd, tk), lambda i: (i // nt, i % nt, 0, 0))),
        compiler_params=_params("parallel"),
        name="kv_proj",
    )(xn, w_c, g.reshape(1, lat), w_uk, w_uv_t)


def _n_key_tiles(qt, tq, tk):
    return lax.div(qt * tq + tq + tk - 1, tk)


def _indexer_kernel(qi_ref, kpe_ref, kpo_ref, wt_ref, o_ref, sc_ref, *, tq, tk, k_sel):
    qt = pl.program_id(1)
    n_kt = _n_key_tiles(qt, tq, tk)
    nk = sc_ref.shape[0]
    neg_inf = -jnp.inf

    def count(pred):
        def body(kt, cnt):
            hit = jnp.where(pred(sc_ref[kt], kt), 1.0, 0.0)
            return cnt + jnp.sum(hit.reshape(tk // 8, 8, tq), axis=0)
        cnt = lax.fori_loop(0, n_kt, body, jnp.zeros((8, tq), F32))
        return jnp.sum(cnt, axis=0, keepdims=True)

    def key_pos(kt):
        return kt * tk + lax.broadcasted_iota(jnp.int32, (tk, tq), 0)

    wt = wt_ref[...] * (IDX_DIM ** -0.5 * IDX_HEADS ** -0.5)
    q_chunk = (qt * tq + lax.broadcasted_iota(jnp.int32, (1, tq), 1)) // CHUNK

    def score_body(kt, carry):
        k0 = pl.multiple_of(kt * tk, tk)
        ke = kpe_ref[pl.ds(k0, tk), :]
        ko = kpo_ref[pl.ds(k0, tk), :]
        acc = jnp.zeros((tk, tq), F32)
        for j in range(IDX_HEADS // 2):
            qp = qi_ref[:, j * 128:(j + 1) * 128]
            acc = acc + wt[2 * j:2 * j + 1, :] * jnp.maximum(_nt_dot(ke, qp), 0.0)
            acc = acc + wt[2 * j + 1:2 * j + 2, :] * jnp.maximum(_nt_dot(ko, qp), 0.0)
        k_chunk = (k0 + lax.broadcasted_iota(jnp.int32, (tk, 1), 0)) // CHUNK
        sc_ref[kt] = jnp.where(k_chunk <= q_chunk, acc, neg_inf)
        return carry
    lax.fori_loop(0, n_kt, score_body, 0)

    def key_to_float(key):
        return lax.bitcast_convert_type(key ^ ((key >> 31) & 0x7FFFFFFF), F32)

    def bit_body(i, tau):
        cand = tau ^ jnp.left_shift(jnp.int32(1), 31 - i)
        cand_f = key_to_float(cand)
        n = count(lambda s, kt: s >= cand_f)
        return jnp.where(n >= k_sel, cand, tau)
    tau = lax.fori_loop(0, 32, bit_body, jnp.full((1, tq), INT_MIN, jnp.int32))
    key_neg_inf = INT_MIN + 0x7FFFFF
    tau_f = jnp.where(tau <= key_neg_inf, neg_inf, key_to_float(tau))
    n_ge = count(lambda s, kt: s >= tau_f)
    has_ties = jnp.max(n_ge) > k_sel

    def write_mask(sel):
        def body(kt, carry):
            s = sc_ref[kt]
            keep = sel(s, kt) & (s > neg_inf)
            o_ref[kt] = jnp.where(keep, 0.0, NEG_BIG).astype(o_ref.dtype)
            return carry
        lax.fori_loop(0, n_kt, body, 0)

    @pl.when(jnp.logical_not(has_ties))
    def _():
        write_mask(lambda s, kt: s >= tau_f)

    @pl.when(has_ties)
    def _():
        need = k_sel - count(lambda s, kt: s > tau_f)

        def idx_body(i, cut):
            cand = cut + jnp.left_shift(jnp.int32(1), 12 - i)
            n = count(lambda s, kt: (s == tau_f) & (key_pos(kt) < cand))
            return jnp.where(n <= need, cand, cut)
        cut = lax.fori_loop(0, 13, idx_body, jnp.zeros((1, tq), jnp.int32))
        write_mask(lambda s, kt: (s > tau_f) | ((s == tau_f) & (key_pos(kt) < cut)))

    def fill_body(kt, carry):
        o_ref[kt] = jnp.full((tk, tq), NEG_BIG, o_ref.dtype)
        return carry
    lax.fori_loop(n_kt, nk, fill_body, 0)


def _indexer(q_idx, kpe, kpo, w_t, tq, tk):
    bsz, seq, _ = kpe.shape
    nq, nk = seq // tq, seq // tk
    k_sel = min(TOPK_MAX, seq // 4)
    assert tq >= k_sel and tq % CHUNK == 0 and tk % tq == 0
    qi_cols = q_idx.shape[1]
    return pl.pallas_call(
        functools.partial(_indexer_kernel, tq=tq, tk=tk, k_sel=k_sel),
        out_shape=jax.ShapeDtypeStruct((bsz, nq, nk, tk, tq), BF16),
        grid=(bsz, nq),
        in_specs=[
            pl.BlockSpec((tq, qi_cols), lambda b, t: (b * nq + t, 0)),
            pl.BlockSpec((None, seq, 2 * IDX_DIM), lambda b, t: (b, 0, 0)),
            pl.BlockSpec((None, seq, 2 * IDX_DIM), lambda b, t: (b, 0, 0)),
            pl.BlockSpec((None, IDX_HEADS, tq), lambda b, t: (b, 0, t)),
        ],
        out_specs=pl.BlockSpec((None, None, nk, tk, tq), lambda b, t: (b, t, 0, 0, 0)),
        scratch_shapes=[pltpu.VMEM((nk, tk, tq), F32)],
        compiler_params=_params("parallel", "parallel"),
        name="indexer",
    )(q_idx, kpe, kpo, w_t)


def _attention_kernel(q_ref, k_ref, vt_ref, bias_ref, o_ref, m_ref, l_ref, acc_ref, *, tq, tk, group):
    qt = pl.program_id(2)
    n_kt = _n_key_tiles(qt, tq, tk)
    m_ref[...] = jnp.full(m_ref.shape, NEG_BIG, F32)
    l_ref[...] = jnp.zeros(l_ref.shape, F32)
    acc_ref[...] = jnp.zeros(acc_ref.shape, F32)

    def body(kt, carry):
        k0 = pl.multiple_of(kt * tk, tk)
        bias = bias_ref[kt].astype(F32)
        heads = [slice(g * HEAD_DIM, (g + 1) * HEAD_DIM) for g in range(group)]
        scores = [_nt_dot(k_ref[pl.ds(k0, tk), cols], q_ref[:, cols]) + bias for cols in heads]
        probs, alphas = [], []
        for g, s in enumerate(scores):
            m_old = m_ref[g]
            m_new = jnp.maximum(m_old, jnp.max(s, axis=0, keepdims=True))
            alpha = jnp.exp(m_old - m_new)
            p = jnp.exp(s - m_new)
            m_ref[g] = m_new
            l_ref[g] = alpha * l_ref[g] + jnp.sum(p, axis=0, keepdims=True)
            probs.append(p.astype(BF16))
            alphas.append(alpha)
        for g, cols in enumerate(heads):
            acc_ref[g] = alphas[g] * acc_ref[g] + _dot(vt_ref[kt, cols, :], probs[g])
        return carry
    lax.fori_loop(0, n_kt, body, 0)

    for g in range(group):
        o_ref[:, g * HEAD_DIM:(g + 1) * HEAD_DIM] = (acc_ref[g] / l_ref[g]).T.astype(o_ref.dtype)


def _attention(q, k, vt, bias, tq, tk, group=4):
    bsz, nk, hd, _ = vt.shape
    seq = nk * tk
    nq = seq // tq
    gw = group * HEAD_DIM
    return pl.pallas_call(
        functools.partial(_attention_kernel, tq=tq, tk=tk, group=group),
        out_shape=jax.ShapeDtypeStruct((bsz * seq, hd), BF16),
        grid=(bsz, hd // gw, nq),
        in_specs=[
            pl.BlockSpec((tq, gw), lambda b, g, t: (b * nq + t, g)),
            pl.BlockSpec((None, seq, gw), lambda b, g, t: (b, 0, g)),
            pl.BlockSpec((None, nk, gw, tk), lambda b, g, t: (b, 0, g, 0)),
            pl.BlockSpec((None, None, nk, tk, tq), lambda b, g, t: (b, t, 0, 0, 0)),
        ],
        out_specs=pl.BlockSpec((tq, gw), lambda b, g, t: (b * nq + t, g)),
        scratch_shapes=[pltpu.VMEM((group, 1, tq), F32), pltpu.VMEM((group, 1, tq), F32),
                        pltpu.VMEM((group, HEAD_DIM, tq), F32)],
        compiler_params=_params("parallel", "parallel", "arbitrary"),
        name="attention",
    )(q, k.reshape(bsz, seq, hd), vt, bias)


def _merge_kernel(xn_ref, ya_ref, yb_ref, wga_ref, wgb_ref, woa_ref, wob_ref, o_ref):
    xn = xn_ref[...]
    ga = jax.nn.sigmoid(_dot(xn, wga_ref[...]))
    gb = jax.nn.sigmoid(_dot(xn, wgb_ref[...]))
    o_ref[...] = (ga * _dot(ya_ref[...], woa_ref[...]) + gb * _dot(yb_ref[...], wob_ref[...])).astype(o_ref.dtype)


def _merge(xn, ya, yb, w_ga, w_gb, w_oa, w_ob, tm=512, tn=512):
    m, d = xn.shape
    row = lambda k: pl.BlockSpec((tm, k), lambda i, j: (i, 0))
    colw = lambda k: pl.BlockSpec((k, tn), lambda i, j: (0, j))
    return pl.pallas_call(
        _merge_kernel,
        out_shape=jax.ShapeDtypeStruct((m, d), BF16),
        grid=(m // tm, d // tn),
        in_specs=[row(d), row(ya.shape[1]), row(yb.shape[1]),
                  colw(d), colw(d), colw(ya.shape[1]), colw(yb.shape[1])],
        out_specs=pl.BlockSpec((tm, tn), lambda i, j: (i, j)),
        compiler_params=_params("parallel", "arbitrary"),
        name="merge",
    )(xn, ya, yb, w_ga, w_gb, w_oa, w_ob)


def _out_kernel(x_ref, mix_ref, w_ref, g_ref, h_ref, hn_ref):
    h = x_ref[...] + _dot(mix_ref[...], w_ref[...])
    h_ref[...] = h
    hn_ref[...] = _rms(h, g_ref[...]).astype(hn_ref.dtype)


def _out_proj(x, mixed, w_out, g, tm=256):
    m, d = x.shape
    row = pl.BlockSpec((tm, d), lambda i: (i, 0))
    return pl.pallas_call(
        _out_kernel,
        out_shape=(jax.ShapeDtypeStruct((m, d), F32), jax.ShapeDtypeStruct((m, d), BF16)),
        grid=(m // tm,),
        in_specs=[row, row, pl.BlockSpec((d, d), lambda i: (0, 0)), pl.BlockSpec((1, d), lambda i: (0, 0))],
        out_specs=(row, row),
        compiler_params=_params("parallel"),
        name="out_proj",
    )(x, mixed, w_out, g.reshape(1, d))


def _ffn_kernel(hn_ref, h_ref, wg_ref, wu_ref, wd_ref, g_ref, o_ref, acc_ref, *, final_norm):
    j = pl.program_id(1)
    hn = hn_ref[...]
    t = (jax.nn.silu(_dot(hn, wg_ref[...])) * _dot(hn, wu_ref[...])).astype(BF16)
    part = _dot(t, wd_ref[...])

    @pl.when(j == 0)
    def _():
        acc_ref[...] = part

    @pl.when(j > 0)
    def _():
        acc_ref[...] += part

    @pl.when(j == pl.num_programs(1) - 1)
    def _():
        h2 = h_ref[...] + acc_ref[...]
        o_ref[...] = _rms(h2, g_ref[...]) if final_norm else h2


def _ffn(hn, h, w_gate, w_up, w_down, g, final_norm, tm=512, tf=512):
    m, d = hn.shape
    f = w_gate.shape[1]
    row = pl.BlockSpec((tm, d), lambda i, j: (i, 0))
    return pl.pallas_call(
        functools.partial(_ffn_kernel, final_norm=final_norm),
        out_shape=jax.ShapeDtypeStruct((m, d), F32),
        grid=(m // tm, f // tf),
        in_specs=[row, row,
                  pl.BlockSpec((d, tf), lambda i, j: (0, j)),
                  pl.BlockSpec((d, tf), lambda i, j: (0, j)),
                  pl.BlockSpec((tf, d), lambda i, j: (j, 0)),
                  pl.BlockSpec((1, d), lambda i, j: (0, 0))],
        out_specs=row,
        scratch_shapes=[pltpu.VMEM((tm, d), F32)],
        compiler_params=_params("parallel", "arbitrary"),
        name="ffn",
    )(hn, h, w_gate, w_up, w_down, g.reshape(1, d))


def kernel(x, norm1_g, w_in, a_ln_g, a_ln_b, a_w_s, a_b_s, kv_norm_g, w_uk, w_uv, w_oa, w_ob, w_out, norm2_g,
           w_ff_gate, w_ff_up, w_ff_down, final_g):
    bsz, seq, d = x.shape
    m = bsz * seq
    depth = norm1_g.shape[0]
    a_width = a_ln_g.shape[1]
    lat = kv_norm_g.shape[1]
    heads, head_dim = w_uk.shape[2], w_uk.shape[3]
    assert head_dim == HEAD_DIM
    hd = heads * head_dim
    qi = IDX_HEADS * IDX_DIM
    c_q = 2 * a_width
    c_kv = c_q + hd
    c_qi = c_kv + lat
    c_ki = c_qi + qi
    c_wi = c_ki + IDX_DIM
    c_g = c_wi + IDX_HEADS

    h = x.reshape(m, d)
    for l in range(depth):
        w = w_in[l]
        w_za = w[:, :c_q].astype(BF16)
        w_q = w[:, c_q:c_kv].astype(BF16)
        w_c = w[:, c_kv:c_qi].astype(BF16)
        w_qi = w[:, c_qi:c_ki].astype(BF16)
        w_kw = jnp.pad(w[:, c_ki:c_g], ((0, 0), (0, 128 - IDX_DIM - IDX_HEADS))).astype(BF16)
        w_ga = w[:, c_g:c_g + d].astype(BF16)
        w_gb = w[:, c_g + d:].astype(BF16)

        xn = _rmsnorm(h, norm1_g[l])
        y_a = _branch_a(xn, w_za, a_ln_g[l], a_ln_b[l], a_w_s[l], a_b_s[l])
        q = _matmul(xn, w_q, BF16, 1024, 512, "q_proj", scale=HEAD_DIM ** -0.5)
        q_idx = _matmul(xn, w_qi, BF16, 1024, 512, "qi_proj")
        kw = _matmul(xn, w_kw, F32, 1024, 128, "kw_proj")
        k_tok, v_t = _kv_proj(xn, w_c, kv_norm_g[l], w_uk[l].reshape(lat, hd).astype(BF16),
                              w_uv[l].reshape(lat, hd).T.astype(BF16), bsz, seq, ATT_TK)
        k_idx = kw[:, :IDX_DIM].astype(BF16).reshape(bsz, seq, IDX_DIM)
        zeros = jnp.zeros_like(k_idx)
        kpe = jnp.concatenate([k_idx, zeros], axis=-1)
        kpo = jnp.concatenate([zeros, k_idx], axis=-1)
        w_t = kw[:, IDX_DIM:IDX_DIM + IDX_HEADS].reshape(bsz, seq, IDX_HEADS).transpose(0, 2, 1)
        bias = _indexer(q_idx, kpe, kpo, w_t, ATT_TQ, ATT_TK)
        y_b = _attention(q, k_tok, v_t, bias, ATT_TQ, ATT_TK)
        mixed = _merge(xn, y_a, y_b, w_ga, w_gb, w_oa[l].astype(BF16), w_ob[l].astype(BF16))
        h, hn = _out_proj(h, mixed, w_out[l].astype(BF16), norm2_g[l])
        h = _ffn(hn, h, w_ff_gate[l].astype(BF16), w_ff_up[l].astype(BF16), w_ff_down[l].astype(BF16), final_g,
                 final_norm=(l == depth - 1))
    return h.reshape(bsz, seq, d)
```

```python
import functools

import jax
import jax.numpy as jnp
from jax import lax
from jax.experimental import pallas as pl
from jax.experimental.pallas import tpu as pltpu

EPS = 1e-6
CHUNK = 64
A_GROUPS = 8
A_BLOCK = 128
HEAD_DIM = 128
IDX_DIM = 64
IDX_HEADS = 16
TOPK_MAX = 256
ONES_ROWS = 16
LOG2E = 1.4426950408889634

VMEM_LIMIT_BYTES = 56 * 1024 * 1024
INT_MIN = -2147483648
ATT_TQ = 256
ATT_TK = 512
NEG_BIG = -1e30

F32 = jnp.float32
BF16 = jnp.bfloat16


def _params(*sem, flags=None):
    return pltpu.CompilerParams(dimension_semantics=sem, vmem_limit_bytes=VMEM_LIMIT_BYTES, flags=flags)


def _nt_dot(a, b):
    return lax.dot_general(a, b, (((1,), (1,)), ((), ())), preferred_element_type=F32)


def _dot(a, b):
    return jnp.dot(a, b, preferred_element_type=F32)


def _rms(x, g):
    return x * lax.rsqrt(jnp.mean(x * x, axis=-1, keepdims=True) + EPS) * g


def _rmsnorm_kernel(x_ref, g_ref, o_ref):
    o_ref[...] = _rms(x_ref[...], g_ref[...]).astype(o_ref.dtype)


def _rmsnorm(x, g, tm=512):
    m, d = x.shape
    return pl.pallas_call(
        _rmsnorm_kernel,
        out_shape=jax.ShapeDtypeStruct((m, d), BF16),
        grid=(m // tm,),
        in_specs=[pl.BlockSpec((tm, d), lambda i: (i, 0)), pl.BlockSpec((1, d), lambda i: (0, 0))],
        out_specs=pl.BlockSpec((tm, d), lambda i: (i, 0)),
        compiler_params=_params("parallel"),
        name="rmsnorm",
    )(x, g.reshape(1, d))


def _mm_kernel(a_ref, w_ref, o_ref, *, scale):
    o_ref[...] = (_dot(a_ref[...], w_ref[...]) * scale).astype(o_ref.dtype)


def _matmul(a, w, out_dtype, tm, tn, name, scale=1.0):
    m, k = a.shape
    n = w.shape[1]
    return pl.pallas_call(
        functools.partial(_mm_kernel, scale=scale),
        out_shape=jax.ShapeDtypeStruct((m, n), out_dtype),
        grid=(m // tm, n // tn),
        in_specs=[pl.BlockSpec((tm, k), lambda i, j: (i, 0)), pl.BlockSpec((k, tn), lambda i, j: (0, j))],
        out_specs=pl.BlockSpec((tm, tn), lambda i, j: (i, j)),
        compiler_params=_params("parallel", "arbitrary"),
        name=name,
    )(a, w)


def _branch_a_kernel(xn_ref, w_ref, lng_ref, lnb_ref, ws_ref, bst_ref, o_ref, z_ref, *, tm, tn, width):
    j = pl.program_id(1)
    z_ref[j] = _dot(xn_ref[...], w_ref[...])

    @pl.when(j == pl.num_programs(1) - 1)
    def _():
        per_half = width // tn
        gw = width // A_GROUPS
        row = lax.broadcasted_iota(jnp.int32, (A_BLOCK, A_BLOCK), 0)
        col = lax.broadcasted_iota(jnp.int32, (A_BLOCK, A_BLOCK), 1)
        causal = (col // CHUNK) <= (row // CHUNK)
        for r in range(tm // A_BLOCK):
            rows = slice(r * A_BLOCK, (r + 1) * A_BLOCK)
            v = jnp.concatenate([z_ref[per_half + c, rows, :] for c in range(per_half)], axis=-1)
            v = jax.nn.gelu(v)
            vc = v - jnp.mean(v, axis=-1, keepdims=True)
            vn = vc * lax.rsqrt(jnp.mean(vc * vc, axis=-1, keepdims=True) + EPS)
            vn = (vn * lng_ref[...] + lnb_ref[...]).astype(BF16)
            for g in range(A_GROUPS):
                wm = jnp.where(causal, ws_ref[g], 0.0).astype(BF16)
                sv = _dot(wm, vn[:, g * gw:(g + 1) * gw]) + bst_ref[:, g:g + 1]
                c, off = divmod(g * gw, tn)
                u = jax.nn.gelu(z_ref[c, rows, off:off + gw])
                o_ref[rows, g * gw:(g + 1) * gw] = (u * sv).astype(o_ref.dtype)


def _branch_a(xn, w_za, ln_g, ln_b, w_s, b_s, tm=512, tn=512):
    m, d = xn.shape
    width = w_za.shape[1] // 2
    assert (width // A_GROUPS) <= tn and tn % (width // A_GROUPS) == 0
    kern = functools.partial(_branch_a_kernel, tm=tm, tn=tn, width=width)
    return pl.pallas_call(
        kern,
        out_shape=jax.ShapeDtypeStruct((m, width), BF16),
        grid=(m // tm, 2 * width // tn),
        in_specs=[
            pl.BlockSpec((tm, d), lambda i, j: (i, 0)),
            pl.BlockSpec((d, tn), lambda i, j: (0, j)),
            pl.BlockSpec((1, width), lambda i, j: (0, 0)),
            pl.BlockSpec((1, width), lambda i, j: (0, 0)),
            pl.BlockSpec((A_GROUPS, A_BLOCK, A_BLOCK), lambda i, j: (0, 0, 0)),
            pl.BlockSpec((A_BLOCK, A_GROUPS), lambda i, j: (0, 0)),
        ],
        out_specs=pl.BlockSpec((tm, width), lambda i, j: (i, 0)),
        scratch_shapes=[pltpu.VMEM((2 * width // tn, tm, tn), F32)],
        compiler_params=_params("parallel", "arbitrary"),
        name="branch_a",
    )(xn, w_za, ln_g.reshape(1, width), ln_b.reshape(1, width), w_s, b_s.T)


def _kv_kernel(xn_ref, wc_ref, g_ref, wuk_ref, wuvt_ref, k_ref, vt_ref):
    c = _dot(xn_ref[...], wc_ref[...])
    cn = _rms(c, g_ref[...]).astype(BF16)
    k_ref[...] = _dot(cn, wuk_ref[...]).astype(k_ref.dtype)
    vt = _nt_dot(wuvt_ref[...], cn).astype(vt_ref.dtype)
    rows = HEAD_DIM + ONES_ROWS
    for h in range(vt.shape[0] // HEAD_DIM):
        vt_ref[h * rows:h * rows + HEAD_DIM, :] = vt[h * HEAD_DIM:(h + 1) * HEAD_DIM]
        vt_ref[h * rows + HEAD_DIM:(h + 1) * rows, :] = jnp.ones((ONES_ROWS, vt.shape[1]), vt_ref.dtype)


def _kv_proj(xn, w_c, g, w_uk, w_uv_t, bsz, seq, tk):
    m, d = xn.shape
    lat = w_c.shape[1]
    hd = w_uk.shape[1]
    nt = seq // tk
    vrows = hd // HEAD_DIM * (HEAD_DIM + ONES_ROWS)
    return pl.pallas_call(
        _kv_kernel,
        out_shape=(jax.ShapeDtypeStruct((m, hd), BF16), jax.ShapeDtypeStruct((bsz, nt, vrows, tk), BF16)),
        grid=(m // tk,),
        in_specs=[
            pl.BlockSpec((tk, d), lambda i: (i, 0)),
            pl.BlockSpec((d, lat), lambda i: (0, 0)),
            pl.BlockSpec((1, lat), lambda i: (0, 0)),
            pl.BlockSpec((lat, hd), lambda i: (0, 0)),
            pl.BlockSpec((hd, lat), lambda i: (0, 0)),
        ],
        out_specs=(pl.BlockSpec((tk, hd), lambda i: (i, 0)),
                   pl.BlockSpec((None, None, vrows, tk), lambda i: (i // nt, i % nt, 0, 0))),
        compiler_params=_params("parallel"),
        name="kv_proj",
    )(xn, w_c, g.reshape(1, lat), w_uk, w_uv_t)


def _n_key_tiles(qt, tq, tk):
    return lax.div(qt * tq + tq + tk - 1, tk)


def _indexer_kernel(qi_ref, kpe_ref, kpo_ref, wt_ref, o_ref, sc_ref, *, tq, tk, k_sel):
    qt = pl.program_id(1)
    n_kt = _n_key_tiles(qt, tq, tk)
    nk = sc_ref.shape[0]
    neg_inf = -jnp.inf

    def count(pred):
        rows = 16
        def body(kt, cnt):
            hit = jnp.where(pred(sc_ref[kt], kt), 1.0, 0.0)
            return cnt + jnp.sum(hit.reshape(tk // rows, rows, tq), axis=0)
        cnt = lax.fori_loop(0, n_kt, body, jnp.zeros((rows, tq), F32))
        return jnp.sum(cnt, axis=0, keepdims=True)

    def key_pos(kt):
        return kt * tk + lax.broadcasted_iota(jnp.int32, (tk, tq), 0)

    wt = wt_ref[...] * (IDX_DIM ** -0.5 * IDX_HEADS ** -0.5)
    q_chunk = (qt * tq + lax.broadcasted_iota(jnp.int32, (1, tq), 1)) // CHUNK

    def score_body(kt, carry):
        k0 = pl.multiple_of(kt * tk, tk)
        ke = kpe_ref[pl.ds(k0, tk), :]
        ko = kpo_ref[pl.ds(k0, tk), :]
        acc = jnp.zeros((tk, tq), F32)
        for j in range(IDX_HEADS // 2):
            qp = qi_ref[:, j * 128:(j + 1) * 128]
            acc = acc + wt[2 * j:2 * j + 1, :] * jnp.maximum(_nt_dot(ke, qp), 0.0)
            acc = acc + wt[2 * j + 1:2 * j + 2, :] * jnp.maximum(_nt_dot(ko, qp), 0.0)
        k_chunk = (k0 + lax.broadcasted_iota(jnp.int32, (tk, 1), 0)) // CHUNK
        sc_ref[kt] = jnp.where(k_chunk <= q_chunk, acc, neg_inf)
        return carry
    lax.fori_loop(0, n_kt, score_body, 0)

    def key_to_float(key):
        return lax.bitcast_convert_type(key ^ ((key >> 31) & 0x7FFFFFFF), F32)

    def bit_body(i, tau):
        cand = tau ^ jnp.left_shift(jnp.int32(1), 31 - i)
        cand_f = key_to_float(cand)
        n = count(lambda s, kt: s >= cand_f)
        return jnp.where(n >= k_sel, cand, tau)
    tau = lax.fori_loop(0, 32, bit_body, jnp.full((1, tq), INT_MIN, jnp.int32))
    key_neg_inf = INT_MIN + 0x7FFFFF
    tau_f = jnp.where(tau <= key_neg_inf, neg_inf, key_to_float(tau))
    n_ge = count(lambda s, kt: s >= tau_f)
    has_ties = jnp.max(n_ge) > k_sel

    def write_mask(sel):
        def body(kt, carry):
            s = sc_ref[kt]
            keep = sel(s, kt) & (s > neg_inf)
            o_ref[kt] = jnp.where(keep, 0.0, NEG_BIG).astype(o_ref.dtype)
            return carry
        lax.fori_loop(0, n_kt, body, 0)

    @pl.when(jnp.logical_not(has_ties))
    def _():
        write_mask(lambda s, kt: s >= tau_f)

    @pl.when(has_ties)
    def _():
        need = k_sel - count(lambda s, kt: s > tau_f)

        def idx_body(i, cut):
            cand = cut + jnp.left_shift(jnp.int32(1), 12 - i)
            n = count(lambda s, kt: (s == tau_f) & (key_pos(kt) < cand))
            return jnp.where(n <= need, cand, cut)
        cut = lax.fori_loop(0, 13, idx_body, jnp.zeros((1, tq), jnp.int32))
        write_mask(lambda s, kt: (s > tau_f) | ((s == tau_f) & (key_pos(kt) < cut)))

    def fill_body(kt, carry):
        o_ref[kt] = jnp.full((tk, tq), NEG_BIG, o_ref.dtype)
        return carry
    lax.fori_loop(n_kt, nk, fill_body, 0)


def _indexer(q_idx, kpe, kpo, w_t, tq, tk):
    bsz, seq, _ = kpe.shape
    nq, nk = seq // tq, seq // tk
    k_sel = min(TOPK_MAX, seq // 4)
    assert tq >= k_sel and tq % CHUNK == 0 and tk % tq == 0
    qi_cols = q_idx.shape[1]
    return pl.pallas_call(
        functools.partial(_indexer_kernel, tq=tq, tk=tk, k_sel=k_sel),
        out_shape=jax.ShapeDtypeStruct((bsz, nq, nk, tk, tq), F32),
        grid=(bsz, nq),
        in_specs=[
            pl.BlockSpec((tq, qi_cols), lambda b, t: (b * nq + t, 0)),
            pl.BlockSpec((None, seq, 2 * IDX_DIM), lambda b, t: (b, 0, 0)),
            pl.BlockSpec((None, seq, 2 * IDX_DIM), lambda b, t: (b, 0, 0)),
            pl.BlockSpec((None, IDX_HEADS, tq), lambda b, t: (b, 0, t)),
        ],
        out_specs=pl.BlockSpec((None, None, nk, tk, tq), lambda b, t: (b, t, 0, 0, 0)),
        scratch_shapes=[pltpu.VMEM((nk, tk, tq), F32)],
        compiler_params=_params("parallel", "parallel"),
        name="indexer",
    )(q_idx, kpe, kpo, w_t)


def _attention_kernel(q_ref, k_ref, vt_ref, bias_ref, o_ref, m_ref, acc_ref, s_ref, *, tq, tk, group):
    qt = pl.program_id(2)
    n_kt = _n_key_tiles(qt, tq, tk)
    rows = HEAD_DIM + ONES_ROWS
    m_ref[...] = jnp.full(m_ref.shape, NEG_BIG, F32)
    acc_ref[...] = jnp.zeros(acc_ref.shape, F32)

    def body(kt, carry):
        k0 = pl.multiple_of(kt * tk, tk)
        tile_max = []
        for g in range(group):
            cols = slice(g * HEAD_DIM, (g + 1) * HEAD_DIM)
            s = _nt_dot(k_ref[pl.ds(k0, tk), cols], q_ref[:, cols]) + bias_ref[kt]
            s_ref[g] = s
            tile_max.append(jnp.max(s, axis=0, keepdims=True))
        for g in range(group):
            m_old = m_ref[g]
            m_new = jnp.maximum(m_old, tile_max[g])
            m_ref[g] = m_new
            p = jnp.exp2(s_ref[g] - m_new).astype(BF16)
            pv = _dot(vt_ref[kt, g * rows:(g + 1) * rows, :], p)
            acc_ref[g] = jnp.exp2(m_old - m_new) * acc_ref[g] + pv
        return carry
    lax.fori_loop(0, n_kt, body, 0)

    for g in range(group):
        acc = acc_ref[g]
        out = acc[:HEAD_DIM] / acc[HEAD_DIM:HEAD_DIM + 1]
        o_ref[:, g * HEAD_DIM:(g + 1) * HEAD_DIM] = out.T.astype(o_ref.dtype)


def _attention(q, k, vt, bias, tq, tk, group=8):
    bsz, nk, vrows, _ = vt.shape
    rows = HEAD_DIM + ONES_ROWS
    hd = vrows // rows * HEAD_DIM
    seq = nk * tk
    nq = seq // tq
    gw = group * HEAD_DIM
    return pl.pallas_call(
        functools.partial(_attention_kernel, tq=tq, tk=tk, group=group),
        out_shape=jax.ShapeDtypeStruct((bsz * seq, hd), BF16),
        grid=(bsz, hd // gw, nq),
        in_specs=[
            pl.BlockSpec((tq, gw), lambda b, g, t: (b * nq + t, g)),
            pl.BlockSpec((None, seq, gw), lambda b, g, t: (b, 0, g)),
            pl.BlockSpec((None, nk, group * rows, tk), lambda b, g, t: (b, 0, g, 0)),
            pl.BlockSpec((None, None, nk, tk, tq), lambda b, g, t: (b, t, 0, 0, 0)),
        ],
        out_specs=pl.BlockSpec((tq, gw), lambda b, g, t: (b * nq + t, g)),
        scratch_shapes=[pltpu.VMEM((group, 1, tq), F32), pltpu.VMEM((group, rows, tq), F32),
                        pltpu.VMEM((group, tk, tq), F32)],
        compiler_params=_params("parallel", "parallel", "arbitrary"),
        name="attention",
    )(q, k.reshape(bsz, seq, hd), vt, bias)


def _merge_kernel(xn_ref, ya_ref, yb_ref, wga_ref, wgb_ref, woa_ref, wob_ref, o_ref):
    xn = xn_ref[...]
    ga = jax.nn.sigmoid(_dot(xn, wga_ref[...]))
    gb = jax.nn.sigmoid(_dot(xn, wgb_ref[...]))
    o_ref[...] = (ga * _dot(ya_ref[...], woa_ref[...]) + gb * _dot(yb_ref[...], wob_ref[...])).astype(o_ref.dtype)


def _merge(xn, ya, yb, w_ga, w_gb, w_oa, w_ob, tm=512, tn=512):
    m, d = xn.shape
    row = lambda k: pl.BlockSpec((tm, k), lambda i, j: (i, 0))
    colw = lambda k: pl.BlockSpec((k, tn), lambda i, j: (0, j))
    return pl.pallas_call(
        _merge_kernel,
        out_shape=jax.ShapeDtypeStruct((m, d), BF16),
        grid=(m // tm, d // tn),
        in_specs=[row(d), row(ya.shape[1]), row(yb.shape[1]),
                  colw(d), colw(d), colw(ya.shape[1]), colw(yb.shape[1])],
        out_specs=pl.BlockSpec((tm, tn), lambda i, j: (i, j)),
        compiler_params=_params("parallel", "arbitrary"),
        name="merge",
    )(xn, ya, yb, w_ga, w_gb, w_oa, w_ob)


def _out_kernel(x_ref, mix_ref, w_ref, g_ref, h_ref, hn_ref):
    h = x_ref[...] + _dot(mix_ref[...], w_ref[...])
    h_ref[...] = h
    hn_ref[...] = _rms(h, g_ref[...]).astype(hn_ref.dtype)


def _out_proj(x, mixed, w_out, g, tm=256):
    m, d = x.shape
    row = pl.BlockSpec((tm, d), lambda i: (i, 0))
    return pl.pallas_call(
        _out_kernel,
        out_shape=(jax.ShapeDtypeStruct((m, d), F32), jax.ShapeDtypeStruct((m, d), BF16)),
        grid=(m // tm,),
        in_specs=[row, row, pl.BlockSpec((d, d), lambda i: (0, 0)), pl.BlockSpec((1, d), lambda i: (0, 0))],
        out_specs=(row, row),
        compiler_params=_params("parallel"),
        name="out_proj",
    )(x, mixed, w_out, g.reshape(1, d))


def _ffn_kernel(hn_ref, h_ref, wg_ref, wu_ref, wd_ref, g_ref, o_ref, acc_ref, *, final_norm):
    j = pl.program_id(1)
    hn = hn_ref[...]
    t = (jax.nn.silu(_dot(hn, wg_ref[...])) * _dot(hn, wu_ref[...])).astype(BF16)
    part = _dot(t, wd_ref[...])

    @pl.when(j == 0)
    def _():
        acc_ref[...] = part

    @pl.when(j > 0)
    def _():
        acc_ref[...] += part

    @pl.when(j == pl.num_programs(1) - 1)
    def _():
        h2 = h_ref[...] + acc_ref[...]
        o_ref[...] = _rms(h2, g_ref[...]) if final_norm else h2


def _ffn(hn, h, w_gate, w_up, w_down, g, final_norm, tm=512, tf=512):
    m, d = hn.shape
    f = w_gate.shape[1]
    row = pl.BlockSpec((tm, d), lambda i, j: (i, 0))
    return pl.pallas_call(
        functools.partial(_ffn_kernel, final_norm=final_norm),
        out_shape=jax.ShapeDtypeStruct((m, d), F32),
        grid=(m // tm, f // tf),
        in_specs=[row, row,
                  pl.BlockSpec((d, tf), lambda i, j: (0, j)),
                  pl.BlockSpec((d, tf), lambda i, j: (0, j)),
                  pl.BlockSpec((tf, d), lambda i, j: (j, 0)),
                  pl.BlockSpec((1, d), lambda i, j: (0, 0))],
        out_specs=row,
        scratch_shapes=[pltpu.VMEM((tm, d), F32)],
        compiler_params=_params("parallel", "arbitrary"),
        name="ffn",
    )(hn, h, w_gate, w_up, w_down, g.reshape(1, d))


def kernel(x, norm1_g, w_in, a_ln_g, a_ln_b, a_w_s, a_b_s, kv_norm_g, w_uk, w_uv, w_oa, w_ob, w_out, norm2_g,
           w_ff_gate, w_ff_up, w_ff_down, final_g):
    bsz, seq, d = x.shape
    m = bsz * seq
    depth = norm1_g.shape[0]
    a_width = a_ln_g.shape[1]
    lat = kv_norm_g.shape[1]
    heads, head_dim = w_uk.shape[2], w_uk.shape[3]
    assert head_dim == HEAD_DIM
    hd = heads * head_dim
    qi = IDX_HEADS * IDX_DIM
    c_q = 2 * a_width
    c_kv = c_q + hd
    c_qi = c_kv + lat
    c_ki = c_qi + qi
    c_wi = c_ki + IDX_DIM
    c_g = c_wi + IDX_HEADS

    h = x.reshape(m, d)
    for l in range(depth):
        w = w_in[l]
        w_za = w[:, :c_q].astype(BF16)
        w_q = w[:, c_q:c_kv].astype(BF16)
        w_c = w[:, c_kv:c_qi].astype(BF16)
        w_qi = w[:, c_qi:c_ki].astype(BF16)
        w_kw = jnp.pad(w[:, c_ki:c_g], ((0, 0), (0, 128 - IDX_DIM - IDX_HEADS))).astype(BF16)
        w_ga = w[:, c_g:c_g + d].astype(BF16)
        w_gb = w[:, c_g + d:].astype(BF16)

        xn = _rmsnorm(h, norm1_g[l])
        y_a = _branch_a(xn, w_za, a_ln_g[l], a_ln_b[l], a_w_s[l], a_b_s[l])
        q = _matmul(xn, w_q, BF16, 1024, 512, "q_proj", scale=HEAD_DIM ** -0.5 * LOG2E)
        q_idx = _matmul(xn, w_qi, BF16, 1024, 512, "qi_proj")
        kw = _matmul(xn, w_kw, F32, 1024, 128, "kw_proj")
        k_tok, v_t = _kv_proj(xn, w_c, kv_norm_g[l], w_uk[l].reshape(lat, hd).astype(BF16),
                              w_uv[l].reshape(lat, hd).T.astype(BF16), bsz, seq, ATT_TK)
        k_idx = kw[:, :IDX_DIM].astype(BF16).reshape(bsz, seq, IDX_DIM)
        zeros = jnp.zeros_like(k_idx)
        kpe = jnp.concatenate([k_idx, zeros], axis=-1)
        kpo = jnp.concatenate([zeros, k_idx], axis=-1)
        w_t = kw[:, IDX_DIM:IDX_DIM + IDX_HEADS].reshape(bsz, seq, IDX_HEADS).transpose(0, 2, 1)
        bias = _indexer(q_idx, kpe, kpo, w_t, ATT_TQ, ATT_TK)
        y_b = _attention(q, k_tok, v_t, bias, ATT_TQ, ATT_TK)
        mixed = _merge(xn, y_a, y_b, w_ga, w_gb, w_oa[l].astype(BF16), w_ob[l].astype(BF16))
        h, hn = _out_proj(h, mixed, w_out[l].astype(BF16), norm2_g[l])
        h = _ffn(hn, h, w_ff_gate[l].astype(BF16), w_ff_up[l].astype(BF16), w_ff_down[l].astype(BF16), final_g,
                 final_norm=(l == depth - 1))
    return h.reshape(bsz, seq, d)
```

```python
import functools

import jax
import jax.numpy as jnp
from jax import lax
from jax.experimental import pallas as pl
from jax.experimental.pallas import tpu as pltpu

EPS = 1e-6
CHUNK = 64
A_GROUPS = 8
A_BLOCK = 128
HEAD_DIM = 128
IDX_DIM = 64
IDX_HEADS = 16
TOPK_MAX = 256
ONES_ROWS = 16
LOG2E = 1.4426950408889634

VMEM_LIMIT_BYTES = 56 * 1024 * 1024
INT_MIN = -2147483648
ATT_TQ = 256
ATT_TK = 512
NEG_BIG = -1e30

F32 = jnp.float32
BF16 = jnp.bfloat16


def _params(*sem, flags=None):
    return pltpu.CompilerParams(dimension_semantics=sem, vmem_limit_bytes=VMEM_LIMIT_BYTES, flags=flags)


def _nt_dot(a, b):
    return lax.dot_general(a, b, (((1,), (1,)), ((), ())), preferred_element_type=F32)


def _dot(a, b):
    return jnp.dot(a, b, preferred_element_type=F32)


def _rms(x, g):
    return x * lax.rsqrt(jnp.mean(x * x, axis=-1, keepdims=True) + EPS) * g


def _rmsnorm_kernel(x_ref, g_ref, o_ref):
    o_ref[...] = _rms(x_ref[...], g_ref[...]).astype(o_ref.dtype)


def _rmsnorm(x, g, tm=512):
    m, d = x.shape
    return pl.pallas_call(
        _rmsnorm_kernel,
        out_shape=jax.ShapeDtypeStruct((m, d), BF16),
        grid=(m // tm,),
        in_specs=[pl.BlockSpec((tm, d), lambda i: (i, 0)), pl.BlockSpec((1, d), lambda i: (0, 0))],
        out_specs=pl.BlockSpec((tm, d), lambda i: (i, 0)),
        compiler_params=_params("parallel"),
        name="rmsnorm",
    )(x, g.reshape(1, d))


def _mm_kernel(a_ref, w_ref, o_ref, *, scale):
    o_ref[...] = (_dot(a_ref[...], w_ref[...]) * scale).astype(o_ref.dtype)


def _matmul(a, w, col0, n, out_dtype, tm, tn, name, scale=1.0):
    m, k = a.shape
    assert col0 % tn == 0 and n % tn == 0
    j0 = col0 // tn
    return pl.pallas_call(
        functools.partial(_mm_kernel, scale=scale),
        out_shape=jax.ShapeDtypeStruct((m, n), out_dtype),
        grid=(m // tm, n // tn),
        in_specs=[pl.BlockSpec((tm, k), lambda i, j: (i, 0)), pl.BlockSpec((k, tn), lambda i, j: (0, j + j0))],
        out_specs=pl.BlockSpec((tm, tn), lambda i, j: (i, j)),
        compiler_params=_params("parallel", "arbitrary"),
        name=name,
    )(a, w)


def _branch_a_kernel(xn_ref, w_ref, lng_ref, lnb_ref, ws_ref, bst_ref, o_ref, z_ref, *, tm, tn, width):
    j = pl.program_id(1)
    z_ref[j] = jax.nn.gelu(_dot(xn_ref[...], w_ref[...]))

    @pl.when(j == pl.num_programs(1) - 1)
    def _():
        per_half = width // tn
        gw = width // A_GROUPS
        row = lax.broadcasted_iota(jnp.int32, (A_BLOCK, A_BLOCK), 0)
        col = lax.broadcasted_iota(jnp.int32, (A_BLOCK, A_BLOCK), 1)
        causal = (col // CHUNK) <= (row // CHUNK)
        for r in range(tm // A_BLOCK):
            rows = slice(r * A_BLOCK, (r + 1) * A_BLOCK)
            v = jnp.concatenate([z_ref[per_half + c, rows, :] for c in range(per_half)], axis=-1)
            vc = v - jnp.mean(v, axis=-1, keepdims=True)
            vn = vc * lax.rsqrt(jnp.mean(vc * vc, axis=-1, keepdims=True) + EPS)
            vn = (vn * lng_ref[...] + lnb_ref[...]).astype(BF16)
            for g in range(A_GROUPS):
                wm = jnp.where(causal, ws_ref[g], 0.0).astype(BF16)
                sv = _dot(wm, vn[:, g * gw:(g + 1) * gw]) + bst_ref[:, g:g + 1]
                c, off = divmod(g * gw, tn)
                u = z_ref[c, rows, off:off + gw]
                o_ref[rows, g * gw:(g + 1) * gw] = (u * sv).astype(o_ref.dtype)


def _branch_a(xn, w_za, ln_g, ln_b, w_s, b_s, tm=512, tn=512):
    m, d = xn.shape
    width = ln_g.shape[0]
    assert (width // A_GROUPS) <= tn and tn % (width // A_GROUPS) == 0
    kern = functools.partial(_branch_a_kernel, tm=tm, tn=tn, width=width)
    return pl.pallas_call(
        kern,
        out_shape=jax.ShapeDtypeStruct((m, width), BF16),
        grid=(m // tm, 2 * width // tn),
        in_specs=[
            pl.BlockSpec((tm, d), lambda i, j: (i, 0)),
            pl.BlockSpec((d, tn), lambda i, j: (0, j)),
            pl.BlockSpec((1, width), lambda i, j: (0, 0)),
            pl.BlockSpec((1, width), lambda i, j: (0, 0)),
            pl.BlockSpec((A_GROUPS, A_BLOCK, A_BLOCK), lambda i, j: (0, 0, 0)),
            pl.BlockSpec((A_BLOCK, A_GROUPS), lambda i, j: (0, 0)),
        ],
        out_specs=pl.BlockSpec((tm, width), lambda i, j: (i, 0)),
        scratch_shapes=[pltpu.VMEM((2 * width // tn, tm, tn), F32)],
        compiler_params=_params("parallel", "arbitrary"),
        name="branch_a",
    )(xn, w_za, ln_g.reshape(1, width), ln_b.reshape(1, width), w_s, b_s.T)


def _kv_kernel(xn_ref, wc_ref, g_ref, wuk_ref, wuvt_ref, k_ref, vt_ref):
    c = _dot(xn_ref[...], wc_ref[...])
    cn = _rms(c, g_ref[...]).astype(BF16)
    k_ref[...] = _dot(cn, wuk_ref[...]).astype(k_ref.dtype)
    vt = _nt_dot(wuvt_ref[...], cn).astype(vt_ref.dtype)
    rows = HEAD_DIM + ONES_ROWS
    for h in range(vt.shape[0] // HEAD_DIM):
        vt_ref[h * rows:h * rows + HEAD_DIM, :] = vt[h * HEAD_DIM:(h + 1) * HEAD_DIM]
        vt_ref[h * rows + HEAD_DIM:(h + 1) * rows, :] = jnp.ones((ONES_ROWS, vt.shape[1]), vt_ref.dtype)


def _kv_proj(xn, w_c, col0, g, w_uk, w_uv_t, bsz, seq, tk):
    m, d = xn.shape
    lat, hd = w_uk.shape
    assert col0 % lat == 0
    j0 = col0 // lat
    nt = seq // tk
    vrows = hd // HEAD_DIM * (HEAD_DIM + ONES_ROWS)
    return pl.pallas_call(
        _kv_kernel,
        out_shape=(jax.ShapeDtypeStruct((m, hd), BF16), jax.ShapeDtypeStruct((bsz, nt, vrows, tk), BF16)),
        grid=(m // tk,),
        in_specs=[
            pl.BlockSpec((tk, d), lambda i: (i, 0)),
            pl.BlockSpec((d, lat), lambda i: (0, j0)),
            pl.BlockSpec((1, lat), lambda i: (0, 0)),
            pl.BlockSpec((lat, hd), lambda i: (0, 0)),
            pl.BlockSpec((hd, lat), lambda i: (0, 0)),
        ],
        out_specs=(pl.BlockSpec((tk, hd), lambda i: (i, 0)),
                   pl.BlockSpec((None, None, vrows, tk), lambda i: (i // nt, i % nt, 0, 0))),
        compiler_params=_params("parallel"),
        name="kv_proj",
    )(xn, w_c, g.reshape(1, lat), w_uk, w_uv_t)


def _n_key_tiles(qt, tq, tk):
    return lax.div(qt * tq + tq + tk - 1, tk)


def _indexer_kernel(qi_ref, kpe_ref, kpo_ref, wt_ref, o_ref, sc_ref, *, tq, tk, k_sel):
    qt = pl.program_id(1)
    n_kt = _n_key_tiles(qt, tq, tk)
    nk = sc_ref.shape[0]
    neg_inf = -jnp.inf

    def count(pred):
        rows = 16
        def body(kt, cnt):
            hit = jnp.where(pred(sc_ref[kt], kt), 1.0, 0.0)
            return cnt + jnp.sum(hit.reshape(tk // rows, rows, tq), axis=0)
        cnt = lax.fori_loop(0, n_kt, body, jnp.zeros((rows, tq), F32))
        return jnp.sum(cnt, axis=0, keepdims=True)

    def key_pos(kt):
        return kt * tk + lax.broadcasted_iota(jnp.int32, (tk, tq), 0)

    wt = wt_ref[...] * (IDX_DIM ** -0.5 * IDX_HEADS ** -0.5)
    q_chunk = (qt * tq + lax.broadcasted_iota(jnp.int32, (1, tq), 1)) // CHUNK

    def score_body(kt, carry):
        k0 = pl.multiple_of(kt * tk, tk)
        ke = kpe_ref[pl.ds(k0, tk), :]
        ko = kpo_ref[pl.ds(k0, tk), :]
        acc = jnp.zeros((tk, tq), F32)
        for j in range(IDX_HEADS // 2):
            qp = qi_ref[:, j * 128:(j + 1) * 128]
            acc = acc + wt[2 * j:2 * j + 1, :] * jnp.maximum(_nt_dot(ke, qp), 0.0)
            acc = acc + wt[2 * j + 1:2 * j + 2, :] * jnp.maximum(_nt_dot(ko, qp), 0.0)
        k_chunk = (k0 + lax.broadcasted_iota(jnp.int32, (tk, 1), 0)) // CHUNK
        sc_ref[kt] = jnp.where(k_chunk <= q_chunk, acc, neg_inf)
        return carry
    lax.fori_loop(0, n_kt, score_body, 0)

    def key_to_float(key):
        return lax.bitcast_convert_type(key ^ ((key >> 31) & 0x7FFFFFFF), F32)

    def bit_body(i, tau):
        cand = tau ^ jnp.left_shift(jnp.int32(1), 31 - i)
        cand_f = key_to_float(cand)
        n = count(lambda s, kt: s >= cand_f)
        return jnp.where(n >= k_sel, cand, tau)
    tau = lax.fori_loop(0, 32, bit_body, jnp.full((1, tq), INT_MIN, jnp.int32))
    key_neg_inf = INT_MIN + 0x7FFFFF
    tau_f = jnp.where(tau <= key_neg_inf, neg_inf, key_to_float(tau))
    n_ge = count(lambda s, kt: s >= tau_f)
    has_ties = jnp.max(n_ge) > k_sel

    def write_mask(sel):
        def body(kt, carry):
            s = sc_ref[kt]
            keep = sel(s, kt) & (s > neg_inf)
            o_ref[kt] = jnp.where(keep, 0.0, NEG_BIG).astype(o_ref.dtype)
            return carry
        lax.fori_loop(0, n_kt, body, 0)

    @pl.when(jnp.logical_not(has_ties))
    def _():
        write_mask(lambda s, kt: s >= tau_f)

    @pl.when(has_ties)
    def _():
        need = k_sel - count(lambda s, kt: s > tau_f)

        def idx_body(i, cut):
            cand = cut + jnp.left_shift(jnp.int32(1), 12 - i)
            n = count(lambda s, kt: (s == tau_f) & (key_pos(kt) < cand))
            return jnp.where(n <= need, cand, cut)
        cut = lax.fori_loop(0, 13, idx_body, jnp.zeros((1, tq), jnp.int32))
        write_mask(lambda s, kt: (s > tau_f) | ((s == tau_f) & (key_pos(kt) < cut)))

    def fill_body(kt, carry):
        o_ref[kt] = jnp.full((tk, tq), NEG_BIG, o_ref.dtype)
        return carry
    lax.fori_loop(n_kt, nk, fill_body, 0)


def _indexer(q_idx, kpe, kpo, w_t, tq, tk):
    bsz, seq, _ = kpe.shape
    nq, nk = seq // tq, seq // tk
    k_sel = min(TOPK_MAX, seq // 4)
    assert tq >= k_sel and tq % CHUNK == 0 and tk % tq == 0
    qi_cols = q_idx.shape[1]
    return pl.pallas_call(
        functools.partial(_indexer_kernel, tq=tq, tk=tk, k_sel=k_sel),
        out_shape=jax.ShapeDtypeStruct((bsz, nq, nk, tk, tq), F32),
        grid=(bsz, nq),
        in_specs=[
            pl.BlockSpec((tq, qi_cols), lambda b, t: (b * nq + t, 0)),
            pl.BlockSpec((None, seq, 2 * IDX_DIM), lambda b, t: (b, 0, 0)),
            pl.BlockSpec((None, seq, 2 * IDX_DIM), lambda b, t: (b, 0, 0)),
            pl.BlockSpec((None, IDX_HEADS, tq), lambda b, t: (b, 0, t)),
        ],
        out_specs=pl.BlockSpec((None, None, nk, tk, tq), lambda b, t: (b, t, 0, 0, 0)),
        scratch_shapes=[pltpu.VMEM((nk, tk, tq), F32)],
        compiler_params=_params("parallel", "parallel"),
        name="indexer",
    )(q_idx, kpe, kpo, w_t)


def _attention_kernel(q_ref, k_ref, vt_ref, bias_ref, o_ref, m_ref, acc_ref, s_ref, *, tq, tk, group):
    qt = pl.program_id(2)
    n_kt = _n_key_tiles(qt, tq, tk)
    rows = HEAD_DIM + ONES_ROWS
    m_ref[...] = jnp.full(m_ref.shape, NEG_BIG, F32)
    acc_ref[...] = jnp.zeros(acc_ref.shape, F32)

    def body(kt, carry):
        k0 = pl.multiple_of(kt * tk, tk)
        tile_max = []
        for g in range(group):
            cols = slice(g * HEAD_DIM, (g + 1) * HEAD_DIM)
            s = _nt_dot(k_ref[pl.ds(k0, tk), cols], q_ref[:, cols]) + bias_ref[kt]
            s_ref[g] = s
            tile_max.append(jnp.max(s, axis=0, keepdims=True))
        for g in range(group):
            m_old = m_ref[g]
            m_new = jnp.maximum(m_old, tile_max[g])
            m_ref[g] = m_new
            p = jnp.exp2(s_ref[g] - m_new).astype(BF16)
            pv = _dot(vt_ref[kt, g * rows:(g + 1) * rows, :], p)
            acc_ref[g] = jnp.exp2(m_old - m_new) * acc_ref[g] + pv
        return carry
    lax.fori_loop(0, n_kt, body, 0)

    for g in range(group):
        acc = acc_ref[g]
        out = acc[:HEAD_DIM] / acc[HEAD_DIM:HEAD_DIM + 1]
        o_ref[:, g * HEAD_DIM:(g + 1) * HEAD_DIM] = out.T.astype(o_ref.dtype)


def _attention(q, k, vt, bias, tq, tk, group=8):
    bsz, nk, vrows, _ = vt.shape
    rows = HEAD_DIM + ONES_ROWS
    hd = vrows // rows * HEAD_DIM
    seq = nk * tk
    nq = seq // tq
    gw = group * HEAD_DIM
    return pl.pallas_call(
        functools.partial(_attention_kernel, tq=tq, tk=tk, group=group),
        out_shape=jax.ShapeDtypeStruct((bsz * seq, hd), BF16),
        grid=(bsz, hd // gw, nq),
        in_specs=[
            pl.BlockSpec((tq, gw), lambda b, g, t: (b * nq + t, g)),
            pl.BlockSpec((None, seq, gw), lambda b, g, t: (b, 0, g)),
            pl.BlockSpec((None, nk, group * rows, tk), lambda b, g, t: (b, 0, g, 0)),
            pl.BlockSpec((None, None, nk, tk, tq), lambda b, g, t: (b, t, 0, 0, 0)),
        ],
        out_specs=pl.BlockSpec((tq, gw), lambda b, g, t: (b * nq + t, g)),
        scratch_shapes=[pltpu.VMEM((group, 1, tq), F32), pltpu.VMEM((group, rows, tq), F32),
                        pltpu.VMEM((group, tk, tq), F32)],
        compiler_params=_params("parallel", "parallel", "arbitrary"),
        name="attention",
    )(q, k.reshape(bsz, seq, hd), vt, bias)


def _merge_kernel(xn_ref, ya_ref, yb_ref, wga_ref, wgb_ref, woa_ref, wob_ref, o_ref):
    xn = xn_ref[...]
    ga = jax.nn.sigmoid(_dot(xn, wga_ref[...]))
    gb = jax.nn.sigmoid(_dot(xn, wgb_ref[...]))
    o_ref[...] = (ga * _dot(ya_ref[...], woa_ref[...]) + gb * _dot(yb_ref[...], wob_ref[...])).astype(o_ref.dtype)


def _merge(xn, ya, yb, w_ga, w_gb, w_oa, w_ob, tm=512, tn=512):
    m, d = xn.shape
    row = lambda k: pl.BlockSpec((tm, k), lambda i, j: (i, 0))
    colw = lambda k: pl.BlockSpec((k, tn), lambda i, j: (0, j))
    return pl.pallas_call(
        _merge_kernel,
        out_shape=jax.ShapeDtypeStruct((m, d), BF16),
        grid=(m // tm, d // tn),
        in_specs=[row(d), row(ya.shape[1]), row(yb.shape[1]),
                  colw(d), colw(d), colw(ya.shape[1]), colw(yb.shape[1])],
        out_specs=pl.BlockSpec((tm, tn), lambda i, j: (i, j)),
        compiler_params=_params("parallel", "arbitrary"),
        name="merge",
    )(xn, ya, yb, w_ga, w_gb, w_oa, w_ob)


def _out_kernel(x_ref, mix_ref, w_ref, g_ref, h_ref, hn_ref):
    h = x_ref[...] + _dot(mix_ref[...], w_ref[...])
    h_ref[...] = h
    hn_ref[...] = _rms(h, g_ref[...]).astype(hn_ref.dtype)


def _out_proj(x, mixed, w_out, g, tm=256):
    m, d = x.shape
    row = pl.BlockSpec((tm, d), lambda i: (i, 0))
    return pl.pallas_call(
        _out_kernel,
        out_shape=(jax.ShapeDtypeStruct((m, d), F32), jax.ShapeDtypeStruct((m, d), BF16)),
        grid=(m // tm,),
        in_specs=[row, row, pl.BlockSpec((d, d), lambda i: (0, 0)), pl.BlockSpec((1, d), lambda i: (0, 0))],
        out_specs=(row, row),
        compiler_params=_params("parallel"),
        name="out_proj",
    )(x, mixed, w_out, g.reshape(1, d))


def _ffn_kernel(hn_ref, h_ref, wg_ref, wu_ref, wd_ref, g_ref, o_ref, acc_ref, *, final_norm):
    j = pl.program_id(1)

    @pl.when(j == 0)
    def _():
        acc_ref[...] = jnp.zeros(acc_ref.shape, F32)

    hn = hn_ref[...]
    t = (jax.nn.silu(_dot(hn, wg_ref[...])) * _dot(hn, wu_ref[...])).astype(BF16)
    acc_ref[...] += _dot(t, wd_ref[...])

    @pl.when(j == pl.num_programs(1) - 1)
    def _():
        h2 = h_ref[...] + acc_ref[...]
        o_ref[...] = _rms(h2, g_ref[...]) if final_norm else h2


def _ffn(hn, h, w_gate, w_up, w_down, g, final_norm, tm=512, tf=512):
    m, d = hn.shape
    f = w_gate.shape[1]
    row = pl.BlockSpec((tm, d), lambda i, j: (i, 0))
    return pl.pallas_call(
        functools.partial(_ffn_kernel, final_norm=final_norm),
        out_shape=jax.ShapeDtypeStruct((m, d), F32),
        grid=(m // tm, f // tf),
        in_specs=[row, row,
                  pl.BlockSpec((d, tf), lambda i, j: (0, j)),
                  pl.BlockSpec((d, tf), lambda i, j: (0, j)),
                  pl.BlockSpec((tf, d), lambda i, j: (j, 0)),
                  pl.BlockSpec((1, d), lambda i, j: (0, 0))],
        out_specs=row,
        scratch_shapes=[pltpu.VMEM((tm, d), F32)],
        compiler_params=_params("parallel", "arbitrary"),
        name="ffn",
    )(hn, h, w_gate, w_up, w_down, g.reshape(1, d))


def kernel(x, norm1_g, w_in, a_ln_g, a_ln_b, a_w_s, a_b_s, kv_norm_g, w_uk, w_uv, w_oa, w_ob, w_out, norm2_g,
           w_ff_gate, w_ff_up, w_ff_down, final_g):
    bsz, seq, d = x.shape
    m = bsz * seq
    depth = norm1_g.shape[0]
    a_width = a_ln_g.shape[1]
    lat = kv_norm_g.shape[1]
    heads, head_dim = w_uk.shape[2], w_uk.shape[3]
    assert head_dim == HEAD_DIM
    hd = heads * head_dim
    qi = IDX_HEADS * IDX_DIM
    c_q = 2 * a_width
    c_kv = c_q + hd
    c_qi = c_kv + lat
    c_ki = c_qi + qi
    c_wi = c_ki + IDX_DIM
    c_g = c_wi + IDX_HEADS

    h = x.reshape(m, d)
    for l in range(depth):
        w = w_in[l].astype(BF16)
        w_ga = w[:, c_g:c_g + d]
        w_gb = w[:, c_g + d:]

        xn = _rmsnorm(h, norm1_g[l])
        y_a = _branch_a(xn, w, a_ln_g[l], a_ln_b[l], a_w_s[l], a_b_s[l])
        q = _matmul(xn, w, c_q, hd, BF16, 1024, 512, "q_proj", scale=HEAD_DIM ** -0.5 * LOG2E)
        q_idx = _matmul(xn, w, c_qi, qi, BF16, 1024, 512, "qi_proj")
        kw = _matmul(xn, w, c_ki, 128, F32, 1024, 128, "kw_proj")
        k_tok, v_t = _kv_proj(xn, w, c_kv, kv_norm_g[l], w_uk[l].reshape(lat, hd).astype(BF16),
                              w_uv[l].reshape(lat, hd).T.astype(BF16), bsz, seq, ATT_TK)
        k_idx = kw[:, :IDX_DIM].astype(BF16).reshape(bsz, seq, IDX_DIM)
        zeros = jnp.zeros_like(k_idx)
        kpe = jnp.concatenate([k_idx, zeros], axis=-1)
        kpo = jnp.concatenate([zeros, k_idx], axis=-1)
        w_t = kw[:, IDX_DIM:IDX_DIM + IDX_HEADS].reshape(bsz, seq, IDX_HEADS).transpose(0, 2, 1)
        bias = _indexer(q_idx, kpe, kpo, w_t, ATT_TQ, ATT_TK)
        y_b = _attention(q, k_tok, v_t, bias, ATT_TQ, ATT_TK)
        mixed = _merge(xn, y_a, y_b, w_ga, w_gb, w_oa[l].astype(BF16), w_ob[l].astype(BF16))
        h, hn = _out_proj(h, mixed, w_out[l].astype(BF16), norm2_g[l])
        h = _ffn(hn, h, w_ff_gate[l].astype(BF16), w_ff_up[l].astype(BF16), w_ff_down[l].astype(BF16), final_g,
                 final_norm=(l == depth - 1))
    return h.reshape(bsz, seq, d)
```

```python
import functools

import jax
import jax.numpy as jnp
from jax import lax
from jax.experimental import pallas as pl
from jax.experimental.pallas import tpu as pltpu

EPS = 1e-6
CHUNK = 64
A_GROUPS = 8
A_BLOCK = 128
HEAD_DIM = 128
IDX_DIM = 64
IDX_HEADS = 16
TOPK_MAX = 256
ONES_ROWS = 16
LOG2E = 1.4426950408889634

VMEM_LIMIT_BYTES = 56 * 1024 * 1024
INT_MIN = -2147483648
ATT_TQ = 256
ATT_TK = 512
NEG_BIG = -1e30

F32 = jnp.float32
BF16 = jnp.bfloat16


def _params(*sem, flags=None):
    return pltpu.CompilerParams(dimension_semantics=sem, vmem_limit_bytes=VMEM_LIMIT_BYTES, flags=flags)


def _nt_dot(a, b):
    return lax.dot_general(a, b, (((1,), (1,)), ((), ())), preferred_element_type=F32)


def _dot(a, b):
    return jnp.dot(a, b, preferred_element_type=F32)


def _rms(x, g):
    return x * lax.rsqrt(jnp.mean(x * x, axis=-1, keepdims=True) + EPS) * g


def _rmsnorm_kernel(x_ref, g_ref, o_ref):
    o_ref[...] = _rms(x_ref[...], g_ref[...]).astype(o_ref.dtype)


def _rmsnorm(x, g, tm=512):
    m, d = x.shape
    return pl.pallas_call(
        _rmsnorm_kernel,
        out_shape=jax.ShapeDtypeStruct((m, d), BF16),
        grid=(m // tm,),
        in_specs=[pl.BlockSpec((tm, d), lambda i: (i, 0)), pl.BlockSpec((1, d), lambda i: (0, 0))],
        out_specs=pl.BlockSpec((tm, d), lambda i: (i, 0)),
        compiler_params=_params("parallel"),
        name="rmsnorm",
    )(x, g.reshape(1, d))


def _mm_kernel(a_ref, w_ref, o_ref, *, scale):
    o_ref[...] = (_dot(a_ref[...], w_ref[...]) * scale).astype(o_ref.dtype)


def _matmul(a, w, col0, n, out_dtype, tm, tn, name, scale=1.0):
    m, k = a.shape
    assert col0 % tn == 0 and n % tn == 0
    j0 = col0 // tn
    return pl.pallas_call(
        functools.partial(_mm_kernel, scale=scale),
        out_shape=jax.ShapeDtypeStruct((m, n), out_dtype),
        grid=(m // tm, n // tn),
        in_specs=[pl.BlockSpec((tm, k), lambda i, j: (i, 0)), pl.BlockSpec((k, tn), lambda i, j: (0, j + j0))],
        out_specs=pl.BlockSpec((tm, tn), lambda i, j: (i, j)),
        compiler_params=_params("parallel", "arbitrary"),
        name=name,
    )(a, w)


def _branch_a_kernel(xn_ref, w_ref, lng_ref, lnb_ref, ws_ref, bst_ref, o_ref, z_ref, *, tm, tn, width):
    j = pl.program_id(1)
    z_ref[j] = jax.nn.gelu(_dot(xn_ref[...], w_ref[...]))

    @pl.when(j == pl.num_programs(1) - 1)
    def _():
        per_half = width // tn
        gw = width // A_GROUPS
        row = lax.broadcasted_iota(jnp.int32, (A_BLOCK, A_BLOCK), 0)
        col = lax.broadcasted_iota(jnp.int32, (A_BLOCK, A_BLOCK), 1)
        causal = (col // CHUNK) <= (row // CHUNK)
        for r in range(tm // A_BLOCK):
            rows = slice(r * A_BLOCK, (r + 1) * A_BLOCK)
            v = jnp.concatenate([z_ref[per_half + c, rows, :] for c in range(per_half)], axis=-1)
            vc = v - jnp.mean(v, axis=-1, keepdims=True)
            vn = vc * lax.rsqrt(jnp.mean(vc * vc, axis=-1, keepdims=True) + EPS)
            vn = (vn * lng_ref[...] + lnb_ref[...]).astype(BF16)
            for g in range(A_GROUPS):
                wm = jnp.where(causal, ws_ref[g], 0.0).astype(BF16)
                sv = _dot(wm, vn[:, g * gw:(g + 1) * gw]) + bst_ref[:, g:g + 1]
                c, off = divmod(g * gw, tn)
                u = z_ref[c, rows, off:off + gw]
                o_ref[rows, g * gw:(g + 1) * gw] = (u * sv).astype(o_ref.dtype)


def _branch_a(xn, w_za, ln_g, ln_b, w_s, b_s, tm=512, tn=512):
    m, d = xn.shape
    width = ln_g.shape[0]
    assert (width // A_GROUPS) <= tn and tn % (width // A_GROUPS) == 0
    kern = functools.partial(_branch_a_kernel, tm=tm, tn=tn, width=width)
    return pl.pallas_call(
        kern,
        out_shape=jax.ShapeDtypeStruct((m, width), BF16),
        grid=(m // tm, 2 * width // tn),
        in_specs=[
            pl.BlockSpec((tm, d), lambda i, j: (i, 0)),
            pl.BlockSpec((d, tn), lambda i, j: (0, j)),
            pl.BlockSpec((1, width), lambda i, j: (0, 0)),
            pl.BlockSpec((1, width), lambda i, j: (0, 0)),
            pl.BlockSpec((A_GROUPS, A_BLOCK, A_BLOCK), lambda i, j: (0, 0, 0)),
            pl.BlockSpec((A_BLOCK, A_GROUPS), lambda i, j: (0, 0)),
        ],
        out_specs=pl.BlockSpec((tm, width), lambda i, j: (i, 0)),
        scratch_shapes=[pltpu.VMEM((2 * width // tn, tm, tn), F32)],
        compiler_params=_params("parallel", "arbitrary"),
        name="branch_a",
    )(xn, w_za, ln_g.reshape(1, width), ln_b.reshape(1, width), w_s, b_s.T)


def _kv_kernel(xn_ref, wc_ref, g_ref, wuk_ref, wuvt_ref, k_ref, vt_ref):
    c = _dot(xn_ref[...], wc_ref[...])
    cn = _rms(c, g_ref[...]).astype(BF16)
    k_ref[...] = _dot(cn, wuk_ref[...]).astype(k_ref.dtype)
    vt = _nt_dot(wuvt_ref[...], cn).astype(vt_ref.dtype)
    rows = HEAD_DIM + ONES_ROWS
    for h in range(vt.shape[0] // HEAD_DIM):
        vt_ref[h * rows:h * rows + HEAD_DIM, :] = vt[h * HEAD_DIM:(h + 1) * HEAD_DIM]
        vt_ref[h * rows + HEAD_DIM:(h + 1) * rows, :] = jnp.ones((ONES_ROWS, vt.shape[1]), vt_ref.dtype)


def _kv_proj(xn, w_c, col0, g, w_uk, w_uv_t, bsz, seq, tk):
    m, d = xn.shape
    lat, hd = w_uk.shape
    assert col0 % lat == 0
    j0 = col0 // lat
    nt = seq // tk
    vrows = hd // HEAD_DIM * (HEAD_DIM + ONES_ROWS)
    return pl.pallas_call(
        _kv_kernel,
        out_shape=(jax.ShapeDtypeStruct((m, hd), BF16), jax.ShapeDtypeStruct((bsz, nt, vrows, tk), BF16)),
        grid=(m // tk,),
        in_specs=[
            pl.BlockSpec((tk, d), lambda i: (i, 0)),
            pl.BlockSpec((d, lat), lambda i: (0, j0)),
            pl.BlockSpec((1, lat), lambda i: (0, 0)),
            pl.BlockSpec((lat, hd), lambda i: (0, 0)),
            pl.BlockSpec((hd, lat), lambda i: (0, 0)),
        ],
        out_specs=(pl.BlockSpec((tk, hd), lambda i: (i, 0)),
                   pl.BlockSpec((None, None, vrows, tk), lambda i: (i // nt, i % nt, 0, 0))),
        compiler_params=_params("parallel"),
        name="kv_proj",
    )(xn, w_c, g.reshape(1, lat), w_uk, w_uv_t)


def _n_key_tiles(qt, tq, tk):
    return lax.div(qt * tq + tq + tk - 1, tk)


def _indexer_kernel(qi_ref, kpe_ref, kpo_ref, wt_ref, o_ref, sc_ref, *, tq, tk, k_sel):
    qt = pl.program_id(1)
    n_kt = _n_key_tiles(qt, tq, tk)
    nk = sc_ref.shape[0]
    neg_inf = -jnp.inf

    def count(pred):
        rows = 16
        def body(kt, cnt):
            hit = jnp.where(pred(sc_ref[kt], kt), 1.0, 0.0)
            return cnt + jnp.sum(hit.reshape(tk // rows, rows, tq), axis=0)
        cnt = lax.fori_loop(0, n_kt, body, jnp.zeros((rows, tq), F32))
        return jnp.sum(cnt, axis=0, keepdims=True)

    def key_pos(kt):
        return kt * tk + lax.broadcasted_iota(jnp.int32, (tk, tq), 0)

    wt = wt_ref[...] * (IDX_DIM ** -0.5 * IDX_HEADS ** -0.5)
    q_chunk = (qt * tq + lax.broadcasted_iota(jnp.int32, (1, tq), 1)) // CHUNK

    def score_body(kt, carry):
        k0 = pl.multiple_of(kt * tk, tk)
        ke = kpe_ref[pl.ds(k0, tk), :]
        ko = kpo_ref[pl.ds(k0, tk), :]
        acc = jnp.zeros((tk, tq), F32)
        for j in range(IDX_HEADS // 2):
            qp = qi_ref[:, j * 128:(j + 1) * 128]
            acc = acc + wt[2 * j:2 * j + 1, :] * jnp.maximum(_nt_dot(ke, qp), 0.0)
            acc = acc + wt[2 * j + 1:2 * j + 2, :] * jnp.maximum(_nt_dot(ko, qp), 0.0)
        k_chunk = (k0 + lax.broadcasted_iota(jnp.int32, (tk, 1), 0)) // CHUNK
        sc_ref[kt] = jnp.where(k_chunk <= q_chunk, acc, neg_inf)
        return carry
    lax.fori_loop(0, n_kt, score_body, 0)

    def key_to_float(key):
        return lax.bitcast_convert_type(key ^ ((key >> 31) & 0x7FFFFFFF), F32)

    def bit_body(i, tau):
        cand = tau ^ jnp.left_shift(jnp.int32(1), 31 - i)
        cand_f = key_to_float(cand)
        n = count(lambda s, kt: s >= cand_f)
        return jnp.where(n >= k_sel, cand, tau)
    tau = lax.fori_loop(0, 32, bit_body, jnp.full((1, tq), INT_MIN, jnp.int32))
    key_neg_inf = INT_MIN + 0x7FFFFF
    tau_f = jnp.where(tau <= key_neg_inf, neg_inf, key_to_float(tau))
    n_ge = count(lambda s, kt: s >= tau_f)
    has_ties = jnp.max(n_ge) > k_sel

    def write_mask(sel):
        def body(kt, carry):
            s = sc_ref[kt]
            keep = sel(s, kt) & (s > neg_inf)
            o_ref[kt] = jnp.where(keep, 0.0, NEG_BIG).astype(o_ref.dtype)
            return carry
        lax.fori_loop(0, n_kt, body, 0)

    @pl.when(jnp.logical_not(has_ties))
    def _():
        write_mask(lambda s, kt: s >= tau_f)

    @pl.when(has_ties)
    def _():
        need = k_sel - count(lambda s, kt: s > tau_f)

        def idx_body(i, cut):
            cand = cut + jnp.left_shift(jnp.int32(1), 12 - i)
            n = count(lambda s, kt: (s == tau_f) & (key_pos(kt) < cand))
            return jnp.where(n <= need, cand, cut)
        cut = lax.fori_loop(0, 13, idx_body, jnp.zeros((1, tq), jnp.int32))
        write_mask(lambda s, kt: (s > tau_f) | ((s == tau_f) & (key_pos(kt) < cut)))

    def fill_body(kt, carry):
        o_ref[kt] = jnp.full((tk, tq), NEG_BIG, o_ref.dtype)
        return carry
    lax.fori_loop(n_kt, nk, fill_body, 0)


def _indexer(q_idx, kpe, kpo, w_t, tq, tk):
    bsz, seq, _ = kpe.shape
    nq, nk = seq // tq, seq // tk
    k_sel = min(TOPK_MAX, seq // 4)
    assert tq >= k_sel and tq % CHUNK == 0 and tk % tq == 0
    qi_cols = q_idx.shape[1]
    return pl.pallas_call(
        functools.partial(_indexer_kernel, tq=tq, tk=tk, k_sel=k_sel),
        out_shape=jax.ShapeDtypeStruct((bsz, nq, nk, tk, tq), F32),
        grid=(bsz, nq),
        in_specs=[
            pl.BlockSpec((tq, qi_cols), lambda b, t: (b * nq + t, 0)),
            pl.BlockSpec((None, seq, 2 * IDX_DIM), lambda b, t: (b, 0, 0)),
            pl.BlockSpec((None, seq, 2 * IDX_DIM), lambda b, t: (b, 0, 0)),
            pl.BlockSpec((None, IDX_HEADS, tq), lambda b, t: (b, 0, t)),
        ],
        out_specs=pl.BlockSpec((None, None, nk, tk, tq), lambda b, t: (b, t, 0, 0, 0)),
        scratch_shapes=[pltpu.VMEM((nk, tk, tq), F32)],
        compiler_params=_params("parallel", "parallel"),
        name="indexer",
    )(q_idx, kpe, kpo, w_t)


def _attention_kernel(q_ref, k_ref, vt_ref, bias_ref, o_ref, m_ref, acc_ref, sa_ref, sb_ref, ma_ref, mb_ref,
                      *, tq, tk, group):
    qt = pl.program_id(2)
    n_kt = _n_key_tiles(qt, tq, tk)
    rows = HEAD_DIM + ONES_ROWS
    m_ref[...] = jnp.full(m_ref.shape, NEG_BIG, F32)
    acc_ref[...] = jnp.zeros(acc_ref.shape, F32)

    def scores(kt, s_ref, mx_ref):
        k0 = pl.multiple_of(kt * tk, tk)
        for g in range(group):
            cols = slice(g * HEAD_DIM, (g + 1) * HEAD_DIM)
            s = _nt_dot(k_ref[pl.ds(k0, tk), cols], q_ref[:, cols]) + bias_ref[kt]
            s_ref[g] = s
            mx_ref[g] = jnp.max(s, axis=0, keepdims=True)

    def consume(kt, s_ref, mx_ref):
        for g in range(group):
            m_old = m_ref[g]
            m_new = jnp.maximum(m_old, mx_ref[g])
            m_ref[g] = m_new
            p = jnp.exp2(s_ref[g] - m_new).astype(BF16)
            pv = _dot(vt_ref[kt, g * rows:(g + 1) * rows, :], p)
            acc_ref[g] = jnp.exp2(m_old - m_new) * acc_ref[g] + pv

    scores(0, sa_ref, ma_ref)
    n_pairs = lax.div(n_kt - 1, 2)

    def pair_body(i, carry):
        kt = 2 * i
        scores(kt + 1, sb_ref, mb_ref)
        consume(kt, sa_ref, ma_ref)
        scores(kt + 2, sa_ref, ma_ref)
        consume(kt + 1, sb_ref, mb_ref)
        return carry
    lax.fori_loop(0, n_pairs, pair_body, 0)
    kt = 2 * n_pairs

    @pl.when(n_kt - kt == 1)
    def _():
        consume(kt, sa_ref, ma_ref)

    @pl.when(n_kt - kt == 2)
    def _():
        scores(kt + 1, sb_ref, mb_ref)
        consume(kt, sa_ref, ma_ref)
        consume(kt + 1, sb_ref, mb_ref)

    for g in range(group):
        acc = acc_ref[g]
        out = acc[:HEAD_DIM] / acc[HEAD_DIM:HEAD_DIM + 1]
        o_ref[:, g * HEAD_DIM:(g + 1) * HEAD_DIM] = out.T.astype(o_ref.dtype)


def _attention(q, k, vt, bias, tq, tk, group=8):
    bsz, nk, vrows, _ = vt.shape
    rows = HEAD_DIM + ONES_ROWS
    hd = vrows // rows * HEAD_DIM
    seq = nk * tk
    nq = seq // tq
    gw = group * HEAD_DIM
    return pl.pallas_call(
        functools.partial(_attention_kernel, tq=tq, tk=tk, group=group),
        out_shape=jax.ShapeDtypeStruct((bsz * seq, hd), BF16),
        grid=(bsz, hd // gw, nq),
        in_specs=[
            pl.BlockSpec((tq, gw), lambda b, g, t: (b * nq + t, g)),
            pl.BlockSpec((None, seq, gw), lambda b, g, t: (b, 0, g)),
            pl.BlockSpec((None, nk, group * rows, tk), lambda b, g, t: (b, 0, g, 0)),
            pl.BlockSpec((None, None, nk, tk, tq), lambda b, g, t: (b, t, 0, 0, 0)),
        ],
        out_specs=pl.BlockSpec((tq, gw), lambda b, g, t: (b * nq + t, g)),
        scratch_shapes=[pltpu.VMEM((group, 1, tq), F32), pltpu.VMEM((group, rows, tq), F32),
                        pltpu.VMEM((group, tk, tq), F32), pltpu.VMEM((group, tk, tq), F32),
                        pltpu.VMEM((group, 1, tq), F32), pltpu.VMEM((group, 1, tq), F32)],
        compiler_params=_params("parallel", "parallel", "arbitrary"),
        name="attention",
    )(q, k.reshape(bsz, seq, hd), vt, bias)


def _merge_kernel(xn_ref, ya_ref, yb_ref, wga_ref, wgb_ref, woa_ref, wob_ref, o_ref):
    xn = xn_ref[...]
    ga = jax.nn.sigmoid(_dot(xn, wga_ref[...]))
    gb = jax.nn.sigmoid(_dot(xn, wgb_ref[...]))
    o_ref[...] = (ga * _dot(ya_ref[...], woa_ref[...]) + gb * _dot(yb_ref[...], wob_ref[...])).astype(o_ref.dtype)


def _merge(xn, ya, yb, w_ga, w_gb, w_oa, w_ob, tm=512, tn=512):
    m, d = xn.shape
    row = lambda k: pl.BlockSpec((tm, k), lambda i, j: (i, 0))
    colw = lambda k: pl.BlockSpec((k, tn), lambda i, j: (0, j))
    return pl.pallas_call(
        _merge_kernel,
        out_shape=jax.ShapeDtypeStruct((m, d), BF16),
        grid=(m // tm, d // tn),
        in_specs=[row(d), row(ya.shape[1]), row(yb.shape[1]),
                  colw(d), colw(d), colw(ya.shape[1]), colw(yb.shape[1])],
        out_specs=pl.BlockSpec((tm, tn), lambda i, j: (i, j)),
        compiler_params=_params("parallel", "arbitrary"),
        name="merge",
    )(xn, ya, yb, w_ga, w_gb, w_oa, w_ob)


def _out_kernel(x_ref, mix_ref, w_ref, g_ref, h_ref, hn_ref):
    h = x_ref[...] + _dot(mix_ref[...], w_ref[...])
    h_ref[...] = h
    hn_ref[...] = _rms(h, g_ref[...]).astype(hn_ref.dtype)


def _out_proj(x, mixed, w_out, g, tm=256):
    m, d = x.shape
    row = pl.BlockSpec((tm, d), lambda i: (i, 0))
    return pl.pallas_call(
        _out_kernel,
        out_shape=(jax.ShapeDtypeStruct((m, d), F32), jax.ShapeDtypeStruct((m, d), BF16)),
        grid=(m // tm,),
        in_specs=[row, row, pl.BlockSpec((d, d), lambda i: (0, 0)), pl.BlockSpec((1, d), lambda i: (0, 0))],
        out_specs=(row, row),
        compiler_params=_params("parallel"),
        name="out_proj",
    )(x, mixed, w_out, g.reshape(1, d))


def _ffn_kernel(hn_ref, h_ref, wg_ref, wu_ref, wd_ref, g_ref, o_ref, acc_ref, *, final_norm):
    j = pl.program_id(1)

    @pl.when(j == 0)
    def _():
        acc_ref[...] = jnp.zeros(acc_ref.shape, F32)

    hn = hn_ref[...]
    t = (jax.nn.silu(_dot(hn, wg_ref[...])) * _dot(hn, wu_ref[...])).astype(BF16)
    acc_ref[...] += _dot(t, wd_ref[...])

    @pl.when(j == pl.num_programs(1) - 1)
    def _():
        h2 = h_ref[...] + acc_ref[...]
        o_ref[...] = _rms(h2, g_ref[...]) if final_norm else h2


def _ffn(hn, h, w_gate, w_up, w_down, g, final_norm, tm=512, tf=512):
    m, d = hn.shape
    f = w_gate.shape[1]
    row = pl.BlockSpec((tm, d), lambda i, j: (i, 0))
    return pl.pallas_call(
        functools.partial(_ffn_kernel, final_norm=final_norm),
        out_shape=jax.ShapeDtypeStruct((m, d), F32),
        grid=(m // tm, f // tf),
        in_specs=[row, row,
                  pl.BlockSpec((d, tf), lambda i, j: (0, j)),
                  pl.BlockSpec((d, tf), lambda i, j: (0, j)),
                  pl.BlockSpec((tf, d), lambda i, j: (j, 0)),
                  pl.BlockSpec((1, d), lambda i, j: (0, 0))],
        out_specs=row,
        scratch_shapes=[pltpu.VMEM((tm, d), F32)],
        compiler_params=_params("parallel", "arbitrary"),
        name="ffn",
    )(hn, h, w_gate, w_up, w_down, g.reshape(1, d))


def kernel(x, norm1_g, w_in, a_ln_g, a_ln_b, a_w_s, a_b_s, kv_norm_g, w_uk, w_uv, w_oa, w_ob, w_out, norm2_g,
           w_ff_gate, w_ff_up, w_ff_down, final_g):
    bsz, seq, d = x.shape
    m = bsz * seq
    depth = norm1_g.shape[0]
    a_width = a_ln_g.shape[1]
    lat = kv_norm_g.shape[1]
    heads, head_dim = w_uk.shape[2], w_uk.shape[3]
    assert head_dim == HEAD_DIM
    hd = heads * head_dim
    qi = IDX_HEADS * IDX_DIM
    c_q = 2 * a_width
    c_kv = c_q + hd
    c_qi = c_kv + lat
    c_ki = c_qi + qi
    c_wi = c_ki + IDX_DIM
    c_g = c_wi + IDX_HEADS

    h = x.reshape(m, d)
    for l in range(depth):
        w = w_in[l].astype(BF16)
        w_ga = w[:, c_g:c_g + d]
        w_gb = w[:, c_g + d:]

        xn = _rmsnorm(h, norm1_g[l])
        y_a = _branch_a(xn, w, a_ln_g[l], a_ln_b[l], a_w_s[l], a_b_s[l])
        q = _matmul(xn, w, c_q, hd, BF16, 1024, 512, "q_proj", scale=HEAD_DIM ** -0.5 * LOG2E)
        q_idx = _matmul(xn, w, c_qi, qi, BF16, 1024, 512, "qi_proj")
        kw = _matmul(xn, w, c_ki, 128, F32, 1024, 128, "kw_proj")
        k_tok, v_t = _kv_proj(xn, w, c_kv, kv_norm_g[l], w_uk[l].reshape(lat, hd).astype(BF16),
                              w_uv[l].reshape(lat, hd).T.astype(BF16), bsz, seq, ATT_TK)
        k_idx = kw[:, :IDX_DIM].astype(BF16).reshape(bsz, seq, IDX_DIM)
        zeros = jnp.zeros_like(k_idx)
        kpe = jnp.concatenate([k_idx, zeros], axis=-1)
        kpo = jnp.concatenate([zeros, k_idx], axis=-1)
        w_t = kw[:, IDX_DIM:IDX_DIM + IDX_HEADS].reshape(bsz, seq, IDX_HEADS).transpose(0, 2, 1)
        bias = _indexer(q_idx, kpe, kpo, w_t, ATT_TQ, ATT_TK)
        y_b = _attention(q, k_tok, v_t, bias, ATT_TQ, ATT_TK)
        mixed = _merge(xn, y_a, y_b, w_ga, w_gb, w_oa[l].astype(BF16), w_ob[l].astype(BF16))
        h, hn = _out_proj(h, mixed, w_out[l].astype(BF16), norm2_g[l])
        h = _ffn(hn, h, w_ff_gate[l].astype(BF16), w_ff_up[l].astype(BF16), w_ff_down[l].astype(BF16), final_g,
                 final_norm=(l == depth - 1))
    return h.reshape(bsz, seq, d)
```

```python
import functools

import jax
import jax.numpy as jnp
from jax import lax
from jax.experimental import pallas as pl
from jax.experimental.pallas import tpu as pltpu

EPS = 1e-6
CHUNK = 64
A_GROUPS = 8
A_BLOCK = 128
HEAD_DIM = 128
IDX_DIM = 64
IDX_HEADS = 16
TOPK_MAX = 256
ONES_ROWS = 16
LOG2E = 1.4426950408889634

VMEM_LIMIT_BYTES = 56 * 1024 * 1024
INT_MIN = -2147483648
ATT_TQ = 256
ATT_TK = 512
NEG_BIG = -1e30

F32 = jnp.float32
BF16 = jnp.bfloat16


def _params(*sem, flags=None):
    return pltpu.CompilerParams(dimension_semantics=sem, vmem_limit_bytes=VMEM_LIMIT_BYTES, flags=flags)


def _nt_dot(a, b):
    return lax.dot_general(a, b, (((1,), (1,)), ((), ())), preferred_element_type=F32)


def _dot(a, b):
    return jnp.dot(a, b, preferred_element_type=F32)


def _rms(x, g):
    return x * lax.rsqrt(jnp.mean(x * x, axis=-1, keepdims=True) + EPS) * g


def _mm_kernel(a_ref, w_ref, o_ref, *, scale):
    o_ref[...] = (_dot(a_ref[...], w_ref[...]) * scale).astype(o_ref.dtype)


def _matmul(a, w, col0, n, out_dtype, tm, tn, name, scale=1.0):
    m, k = a.shape
    assert col0 % tn == 0 and n % tn == 0
    j0 = col0 // tn
    return pl.pallas_call(
        functools.partial(_mm_kernel, scale=scale),
        out_shape=jax.ShapeDtypeStruct((m, n), out_dtype),
        grid=(m // tm, n // tn),
        in_specs=[pl.BlockSpec((tm, k), lambda i, j: (i, 0)), pl.BlockSpec((k, tn), lambda i, j: (0, j + j0))],
        out_specs=pl.BlockSpec((tm, tn), lambda i, j: (i, j)),
        compiler_params=_params("parallel", "arbitrary"),
        name=name,
    )(a, w)


def _branch_a_kernel(x_ref, g1_ref, w_ref, lng_ref, lnb_ref, ws_ref, bst_ref, o_ref, xn_ref, z_ref, *, tm, tn, width):
    j = pl.program_id(1)

    @pl.when(j == 0)
    def _():
        xn_ref[...] = _rms(x_ref[...], g1_ref[...]).astype(xn_ref.dtype)

    z_ref[j] = jax.nn.gelu(_dot(xn_ref[...], w_ref[...]))

    @pl.when(j == pl.num_programs(1) - 1)
    def _():
        per_half = width // tn
        gw = width // A_GROUPS
        row = lax.broadcasted_iota(jnp.int32, (A_BLOCK, A_BLOCK), 0)
        col = lax.broadcasted_iota(jnp.int32, (A_BLOCK, A_BLOCK), 1)
        causal = (col // CHUNK) <= (row // CHUNK)
        for r in range(tm // A_BLOCK):
            rows = slice(r * A_BLOCK, (r + 1) * A_BLOCK)
            v = jnp.concatenate([z_ref[per_half + c, rows, :] for c in range(per_half)], axis=-1)
            vc = v - jnp.mean(v, axis=-1, keepdims=True)
            vn = vc * lax.rsqrt(jnp.mean(vc * vc, axis=-1, keepdims=True) + EPS)
            vn = (vn * lng_ref[...] + lnb_ref[...]).astype(BF16)
            for g in range(A_GROUPS):
                wm = jnp.where(causal, ws_ref[g], 0.0).astype(BF16)
                sv = _dot(wm, vn[:, g * gw:(g + 1) * gw]) + bst_ref[:, g:g + 1]
                c, off = divmod(g * gw, tn)
                u = z_ref[c, rows, off:off + gw]
                o_ref[rows, g * gw:(g + 1) * gw] = (u * sv).astype(o_ref.dtype)


def _branch_a(x, g1, w_za, ln_g, ln_b, w_s, b_s, tm=512, tn=1024):
    m, d = x.shape
    width = ln_g.shape[0]
    assert (width // A_GROUPS) <= tn and tn % (width // A_GROUPS) == 0
    kern = functools.partial(_branch_a_kernel, tm=tm, tn=tn, width=width)
    return pl.pallas_call(
        kern,
        out_shape=(jax.ShapeDtypeStruct((m, width), BF16), jax.ShapeDtypeStruct((m, d), BF16)),
        grid=(m // tm, 2 * width // tn),
        in_specs=[
            pl.BlockSpec((tm, d), lambda i, j: (i, 0)),
            pl.BlockSpec((1, d), lambda i, j: (0, 0)),
            pl.BlockSpec((d, tn), lambda i, j: (0, j)),
            pl.BlockSpec((1, width), lambda i, j: (0, 0)),
            pl.BlockSpec((1, width), lambda i, j: (0, 0)),
            pl.BlockSpec((A_GROUPS, A_BLOCK, A_BLOCK), lambda i, j: (0, 0, 0)),
            pl.BlockSpec((A_BLOCK, A_GROUPS), lambda i, j: (0, 0)),
        ],
        out_specs=(pl.BlockSpec((tm, width), lambda i, j: (i, 0)), pl.BlockSpec((tm, d), lambda i, j: (i, 0))),
        scratch_shapes=[pltpu.VMEM((2 * width // tn, tm, tn), F32)],
        compiler_params=_params("parallel", "arbitrary"),
        name="branch_a",
    )(x, g1.reshape(1, d), w_za, ln_g.reshape(1, width), ln_b.reshape(1, width), w_s, b_s.T)


def _kv_kernel(xn_ref, wc_ref, g_ref, wuk_ref, wuvt_ref, k_ref, vt_ref):
    c = _dot(xn_ref[...], wc_ref[...])
    cn = _rms(c, g_ref[...]).astype(BF16)
    k_ref[...] = _dot(cn, wuk_ref[...]).astype(k_ref.dtype)
    vt = _nt_dot(wuvt_ref[...], cn).astype(vt_ref.dtype)
    rows = HEAD_DIM + ONES_ROWS
    for h in range(vt.shape[0] // HEAD_DIM):
        vt_ref[h * rows:h * rows + HEAD_DIM, :] = vt[h * HEAD_DIM:(h + 1) * HEAD_DIM]
        vt_ref[h * rows + HEAD_DIM:(h + 1) * rows, :] = jnp.ones((ONES_ROWS, vt.shape[1]), vt_ref.dtype)


def _kv_proj(xn, w_c, col0, g, w_uk, w_uv_t, bsz, seq, tk):
    m, d = xn.shape
    lat, hd = w_uk.shape
    assert col0 % lat == 0
    j0 = col0 // lat
    nt = seq // tk
    vrows = hd // HEAD_DIM * (HEAD_DIM + ONES_ROWS)
    return pl.pallas_call(
        _kv_kernel,
        out_shape=(jax.ShapeDtypeStruct((m, hd), BF16), jax.ShapeDtypeStruct((bsz, nt, vrows, tk), BF16)),
        grid=(m // tk,),
        in_specs=[
            pl.BlockSpec((tk, d), lambda i: (i, 0)),
            pl.BlockSpec((d, lat), lambda i: (0, j0)),
            pl.BlockSpec((1, lat), lambda i: (0, 0)),
            pl.BlockSpec((lat, hd), lambda i: (0, 0)),
            pl.BlockSpec((hd, lat), lambda i: (0, 0)),
        ],
        out_specs=(pl.BlockSpec((tk, hd), lambda i: (i, 0)),
                   pl.BlockSpec((None, None, vrows, tk), lambda i: (i // nt, i % nt, 0, 0))),
        compiler_params=_params("parallel"),
        name="kv_proj",
    )(xn, w_c, g.reshape(1, lat), w_uk, w_uv_t)


def _n_key_tiles(qt, tq, tk):
    return lax.div(qt * tq + tq + tk - 1, tk)


def _indexer_kernel(qi_ref, kpe_ref, kpo_ref, wt_ref, o_ref, sc_ref, *, tq, tk, k_sel):
    qt = pl.program_id(1)
    n_kt = _n_key_tiles(qt, tq, tk)
    nk = sc_ref.shape[0]
    neg_inf = -jnp.inf

    def count(pred):
        rows = 16
        def body(kt, cnt):
            hit = jnp.where(pred(sc_ref[kt], kt), 1.0, 0.0)
            return cnt + jnp.sum(hit.reshape(tk // rows, rows, tq), axis=0)
        cnt = lax.fori_loop(0, n_kt, body, jnp.zeros((rows, tq), F32))
        return jnp.sum(cnt, axis=0, keepdims=True)

    def key_pos(kt):
        return kt * tk + lax.broadcasted_iota(jnp.int32, (tk, tq), 0)

    wt = wt_ref[...] * (IDX_DIM ** -0.5 * IDX_HEADS ** -0.5)
    q_chunk = (qt * tq + lax.broadcasted_iota(jnp.int32, (1, tq), 1)) // CHUNK

    def score_tile(kt):
        k0 = pl.multiple_of(kt * tk, tk)
        ke = kpe_ref[pl.ds(k0, tk), :]
        ko = kpo_ref[pl.ds(k0, tk), :]
        acc = jnp.zeros((tk, tq), F32)
        for j in range(IDX_HEADS // 2):
            qp = qi_ref[:, j * 128:(j + 1) * 128]
            acc = acc + wt[2 * j:2 * j + 1, :] * jnp.maximum(_nt_dot(ke, qp), 0.0)
            acc = acc + wt[2 * j + 1:2 * j + 2, :] * jnp.maximum(_nt_dot(ko, qp), 0.0)
        k_chunk = (k0 + lax.broadcasted_iota(jnp.int32, (tk, 1), 0)) // CHUNK
        sc_ref[kt] = jnp.where(k_chunk <= q_chunk, acc, neg_inf)

    def score_pair(i, carry):
        score_tile(2 * i)
        score_tile(2 * i + 1)
        return carry
    lax.fori_loop(0, lax.div(n_kt, 2), score_pair, 0)

    @pl.when(lax.rem(n_kt, 2) == 1)
    def _():
        score_tile(n_kt - 1)

    def key_to_float(key):
        return lax.bitcast_convert_type(key ^ ((key >> 31) & 0x7FFFFFFF), F32)

    def bit_body(i, tau):
        cand = tau ^ jnp.left_shift(jnp.int32(1), 31 - i)
        cand_f = key_to_float(cand)
        n = count(lambda s, kt: s >= cand_f)
        return jnp.where(n >= k_sel, cand, tau)
    tau = lax.fori_loop(0, 32, bit_body, jnp.full((1, tq), INT_MIN, jnp.int32))
    key_neg_inf = INT_MIN + 0x7FFFFF
    tau_f = jnp.where(tau <= key_neg_inf, neg_inf, key_to_float(tau))
    n_ge = count(lambda s, kt: s >= tau_f)
    has_ties = jnp.max(n_ge) > k_sel

    def write_mask(sel):
        def body(kt, carry):
            s = sc_ref[kt]
            keep = sel(s, kt) & (s > neg_inf)
            o_ref[kt] = jnp.where(keep, 0.0, NEG_BIG).astype(o_ref.dtype)
            return carry
        lax.fori_loop(0, n_kt, body, 0)

    @pl.when(jnp.logical_not(has_ties))
    def _():
        write_mask(lambda s, kt: s >= tau_f)

    @pl.when(has_ties)
    def _():
        need = k_sel - count(lambda s, kt: s > tau_f)

        def idx_body(i, cut):
            cand = cut + jnp.left_shift(jnp.int32(1), 12 - i)
            n = count(lambda s, kt: (s == tau_f) & (key_pos(kt) < cand))
            return jnp.where(n <= need, cand, cut)
        cut = lax.fori_loop(0, 13, idx_body, jnp.zeros((1, tq), jnp.int32))
        write_mask(lambda s, kt: (s > tau_f) | ((s == tau_f) & (key_pos(kt) < cut)))

    def fill_body(kt, carry):
        o_ref[kt] = jnp.full((tk, tq), NEG_BIG, o_ref.dtype)
        return carry
    lax.fori_loop(n_kt, nk, fill_body, 0)


def _indexer(q_idx, kpe, kpo, w_t, tq, tk):
    bsz, seq, _ = kpe.shape
    nq, nk = seq // tq, seq // tk
    k_sel = min(TOPK_MAX, seq // 4)
    assert tq >= k_sel and tq % CHUNK == 0 and tk % tq == 0
    qi_cols = q_idx.shape[1]
    return pl.pallas_call(
        functools.partial(_indexer_kernel, tq=tq, tk=tk, k_sel=k_sel),
        out_shape=jax.ShapeDtypeStruct((bsz, nq, nk, tk, tq), F32),
        grid=(bsz, nq),
        in_specs=[
            pl.BlockSpec((tq, qi_cols), lambda b, t: (b * nq + t, 0)),
            pl.BlockSpec((None, seq, 2 * IDX_DIM), lambda b, t: (b, 0, 0)),
            pl.BlockSpec((None, seq, 2 * IDX_DIM), lambda b, t: (b, 0, 0)),
            pl.BlockSpec((None, IDX_HEADS, tq), lambda b, t: (b, 0, t)),
        ],
        out_specs=pl.BlockSpec((None, None, nk, tk, tq), lambda b, t: (b, t, 0, 0, 0)),
        scratch_shapes=[pltpu.VMEM((nk, tk, tq), F32)],
        compiler_params=_params("parallel", "parallel"),
        name="indexer",
    )(q_idx, kpe, kpo, w_t)


def _attention_kernel(q_ref, k_ref, vt_ref, bias_ref, o_ref, m_ref, acc_ref, sa_ref, sb_ref, ma_ref, mb_ref,
                      *, tq, tk, group):
    qt = pl.program_id(2)
    n_kt = _n_key_tiles(qt, tq, tk)
    rows = HEAD_DIM + ONES_ROWS
    m_ref[...] = jnp.full(m_ref.shape, NEG_BIG, F32)
    acc_ref[...] = jnp.zeros(acc_ref.shape, F32)

    def scores(kt, s_ref, mx_ref):
        k0 = pl.multiple_of(kt * tk, tk)
        for g in range(group):
            cols = slice(g * HEAD_DIM, (g + 1) * HEAD_DIM)
            s = _nt_dot(k_ref[pl.ds(k0, tk), cols], q_ref[:, cols]) + bias_ref[kt]
            s_ref[g] = s
            mx_ref[g] = jnp.max(s, axis=0, keepdims=True)

    def consume(kt, s_ref, mx_ref):
        for g in range(group):
            m_old = m_ref[g]
            m_new = jnp.maximum(m_old, mx_ref[g])
            m_ref[g] = m_new
            p = jnp.exp2(s_ref[g] - m_new).astype(BF16)
            pv = _dot(vt_ref[kt, g * rows:(g + 1) * rows, :], p)
            acc_ref[g] = jnp.exp2(m_old - m_new) * acc_ref[g] + pv

    scores(0, sa_ref, ma_ref)
    n_pairs = lax.div(n_kt - 1, 2)

    def pair_body(i, carry):
        kt = 2 * i
        scores(kt + 1, sb_ref, mb_ref)
        consume(kt, sa_ref, ma_ref)
        scores(kt + 2, sa_ref, ma_ref)
        consume(kt + 1, sb_ref, mb_ref)
        return carry
    lax.fori_loop(0, n_pairs, pair_body, 0)
    kt = 2 * n_pairs

    @pl.when(n_kt - kt == 1)
    def _():
        consume(kt, sa_ref, ma_ref)

    @pl.when(n_kt - kt == 2)
    def _():
        scores(kt + 1, sb_ref, mb_ref)
        consume(kt, sa_ref, ma_ref)
        consume(kt + 1, sb_ref, mb_ref)

    for g in range(group):
        acc = acc_ref[g]
        out = acc[:HEAD_DIM] / acc[HEAD_DIM:HEAD_DIM + 1]
        o_ref[:, g * HEAD_DIM:(g + 1) * HEAD_DIM] = out.T.astype(o_ref.dtype)


def _attention(q, k, vt, bias, tq, tk, group=8):
    bsz, nk, vrows, _ = vt.shape
    rows = HEAD_DIM + ONES_ROWS
    hd = vrows // rows * HEAD_DIM
    seq = nk * tk
    nq = seq // tq
    gw = group * HEAD_DIM
    return pl.pallas_call(
        functools.partial(_attention_kernel, tq=tq, tk=tk, group=group),
        out_shape=jax.ShapeDtypeStruct((bsz * seq, hd), BF16),
        grid=(bsz, hd // gw, nq),
        in_specs=[
            pl.BlockSpec((tq, gw), lambda b, g, t: (b * nq + t, g)),
            pl.BlockSpec((None, seq, gw), lambda b, g, t: (b, 0, g)),
            pl.BlockSpec((None, nk, group * rows, tk), lambda b, g, t: (b, 0, g, 0)),
            pl.BlockSpec((None, None, nk, tk, tq), lambda b, g, t: (b, t, 0, 0, 0)),
        ],
        out_specs=pl.BlockSpec((tq, gw), lambda b, g, t: (b * nq + t, g)),
        scratch_shapes=[pltpu.VMEM((group, 1, tq), F32), pltpu.VMEM((group, rows, tq), F32),
                        pltpu.VMEM((group, tk, tq), F32), pltpu.VMEM((group, tk, tq), F32),
                        pltpu.VMEM((group, 1, tq), F32), pltpu.VMEM((group, 1, tq), F32)],
        compiler_params=_params("parallel", "parallel", "arbitrary"),
        name="attention",
    )(q, k.reshape(bsz, seq, hd), vt, bias)


def _merge_kernel(xn_ref, ya_ref, yb_ref, wga_ref, wgb_ref, woa_ref, wob_ref, o_ref):
    xn = xn_ref[...]
    ga = jax.nn.sigmoid(_dot(xn, wga_ref[...]))
    gb = jax.nn.sigmoid(_dot(xn, wgb_ref[...]))
    o_ref[...] = (ga * _dot(ya_ref[...], woa_ref[...]) + gb * _dot(yb_ref[...], wob_ref[...])).astype(o_ref.dtype)


def _merge(xn, ya, yb, w_ga, w_gb, w_oa, w_ob, tm=1024, tn=512):
    m, d = xn.shape
    row = lambda k: pl.BlockSpec((tm, k), lambda i, j: (i, 0))
    colw = lambda k: pl.BlockSpec((k, tn), lambda i, j: (0, j))
    return pl.pallas_call(
        _merge_kernel,
        out_shape=jax.ShapeDtypeStruct((m, d), BF16),
        grid=(m // tm, d // tn),
        in_specs=[row(d), row(ya.shape[1]), row(yb.shape[1]),
                  colw(d), colw(d), colw(ya.shape[1]), colw(yb.shape[1])],
        out_specs=pl.BlockSpec((tm, tn), lambda i, j: (i, j)),
        compiler_params=_params("parallel", "arbitrary"),
        name="merge",
    )(xn, ya, yb, w_ga, w_gb, w_oa, w_ob)


def _out_kernel(x_ref, mix_ref, w_ref, g_ref, h_ref, hn_ref):
    h = x_ref[...] + _dot(mix_ref[...], w_ref[...])
    h_ref[...] = h
    hn_ref[...] = _rms(h, g_ref[...]).astype(hn_ref.dtype)


def _out_proj(x, mixed, w_out, g, tm=256):
    m, d = x.shape
    row = pl.BlockSpec((tm, d), lambda i: (i, 0))
    return pl.pallas_call(
        _out_kernel,
        out_shape=(jax.ShapeDtypeStruct((m, d), F32), jax.ShapeDtypeStruct((m, d), BF16)),
        grid=(m // tm,),
        in_specs=[row, row, pl.BlockSpec((d, d), lambda i: (0, 0)), pl.BlockSpec((1, d), lambda i: (0, 0))],
        out_specs=(row, row),
        compiler_params=_params("parallel"),
        name="out_proj",
    )(x, mixed, w_out, g.reshape(1, d))


def _ffn_kernel(hn_ref, h_ref, wg_ref, wu_ref, wd_ref, g_ref, o_ref, acc_ref, *, final_norm):
    j = pl.program_id(1)

    @pl.when(j == 0)
    def _():
        acc_ref[...] = jnp.zeros(acc_ref.shape, F32)

    hn = hn_ref[...]
    t = (jax.nn.silu(_dot(hn, wg_ref[...])) * _dot(hn, wu_ref[...])).astype(BF16)
    acc_ref[...] += _dot(t, wd_ref[...])

    @pl.when(j == pl.num_programs(1) - 1)
    def _():
        h2 = h_ref[...] + acc_ref[...]
        o_ref[...] = _rms(h2, g_ref[...]) if final_norm else h2


def _ffn(hn, h, w_gate, w_up, w_down, g, final_norm, tm=512, tf=512):
    m, d = hn.shape
    f = w_gate.shape[1]
    row = pl.BlockSpec((tm, d), lambda i, j: (i, 0))
    return pl.pallas_call(
        functools.partial(_ffn_kernel, final_norm=final_norm),
        out_shape=jax.ShapeDtypeStruct((m, d), F32),
        grid=(m // tm, f // tf),
        in_specs=[row, row,
                  pl.BlockSpec((d, tf), lambda i, j: (0, j)),
                  pl.BlockSpec((d, tf), lambda i, j: (0, j)),
                  pl.BlockSpec((tf, d), lambda i, j: (j, 0)),
                  pl.BlockSpec((1, d), lambda i, j: (0, 0))],
        out_specs=row,
        scratch_shapes=[pltpu.VMEM((tm, d), F32)],
        compiler_params=_params("parallel", "arbitrary"),
        name="ffn",
    )(hn, h, w_gate, w_up, w_down, g.reshape(1, d))


def kernel(x, norm1_g, w_in, a_ln_g, a_ln_b, a_w_s, a_b_s, kv_norm_g, w_uk, w_uv, w_oa, w_ob, w_out, norm2_g,
           w_ff_gate, w_ff_up, w_ff_down, final_g):
    bsz, seq, d = x.shape
    m = bsz * seq
    depth = norm1_g.shape[0]
    a_width = a_ln_g.shape[1]
    lat = kv_norm_g.shape[1]
    heads, head_dim = w_uk.shape[2], w_uk.shape[3]
    assert head_dim == HEAD_DIM
    hd = heads * head_dim
    qi = IDX_HEADS * IDX_DIM
    c_q = 2 * a_width
    c_kv = c_q + hd
    c_qi = c_kv + lat
    c_ki = c_qi + qi
    c_wi = c_ki + IDX_DIM
    c_g = c_wi + IDX_HEADS

    h = x.reshape(m, d)
    for l in range(depth):
        w = w_in[l].astype(BF16)
        w_ga = w[:, c_g:c_g + d]
        w_gb = w[:, c_g + d:]

        y_a, xn = _branch_a(h, norm1_g[l], w, a_ln_g[l], a_ln_b[l], a_w_s[l], a_b_s[l])
        q = _matmul(xn, w, c_q, hd, BF16, 1024, 512, "q_proj", scale=HEAD_DIM ** -0.5 * LOG2E)
        q_idx = _matmul(xn, w, c_qi, qi, BF16, 1024, 512, "qi_proj")
        kw = _matmul(xn, w, c_ki, 128, F32, 1024, 128, "kw_proj")
        k_tok, v_t = _kv_proj(xn, w, c_kv, kv_norm_g[l], w_uk[l].reshape(lat, hd).astype(BF16),
                              w_uv[l].reshape(lat, hd).T.astype(BF16), bsz, seq, ATT_TK)
        k_idx = kw[:, :IDX_DIM].astype(BF16).reshape(bsz, seq, IDX_DIM)
        zeros = jnp.zeros_like(k_idx)
        kpe = jnp.concatenate([k_idx, zeros], axis=-1)
        kpo = jnp.concatenate([zeros, k_idx], axis=-1)
        w_t = kw[:, IDX_DIM:IDX_DIM + IDX_HEADS].reshape(bsz, seq, IDX_HEADS).transpose(0, 2, 1)
        bias = _indexer(q_idx, kpe, kpo, w_t, ATT_TQ, ATT_TK)
        y_b = _attention(q, k_tok, v_t, bias, ATT_TQ, ATT_TK)
        mixed = _merge(xn, y_a, y_b, w_ga, w_gb, w_oa[l].astype(BF16), w_ob[l].astype(BF16))
        h, hn = _out_proj(h, mixed, w_out[l].astype(BF16), norm2_g[l])
        h = _ffn(hn, h, w_ff_gate[l].astype(BF16), w_ff_up[l].astype(BF16), w_ff_down[l].astype(BF16), final_g,
                 final_norm=(l == depth - 1))
    return h.reshape(bsz, seq, d)
```

```python
import functools

import jax
import jax.numpy as jnp
from jax import lax
from jax.experimental import pallas as pl
from jax.experimental.pallas import tpu as pltpu

EPS = 1e-6
CHUNK = 64
A_GROUPS = 8
A_BLOCK = 128
HEAD_DIM = 128
IDX_DIM = 64
IDX_HEADS = 16
TOPK_MAX = 256
ONES_ROWS = 16
LOG2E = 1.4426950408889634

VMEM_LIMIT_BYTES = 56 * 1024 * 1024
INT_MIN = -2147483648
ATT_TQ = 256
ATT_TK = 512
NEG_BIG = -1e30

F32 = jnp.float32
BF16 = jnp.bfloat16


def _params(*sem, flags=None):
    return pltpu.CompilerParams(dimension_semantics=sem, vmem_limit_bytes=VMEM_LIMIT_BYTES, flags=flags)


def _nt_dot(a, b):
    return lax.dot_general(a, b, (((1,), (1,)), ((), ())), preferred_element_type=F32)


def _dot(a, b):
    return jnp.dot(a, b, preferred_element_type=F32)


def _rms(x, g):
    return x * lax.rsqrt(jnp.mean(x * x, axis=-1, keepdims=True) + EPS) * g


def _mm_kernel(a_ref, w_ref, o_ref, *, scale):
    o_ref[...] = (_dot(a_ref[...], w_ref[...]) * scale).astype(o_ref.dtype)


def _matmul(a, w, col0, n, out_dtype, tm, tn, name, scale=1.0):
    m, k = a.shape
    assert col0 % tn == 0 and n % tn == 0
    j0 = col0 // tn
    return pl.pallas_call(
        functools.partial(_mm_kernel, scale=scale),
        out_shape=jax.ShapeDtypeStruct((m, n), out_dtype),
        grid=(m // tm, n // tn),
        in_specs=[pl.BlockSpec((tm, k), lambda i, j: (i, 0)), pl.BlockSpec((k, tn), lambda i, j: (0, j + j0))],
        out_specs=pl.BlockSpec((tm, tn), lambda i, j: (i, j)),
        compiler_params=_params("parallel", "arbitrary"),
        name=name,
    )(a, w)


def _branch_a_kernel(x_ref, g1_ref, w_ref, lng_ref, lnb_ref, ws_ref, bst_ref, o_ref, xn_ref, z_ref, *, tm, tn, width):
    j = pl.program_id(1)

    @pl.when(j == 0)
    def _():
        xn_ref[...] = _rms(x_ref[...], g1_ref[...]).astype(xn_ref.dtype)

    z_ref[j] = jax.nn.gelu(_dot(xn_ref[...], w_ref[...]))

    @pl.when(j == pl.num_programs(1) - 1)
    def _():
        per_half = width // tn
        gw = width // A_GROUPS
        row = lax.broadcasted_iota(jnp.int32, (A_BLOCK, A_BLOCK), 0)
        col = lax.broadcasted_iota(jnp.int32, (A_BLOCK, A_BLOCK), 1)
        causal = (col // CHUNK) <= (row // CHUNK)
        for r in range(tm // A_BLOCK):
            rows = slice(r * A_BLOCK, (r + 1) * A_BLOCK)
            v = jnp.concatenate([z_ref[per_half + c, rows, :] for c in range(per_half)], axis=-1)
            vc = v - jnp.mean(v, axis=-1, keepdims=True)
            vn = vc * lax.rsqrt(jnp.mean(vc * vc, axis=-1, keepdims=True) + EPS)
            vn = (vn * lng_ref[...] + lnb_ref[...]).astype(BF16)
            for g in range(A_GROUPS):
                wm = jnp.where(causal, ws_ref[g], 0.0).astype(BF16)
                sv = _dot(wm, vn[:, g * gw:(g + 1) * gw]) + bst_ref[:, g:g + 1]
                c, off = divmod(g * gw, tn)
                u = z_ref[c, rows, off:off + gw]
                o_ref[rows, g * gw:(g + 1) * gw] = (u * sv).astype(o_ref.dtype)


def _branch_a(x, g1, w_za, ln_g, ln_b, w_s, b_s, tm=512, tn=1024):
    m, d = x.shape
    width = ln_g.shape[0]
    assert (width // A_GROUPS) <= tn and tn % (width // A_GROUPS) == 0
    kern = functools.partial(_branch_a_kernel, tm=tm, tn=tn, width=width)
    return pl.pallas_call(
        kern,
        out_shape=(jax.ShapeDtypeStruct((m, width), BF16), jax.ShapeDtypeStruct((m, d), BF16)),
        grid=(m // tm, 2 * width // tn),
        in_specs=[
            pl.BlockSpec((tm, d), lambda i, j: (i, 0)),
            pl.BlockSpec((1, d), lambda i, j: (0, 0)),
            pl.BlockSpec((d, tn), lambda i, j: (0, j)),
            pl.BlockSpec((1, width), lambda i, j: (0, 0)),
            pl.BlockSpec((1, width), lambda i, j: (0, 0)),
            pl.BlockSpec((A_GROUPS, A_BLOCK, A_BLOCK), lambda i, j: (0, 0, 0)),
            pl.BlockSpec((A_BLOCK, A_GROUPS), lambda i, j: (0, 0)),
        ],
        out_specs=(pl.BlockSpec((tm, width), lambda i, j: (i, 0)), pl.BlockSpec((tm, d), lambda i, j: (i, 0))),
        scratch_shapes=[pltpu.VMEM((2 * width // tn, tm, tn), F32)],
        compiler_params=_params("parallel", "arbitrary"),
        name="branch_a",
    )(x, g1.reshape(1, d), w_za, ln_g.reshape(1, width), ln_b.reshape(1, width), w_s, b_s.T)


def _kv_kernel(xn_ref, wc_ref, g_ref, wuk_ref, wuvt_ref, k_ref, vt_ref):
    c = _dot(xn_ref[...], wc_ref[...])
    cn = _rms(c, g_ref[...]).astype(BF16)
    k_ref[...] = _dot(cn, wuk_ref[...]).astype(k_ref.dtype)
    vt = _nt_dot(wuvt_ref[...], cn).astype(vt_ref.dtype)
    rows = HEAD_DIM + ONES_ROWS
    for h in range(vt.shape[0] // HEAD_DIM):
        vt_ref[h * rows:h * rows + HEAD_DIM, :] = vt[h * HEAD_DIM:(h + 1) * HEAD_DIM]
        vt_ref[h * rows + HEAD_DIM:(h + 1) * rows, :] = jnp.ones((ONES_ROWS, vt.shape[1]), vt_ref.dtype)


def _kv_proj(xn, w_c, col0, g, w_uk, w_uv_t, bsz, seq, tk):
    m, d = xn.shape
    lat, hd = w_uk.shape
    assert col0 % lat == 0
    j0 = col0 // lat
    nt = seq // tk
    vrows = hd // HEAD_DIM * (HEAD_DIM + ONES_ROWS)
    return pl.pallas_call(
        _kv_kernel,
        out_shape=(jax.ShapeDtypeStruct((m, hd), BF16), jax.ShapeDtypeStruct((bsz, nt, vrows, tk), BF16)),
        grid=(m // tk,),
        in_specs=[
            pl.BlockSpec((tk, d), lambda i: (i, 0)),
            pl.BlockSpec((d, lat), lambda i: (0, j0)),
            pl.BlockSpec((1, lat), lambda i: (0, 0)),
            pl.BlockSpec((lat, hd), lambda i: (0, 0)),
            pl.BlockSpec((hd, lat), lambda i: (0, 0)),
        ],
        out_specs=(pl.BlockSpec((tk, hd), lambda i: (i, 0)),
                   pl.BlockSpec((None, None, vrows, tk), lambda i: (i // nt, i % nt, 0, 0))),
        compiler_params=_params("parallel"),
        name="kv_proj",
    )(xn, w_c, g.reshape(1, lat), w_uk, w_uv_t)


def _n_key_tiles(qt, tq, tk):
    return lax.div(qt * tq + tq + tk - 1, tk)


def _indexer_kernel(qi_ref, kpe_ref, kpo_ref, wt_ref, *refs, tq, tk, k_sel, n_cast):
    cast_in, o_ref, cast_out, sc_ref = refs[:n_cast], refs[n_cast], refs[n_cast + 1:2 * n_cast + 1], refs[-1]
    for w_ref, wb_ref in zip(cast_in, cast_out):
        wb_ref[...] = w_ref[...].astype(wb_ref.dtype)

    qt = pl.program_id(1)
    n_kt = _n_key_tiles(qt, tq, tk)
    nk = sc_ref.shape[0]
    neg_inf = -jnp.inf

    def count(pred):
        rows = 16
        def body(kt, cnt):
            hit = jnp.where(pred(sc_ref[kt], kt), 1.0, 0.0)
            return cnt + jnp.sum(hit.reshape(tk // rows, rows, tq), axis=0)
        cnt = lax.fori_loop(0, n_kt, body, jnp.zeros((rows, tq), F32))
        return jnp.sum(cnt, axis=0, keepdims=True)

    def key_pos(kt):
        return kt * tk + lax.broadcasted_iota(jnp.int32, (tk, tq), 0)

    wt = wt_ref[...] * (IDX_DIM ** -0.5 * IDX_HEADS ** -0.5)
    q_chunk = (qt * tq + lax.broadcasted_iota(jnp.int32, (1, tq), 1)) // CHUNK

    def score_tile(kt):
        k0 = pl.multiple_of(kt * tk, tk)
        ke = kpe_ref[pl.ds(k0, tk), :]
        ko = kpo_ref[pl.ds(k0, tk), :]
        acc = jnp.zeros((tk, tq), F32)
        for j in range(IDX_HEADS // 2):
            qp = qi_ref[:, j * 128:(j + 1) * 128]
            acc = acc + wt[2 * j:2 * j + 1, :] * jnp.maximum(_nt_dot(ke, qp), 0.0)
            acc = acc + wt[2 * j + 1:2 * j + 2, :] * jnp.maximum(_nt_dot(ko, qp), 0.0)
        k_chunk = (k0 + lax.broadcasted_iota(jnp.int32, (tk, 1), 0)) // CHUNK
        sc_ref[kt] = jnp.where(k_chunk <= q_chunk, acc, neg_inf)

    def score_pair(i, carry):
        score_tile(2 * i)
        score_tile(2 * i + 1)
        return carry
    lax.fori_loop(0, lax.div(n_kt, 2), score_pair, 0)

    @pl.when(lax.rem(n_kt, 2) == 1)
    def _():
        score_tile(n_kt - 1)

    def key_to_float(key):
        return lax.bitcast_convert_type(key ^ ((key >> 31) & 0x7FFFFFFF), F32)

    def bit_body(i, tau):
        cand = tau ^ jnp.left_shift(jnp.int32(1), 31 - i)
        cand_f = key_to_float(cand)
        n = count(lambda s, kt: s >= cand_f)
        return jnp.where(n >= k_sel, cand, tau)
    tau = lax.fori_loop(0, 32, bit_body, jnp.full((1, tq), INT_MIN, jnp.int32))
    key_neg_inf = INT_MIN + 0x7FFFFF
    tau_f = jnp.where(tau <= key_neg_inf, neg_inf, key_to_float(tau))
    n_ge = count(lambda s, kt: s >= tau_f)
    has_ties = jnp.max(n_ge) > k_sel

    def write_mask(sel):
        def body(kt, carry):
            s = sc_ref[kt]
            keep = sel(s, kt) & (s > neg_inf)
            o_ref[kt] = jnp.where(keep, 0.0, NEG_BIG).astype(o_ref.dtype)
            return carry
        lax.fori_loop(0, n_kt, body, 0)

    @pl.when(jnp.logical_not(has_ties))
    def _():
        write_mask(lambda s, kt: s >= tau_f)

    @pl.when(has_ties)
    def _():
        need = k_sel - count(lambda s, kt: s > tau_f)

        def idx_body(i, cut):
            cand = cut + jnp.left_shift(jnp.int32(1), 12 - i)
            n = count(lambda s, kt: (s == tau_f) & (key_pos(kt) < cand))
            return jnp.where(n <= need, cand, cut)
        cut = lax.fori_loop(0, 13, idx_body, jnp.zeros((1, tq), jnp.int32))
        write_mask(lambda s, kt: (s > tau_f) | ((s == tau_f) & (key_pos(kt) < cut)))

    def fill_body(kt, carry):
        o_ref[kt] = jnp.full((tk, tq), NEG_BIG, o_ref.dtype)
        return carry
    lax.fori_loop(n_kt, nk, fill_body, 0)


def _indexer(q_idx, kpe, kpo, w_t, tq, tk, cast_weights):
    bsz, seq, _ = kpe.shape
    nq, nk = seq // tq, seq // tk
    k_sel = min(TOPK_MAX, seq // 4)
    assert tq >= k_sel and tq % CHUNK == 0 and tk % tq == 0
    qi_cols = q_idx.shape[1]
    steps = bsz * nq
    bf16_sublanes = 16
    assert all(w.shape[0] % (steps * bf16_sublanes) == 0 for w in cast_weights)
    slab = lambda w: pl.BlockSpec((w.shape[0] // steps, w.shape[1]), lambda b, t: (b * nq + t, 0))
    n_cast = len(cast_weights)
    return pl.pallas_call(
        functools.partial(_indexer_kernel, tq=tq, tk=tk, k_sel=k_sel, n_cast=n_cast),
        out_shape=[jax.ShapeDtypeStruct((bsz, nq, nk, tk, tq), F32)]
        + [jax.ShapeDtypeStruct(w.shape, BF16) for w in cast_weights],
        grid=(bsz, nq),
        in_specs=[
            pl.BlockSpec((tq, qi_cols), lambda b, t: (b * nq + t, 0)),
            pl.BlockSpec((None, seq, 2 * IDX_DIM), lambda b, t: (b, 0, 0)),
            pl.BlockSpec((None, seq, 2 * IDX_DIM), lambda b, t: (b, 0, 0)),
            pl.BlockSpec((None, IDX_HEADS, tq), lambda b, t: (b, 0, t)),
        ] + [slab(w) for w in cast_weights],
        out_specs=[pl.BlockSpec((None, None, nk, tk, tq), lambda b, t: (b, t, 0, 0, 0))]
        + [slab(w) for w in cast_weights],
        scratch_shapes=[pltpu.VMEM((nk, tk, tq), F32)],
        compiler_params=_params("parallel", "parallel"),
        name="indexer",
    )(q_idx, kpe, kpo, w_t, *cast_weights)


def _attention_kernel(q_ref, k_ref, vt_ref, bias_ref, o_ref, m_ref, acc_ref, sa_ref, sb_ref, ma_ref, mb_ref,
                      *, tq, tk, group):
    qt = pl.program_id(2)
    n_kt = _n_key_tiles(qt, tq, tk)
    rows = HEAD_DIM + ONES_ROWS
    m_ref[...] = jnp.full(m_ref.shape, NEG_BIG, F32)
    acc_ref[...] = jnp.zeros(acc_ref.shape, F32)

    def scores(kt, s_ref, mx_ref):
        k0 = pl.multiple_of(kt * tk, tk)
        for g in range(group):
            cols = slice(g * HEAD_DIM, (g + 1) * HEAD_DIM)
            s = _nt_dot(k_ref[pl.ds(k0, tk), cols], q_ref[:, cols]) + bias_ref[kt]
            s_ref[g] = s
            mx_ref[g] = jnp.max(s, axis=0, keepdims=True)

    def consume(kt, s_ref, mx_ref):
        for g in range(group):
            m_old = m_ref[g]
            m_new = jnp.maximum(m_old, mx_ref[g])
            m_ref[g] = m_new
            p = jnp.exp2(s_ref[g] - m_new).astype(BF16)
            pv = _dot(vt_ref[kt, g * rows:(g + 1) * rows, :], p)
            acc_ref[g] = jnp.exp2(m_old - m_new) * acc_ref[g] + pv

    scores(0, sa_ref, ma_ref)
    n_pairs = lax.div(n_kt - 1, 2)

    def pair_body(i, carry):
        kt = 2 * i
        scores(kt + 1, sb_ref, mb_ref)
        consume(kt, sa_ref, ma_ref)
        scores(kt + 2, sa_ref, ma_ref)
        consume(kt + 1, sb_ref, mb_ref)
        return carry
    lax.fori_loop(0, n_pairs, pair_body, 0)
    kt = 2 * n_pairs

    @pl.when(n_kt - kt == 1)
    def _():
        consume(kt, sa_ref, ma_ref)

    @pl.when(n_kt - kt == 2)
    def _():
        scores(kt + 1, sb_ref, mb_ref)
        consume(kt, sa_ref, ma_ref)
        consume(kt + 1, sb_ref, mb_ref)

    for g in range(group):
        acc = acc_ref[g]
        out = acc[:HEAD_DIM] / acc[HEAD_DIM:HEAD_DIM + 1]
        o_ref[:, g * HEAD_DIM:(g + 1) * HEAD_DIM] = out.T.astype(o_ref.dtype)


def _attention(q, k, vt, bias, tq, tk, group=8):
    bsz, nk, vrows, _ = vt.shape
    rows = HEAD_DIM + ONES_ROWS
    hd = vrows // rows * HEAD_DIM
    seq = nk * tk
    nq = seq // tq
    gw = group * HEAD_DIM
    return pl.pallas_call(
        functools.partial(_attention_kernel, tq=tq, tk=tk, group=group),
        out_shape=jax.ShapeDtypeStruct((bsz * seq, hd), BF16),
        grid=(bsz, hd // gw, nq),
        in_specs=[
            pl.BlockSpec((tq, gw), lambda b, g, t: (b * nq + t, g)),
            pl.BlockSpec((None, seq, gw), lambda b, g, t: (b, 0, g)),
            pl.BlockSpec((None, nk, group * rows, tk), lambda b, g, t: (b, 0, g, 0)),
            pl.BlockSpec((None, None, nk, tk, tq), lambda b, g, t: (b, t, 0, 0, 0)),
        ],
        out_specs=pl.BlockSpec((tq, gw), lambda b, g, t: (b * nq + t, g)),
        scratch_shapes=[pltpu.VMEM((group, 1, tq), F32), pltpu.VMEM((group, rows, tq), F32),
                        pltpu.VMEM((group, tk, tq), F32), pltpu.VMEM((group, tk, tq), F32),
                        pltpu.VMEM((group, 1, tq), F32), pltpu.VMEM((group, 1, tq), F32)],
        compiler_params=_params("parallel", "parallel", "arbitrary"),
        name="attention",
    )(q, k.reshape(bsz, seq, hd), vt, bias)


def _merge_kernel(xn_ref, ya_ref, yb_ref, wga_ref, wgb_ref, woa_ref, wob_ref, o_ref):
    xn = xn_ref[...]
    ga = jax.nn.sigmoid(_dot(xn, wga_ref[...]))
    gb = jax.nn.sigmoid(_dot(xn, wgb_ref[...]))
    o_ref[...] = (ga * _dot(ya_ref[...], woa_ref[...]) + gb * _dot(yb_ref[...], wob_ref[...])).astype(o_ref.dtype)


def _merge(xn, ya, yb, w_ga, w_gb, w_oa, w_ob, tm=1024, tn=512):
    m, d = xn.shape
    row = lambda k: pl.BlockSpec((tm, k), lambda i, j: (i, 0))
    colw = lambda k: pl.BlockSpec((k, tn), lambda i, j: (0, j))
    return pl.pallas_call(
        _merge_kernel,
        out_shape=jax.ShapeDtypeStruct((m, d), BF16),
        grid=(m // tm, d // tn),
        in_specs=[row(d), row(ya.shape[1]), row(yb.shape[1]),
                  colw(d), colw(d), colw(ya.shape[1]), colw(yb.shape[1])],
        out_specs=pl.BlockSpec((tm, tn), lambda i, j: (i, j)),
        compiler_params=_params("parallel", "arbitrary"),
        name="merge",
    )(xn, ya, yb, w_ga, w_gb, w_oa, w_ob)


def _out_kernel(x_ref, mix_ref, w_ref, g_ref, h_ref, hn_ref):
    h = x_ref[...] + _dot(mix_ref[...], w_ref[...])
    h_ref[...] = h
    hn_ref[...] = _rms(h, g_ref[...]).astype(hn_ref.dtype)


def _out_proj(x, mixed, w_out, g, tm=512):
    m, d = x.shape
    row = pl.BlockSpec((tm, d), lambda i: (i, 0))
    return pl.pallas_call(
        _out_kernel,
        out_shape=(jax.ShapeDtypeStruct((m, d), F32), jax.ShapeDtypeStruct((m, d), BF16)),
        grid=(m // tm,),
        in_specs=[row, row, pl.BlockSpec((d, d), lambda i: (0, 0)), pl.BlockSpec((1, d), lambda i: (0, 0))],
        out_specs=(row, row),
        compiler_params=_params("parallel"),
        name="out_proj",
    )(x, mixed, w_out, g.reshape(1, d))


def _ffn_kernel(hn_ref, h_ref, wg_ref, wu_ref, wd_ref, g_ref, o_ref, acc_ref, *, final_norm):
    j = pl.program_id(1)

    @pl.when(j == 0)
    def _():
        acc_ref[...] = jnp.zeros(acc_ref.shape, F32)

    hn = hn_ref[...]
    t = (jax.nn.silu(_dot(hn, wg_ref[...])) * _dot(hn, wu_ref[...])).astype(BF16)
    acc_ref[...] += _dot(t, wd_ref[...])

    @pl.when(j == pl.num_programs(1) - 1)
    def _():
        h2 = h_ref[...] + acc_ref[...]
        o_ref[...] = _rms(h2, g_ref[...]) if final_norm else h2


def _ffn(hn, h, w_gate, w_up, w_down, g, final_norm, tm=512, tf=512):
    m, d = hn.shape
    f = w_gate.shape[1]
    row = pl.BlockSpec((tm, d), lambda i, j: (i, 0))
    return pl.pallas_call(
        functools.partial(_ffn_kernel, final_norm=final_norm),
        out_shape=jax.ShapeDtypeStruct((m, d), F32),
        grid=(m // tm, f // tf),
        in_specs=[row, row,
                  pl.BlockSpec((d, tf), lambda i, j: (0, j)),
                  pl.BlockSpec((d, tf), lambda i, j: (0, j)),
                  pl.BlockSpec((tf, d), lambda i, j: (j, 0)),
                  pl.BlockSpec((1, d), lambda i, j: (0, 0))],
        out_specs=row,
        scratch_shapes=[pltpu.VMEM((tm, d), F32)],
        compiler_params=_params("parallel", "arbitrary"),
        name="ffn",
    )(hn, h, w_gate, w_up, w_down, g.reshape(1, d))


def kernel(x, norm1_g, w_in, a_ln_g, a_ln_b, a_w_s, a_b_s, kv_norm_g, w_uk, w_uv, w_oa, w_ob, w_out, norm2_g,
           w_ff_gate, w_ff_up, w_ff_down, final_g):
    bsz, seq, d = x.shape
    m = bsz * seq
    depth = norm1_g.shape[0]
    a_width = a_ln_g.shape[1]
    lat = kv_norm_g.shape[1]
    heads, head_dim = w_uk.shape[2], w_uk.shape[3]
    assert head_dim == HEAD_DIM
    hd = heads * head_dim
    qi = IDX_HEADS * IDX_DIM
    c_q = 2 * a_width
    c_kv = c_q + hd
    c_qi = c_kv + lat
    c_ki = c_qi + qi
    c_wi = c_ki + IDX_DIM
    c_g = c_wi + IDX_HEADS

    h = x.reshape(m, d)
    for l in range(depth):
        w = w_in[l].astype(BF16)
        w_ga = w[:, c_g:c_g + d]
        w_gb = w[:, c_g + d:]

        y_a, xn = _branch_a(h, norm1_g[l], w, a_ln_g[l], a_ln_b[l], a_w_s[l], a_b_s[l])
        q = _matmul(xn, w, c_q, hd, BF16, 1024, 512, "q_proj", scale=HEAD_DIM ** -0.5 * LOG2E)
        q_idx = _matmul(xn, w, c_qi, qi, BF16, 1024, 512, "qi_proj")
        kw = _matmul(xn, w, c_ki, 128, F32, 1024, 128, "kw_proj")
        k_tok, v_t = _kv_proj(xn, w, c_kv, kv_norm_g[l], w_uk[l].reshape(lat, hd).astype(BF16),
                              w_uv[l].reshape(lat, hd).T.astype(BF16), bsz, seq, ATT_TK)
        k_idx = kw[:, :IDX_DIM].astype(BF16).reshape(bsz, seq, IDX_DIM)
        zeros = jnp.zeros_like(k_idx)
        kpe = jnp.concatenate([k_idx, zeros], axis=-1)
        kpo = jnp.concatenate([zeros, k_idx], axis=-1)
        w_t = kw[:, IDX_DIM:IDX_DIM + IDX_HEADS].reshape(bsz, seq, IDX_HEADS).transpose(0, 2, 1)
        bias, w_oa_b, w_ob_b, w_out_b, w_fg_b, w_fu_b, w_fd_b = _indexer(
            q_idx, kpe, kpo, w_t, ATT_TQ, ATT_TK,
            [w_oa[l], w_ob[l], w_out[l], w_ff_gate[l], w_ff_up[l], w_ff_down[l]])
        y_b = _attention(q, k_tok, v_t, bias, ATT_TQ, ATT_TK)
        mixed = _merge(xn, y_a, y_b, w_ga, w_gb, w_oa_b, w_ob_b)
        h, hn = _out_proj(h, mixed, w_out_b, norm2_g[l])
        h = _ffn(hn, h, w_fg_b, w_fu_b, w_fd_b, final_g, final_norm=(l == depth - 1))
    return h.reshape(bsz, seq, d)
```

```python
import functools
import math

import jax
import jax.numpy as jnp
from jax import lax
from jax.experimental import pallas as pl
from jax.experimental.pallas import tpu as pltpu

EPS = 1e-6
CHUNK = 64
A_GROUPS = 8
A_BLOCK = 128
LANES = 128
HEAD_DIM = 128
IDX_DIM = 64
IDX_HEADS = 16
TOPK_MAX = 256
ONES_ROWS = 16
LOG2E = 1.4426950408889634

VMEM_LIMIT_BYTES = 56 * 1024 * 1024
INT_MIN = -2147483648
ATT_TQ = 256
ATT_TK = 512
NEG_BIG = -1e30

F32 = jnp.float32
BF16 = jnp.bfloat16


def _params(*sem, flags=None):
    return pltpu.CompilerParams(dimension_semantics=sem, vmem_limit_bytes=VMEM_LIMIT_BYTES, flags=flags)


def _nt_dot(a, b):
    return lax.dot_general(a, b, (((1,), (1,)), ((), ())), preferred_element_type=F32)


def _dot(a, b):
    return jnp.dot(a, b, preferred_element_type=F32)


def _rms(x, g):
    return x * lax.rsqrt(jnp.mean(x * x, axis=-1, keepdims=True) + EPS) * g


def _col_specs(rows, col0, n):
    bw = math.gcd(col0, n) if col0 else n
    assert bw % LANES == 0
    return [pl.BlockSpec((rows, bw), functools.partial(lambda i, jb: (0, jb), jb=col0 // bw + c)) for c in range(n // bw)]


def _qproj_kernel(xn_ref, *refs, n_q, n_qi, q_scale):
    wq, wqi, wkw = refs[:n_q], refs[n_q:n_q + n_qi], refs[n_q + n_qi]
    q_ref, qi_ref, kw_ref = refs[n_q + n_qi + 1:]
    xn = xn_ref[...]
    for c, w_ref in enumerate(wq):
        bw = w_ref.shape[1]
        q_ref[:, c * bw:(c + 1) * bw] = (_dot(xn, w_ref[...]) * q_scale).astype(q_ref.dtype)
    for c, w_ref in enumerate(wqi):
        bw = w_ref.shape[1]
        qi_ref[:, c * bw:(c + 1) * bw] = _dot(xn, w_ref[...]).astype(qi_ref.dtype)
    kw_ref[...] = _dot(xn, wkw[...])


def _q_projections(xn, w, c_q, n_q, c_qi, n_qi, c_kw, q_scale, tm=512):
    m, d = xn.shape
    q_specs, qi_specs, kw_specs = _col_specs(d, c_q, n_q), _col_specs(d, c_qi, n_qi), _col_specs(d, c_kw, LANES)
    row = lambda n: pl.BlockSpec((tm, n), lambda i: (i, 0))
    return pl.pallas_call(
        functools.partial(_qproj_kernel, n_q=len(q_specs), n_qi=len(qi_specs), q_scale=q_scale),
        out_shape=(jax.ShapeDtypeStruct((m, n_q), BF16), jax.ShapeDtypeStruct((m, n_qi), BF16),
                   jax.ShapeDtypeStruct((m, LANES), F32)),
        grid=(m // tm,),
        in_specs=[row(d)] + q_specs + qi_specs + kw_specs,
        out_specs=(row(n_q), row(n_qi), row(LANES)),
        compiler_params=_params("parallel"),
        name="q_proj",
    )(xn, *([w] * (len(q_specs) + len(qi_specs) + 1)))


def _branch_a_kernel(x_ref, g1_ref, w_ref, lng_ref, lnb_ref, ws_ref, bst_ref, o_ref, xn_ref, z_ref, *, tm, tn, width):
    j = pl.program_id(1)

    @pl.when(j == 0)
    def _():
        xn_ref[...] = _rms(x_ref[...], g1_ref[...]).astype(xn_ref.dtype)

    z_ref[j] = jax.nn.gelu(_dot(xn_ref[...], w_ref[...]))

    @pl.when(j == pl.num_programs(1) - 1)
    def _():
        per_half = width // tn
        gw = width // A_GROUPS
        row = lax.broadcasted_iota(jnp.int32, (A_BLOCK, A_BLOCK), 0)
        col = lax.broadcasted_iota(jnp.int32, (A_BLOCK, A_BLOCK), 1)
        causal = (col // CHUNK) <= (row // CHUNK)
        for r in range(tm // A_BLOCK):
            rows = slice(r * A_BLOCK, (r + 1) * A_BLOCK)
            v = jnp.concatenate([z_ref[per_half + c, rows, :] for c in range(per_half)], axis=-1)
            vc = v - jnp.mean(v, axis=-1, keepdims=True)
            vn = vc * lax.rsqrt(jnp.mean(vc * vc, axis=-1, keepdims=True) + EPS)
            vn = (vn * lng_ref[...] + lnb_ref[...]).astype(BF16)
            for g in range(A_GROUPS):
                wm = jnp.where(causal, ws_ref[g], 0.0).astype(BF16)
                sv = _dot(wm, vn[:, g * gw:(g + 1) * gw]) + bst_ref[:, g:g + 1]
                c, off = divmod(g * gw, tn)
                u = z_ref[c, rows, off:off + gw]
                o_ref[rows, g * gw:(g + 1) * gw] = (u * sv).astype(o_ref.dtype)


def _branch_a(x, g1, w_za, ln_g, ln_b, w_s, b_s, tm=512, tn=1024):
    m, d = x.shape
    width = ln_g.shape[0]
    assert (width // A_GROUPS) <= tn and tn % (width // A_GROUPS) == 0
    kern = functools.partial(_branch_a_kernel, tm=tm, tn=tn, width=width)
    return pl.pallas_call(
        kern,
        out_shape=(jax.ShapeDtypeStruct((m, width), BF16), jax.ShapeDtypeStruct((m, d), BF16)),
        grid=(m // tm, 2 * width // tn),
        in_specs=[
            pl.BlockSpec((tm, d), lambda i, j: (i, 0)),
            pl.BlockSpec((1, d), lambda i, j: (0, 0)),
            pl.BlockSpec((d, tn), lambda i, j: (0, j)),
            pl.BlockSpec((1, width), lambda i, j: (0, 0)),
            pl.BlockSpec((1, width), lambda i, j: (0, 0)),
            pl.BlockSpec((A_GROUPS, A_BLOCK, A_BLOCK), lambda i, j: (0, 0, 0)),
            pl.BlockSpec((A_BLOCK, A_GROUPS), lambda i, j: (0, 0)),
        ],
        out_specs=(pl.BlockSpec((tm, width), lambda i, j: (i, 0)), pl.BlockSpec((tm, d), lambda i, j: (i, 0))),
        scratch_shapes=[pltpu.VMEM((2 * width // tn, tm, tn), F32)],
        compiler_params=_params("parallel", "arbitrary"),
        name="branch_a",
    )(x, g1.reshape(1, d), w_za, ln_g.reshape(1, width), ln_b.reshape(1, width), w_s, b_s.T)


def _kv_kernel(xn_ref, wc_ref, g_ref, wuk_ref, wuvt_ref, k_ref, vt_ref):
    c = _dot(xn_ref[...], wc_ref[...])
    cn = _rms(c, g_ref[...]).astype(BF16)
    k_ref[...] = _dot(cn, wuk_ref[...]).astype(k_ref.dtype)
    vt = _nt_dot(wuvt_ref[...], cn).astype(vt_ref.dtype)
    rows = HEAD_DIM + ONES_ROWS
    for h in range(vt.shape[0] // HEAD_DIM):
        vt_ref[h * rows:h * rows + HEAD_DIM, :] = vt[h * HEAD_DIM:(h + 1) * HEAD_DIM]
        vt_ref[h * rows + HEAD_DIM:(h + 1) * rows, :] = jnp.ones((ONES_ROWS, vt.shape[1]), vt_ref.dtype)


def _kv_proj(xn, w_c, col0, g, w_uk, w_uv_t, bsz, seq, tk):
    m, d = xn.shape
    lat, hd = w_uk.shape
    assert col0 % lat == 0
    j0 = col0 // lat
    nt = seq // tk
    vrows = hd // HEAD_DIM * (HEAD_DIM + ONES_ROWS)
    return pl.pallas_call(
        _kv_kernel,
        out_shape=(jax.ShapeDtypeStruct((m, hd), BF16), jax.ShapeDtypeStruct((bsz, nt, vrows, tk), BF16)),
        grid=(m // tk,),
        in_specs=[
            pl.BlockSpec((tk, d), lambda i: (i, 0)),
            pl.BlockSpec((d, lat), lambda i: (0, j0)),
            pl.BlockSpec((1, lat), lambda i: (0, 0)),
            pl.BlockSpec((lat, hd), lambda i: (0, 0)),
            pl.BlockSpec((hd, lat), lambda i: (0, 0)),
        ],
        out_specs=(pl.BlockSpec((tk, hd), lambda i: (i, 0)),
                   pl.BlockSpec((None, None, vrows, tk), lambda i: (i // nt, i % nt, 0, 0))),
        compiler_params=_params("parallel"),
        name="kv_proj",
    )(xn, w_c, g.reshape(1, lat), w_uk, w_uv_t)


def _n_key_tiles(qt, tq, tk):
    return lax.div(qt * tq + tq + tk - 1, tk)


def _indexer_kernel(qi_ref, kpe_ref, kpo_ref, wt_ref, *refs, tq, tk, k_sel, n_cast):
    cast_in, o_ref, cast_out, sc_ref = refs[:n_cast], refs[n_cast], refs[n_cast + 1:2 * n_cast + 1], refs[-1]
    for w_ref, wb_ref in zip(cast_in, cast_out):
        wb_ref[...] = w_ref[...].astype(wb_ref.dtype)

    qt = pl.program_id(1)
    n_kt = _n_key_tiles(qt, tq, tk)
    nk = sc_ref.shape[0]
    neg_inf = -jnp.inf

    def count(pred):
        rows = 16
        def body(kt, cnt):
            hit = jnp.where(pred(sc_ref[kt], kt), 1.0, 0.0)
            return cnt + jnp.sum(hit.reshape(tk // rows, rows, tq), axis=0)
        cnt = lax.fori_loop(0, n_kt, body, jnp.zeros((rows, tq), F32))
        return jnp.sum(cnt, axis=0, keepdims=True)

    def key_pos(kt):
        return kt * tk + lax.broadcasted_iota(jnp.int32, (tk, tq), 0)

    wt = wt_ref[...] * (IDX_DIM ** -0.5 * IDX_HEADS ** -0.5)
    q_chunk = (qt * tq + lax.broadcasted_iota(jnp.int32, (1, tq), 1)) // CHUNK

    def score_tile(kt):
        k0 = pl.multiple_of(kt * tk, tk)
        ke = kpe_ref[pl.ds(k0, tk), :]
        ko = kpo_ref[pl.ds(k0, tk), :]
        acc = jnp.zeros((tk, tq), F32)
        for j in range(IDX_HEADS // 2):
            qp = qi_ref[:, j * 128:(j + 1) * 128]
            acc = acc + wt[2 * j:2 * j + 1, :] * jnp.maximum(_nt_dot(ke, qp), 0.0)
            acc = acc + wt[2 * j + 1:2 * j + 2, :] * jnp.maximum(_nt_dot(ko, qp), 0.0)
        k_chunk = (k0 + lax.broadcasted_iota(jnp.int32, (tk, 1), 0)) // CHUNK
        sc_ref[kt] = jnp.where(k_chunk <= q_chunk, acc, neg_inf)

    def score_pair(i, carry):
        score_tile(2 * i)
        score_tile(2 * i + 1)
        return carry
    lax.fori_loop(0, lax.div(n_kt, 2), score_pair, 0)

    @pl.when(lax.rem(n_kt, 2) == 1)
    def _():
        score_tile(n_kt - 1)

    def key_to_float(key):
        return lax.bitcast_convert_type(key ^ ((key >> 31) & 0x7FFFFFFF), F32)

    def bit_body(i, tau):
        cand = tau ^ jnp.left_shift(jnp.int32(1), 31 - i)
        cand_f = key_to_float(cand)
        n = count(lambda s, kt: s >= cand_f)
        return jnp.where(n >= k_sel, cand, tau)
    tau = lax.fori_loop(0, 32, bit_body, jnp.full((1, tq), INT_MIN, jnp.int32))
    key_neg_inf = INT_MIN + 0x7FFFFF
    tau_f = jnp.where(tau <= key_neg_inf, neg_inf, key_to_float(tau))
    n_ge = count(lambda s, kt: s >= tau_f)
    has_ties = jnp.max(n_ge) > k_sel

    def write_mask(sel):
        def body(kt, carry):
            s = sc_ref[kt]
            keep = sel(s, kt) & (s > neg_inf)
            o_ref[kt] = jnp.where(keep, 0.0, NEG_BIG).astype(o_ref.dtype)
            return carry
        lax.fori_loop(0, n_kt, body, 0)

    @pl.when(jnp.logical_not(has_ties))
    def _():
        write_mask(lambda s, kt: s >= tau_f)

    @pl.when(has_ties)
    def _():
        need = k_sel - count(lambda s, kt: s > tau_f)

        def idx_body(i, cut):
            cand = cut + jnp.left_shift(jnp.int32(1), 12 - i)
            n = count(lambda s, kt: (s == tau_f) & (key_pos(kt) < cand))
            return jnp.where(n <= need, cand, cut)
        cut = lax.fori_loop(0, 13, idx_body, jnp.zeros((1, tq), jnp.int32))
        write_mask(lambda s, kt: (s > tau_f) | ((s == tau_f) & (key_pos(kt) < cut)))

    def fill_body(kt, carry):
        o_ref[kt] = jnp.full((tk, tq), NEG_BIG, o_ref.dtype)
        return carry
    lax.fori_loop(n_kt, nk, fill_body, 0)


def _indexer(q_idx, kpe, kpo, w_t, tq, tk, cast_weights):
    bsz, seq, _ = kpe.shape
    nq, nk = seq // tq, seq // tk
    k_sel = min(TOPK_MAX, seq // 4)
    assert tq >= k_sel and tq % CHUNK == 0 and tk % tq == 0
    qi_cols = q_idx.shape[1]
    steps = bsz * nq
    bf16_sublanes = 16
    assert all(w.shape[0] % (steps * bf16_sublanes) == 0 for w in cast_weights)
    slab = lambda w: pl.BlockSpec((w.shape[0] // steps, w.shape[1]), lambda b, t: (b * nq + t, 0))
    n_cast = len(cast_weights)
    return pl.pallas_call(
        functools.partial(_indexer_kernel, tq=tq, tk=tk, k_sel=k_sel, n_cast=n_cast),
        out_shape=[jax.ShapeDtypeStruct((bsz, nq, nk, tk, tq), F32)]
        + [jax.ShapeDtypeStruct(w.shape, BF16) for w in cast_weights],
        grid=(bsz, nq),
        in_specs=[
            pl.BlockSpec((tq, qi_cols), lambda b, t: (b * nq + t, 0)),
            pl.BlockSpec((None, seq, 2 * IDX_DIM), lambda b, t: (b, 0, 0)),
            pl.BlockSpec((None, seq, 2 * IDX_DIM), lambda b, t: (b, 0, 0)),
            pl.BlockSpec((None, IDX_HEADS, tq), lambda b, t: (b, 0, t)),
        ] + [slab(w) for w in cast_weights],
        out_specs=[pl.BlockSpec((None, None, nk, tk, tq), lambda b, t: (b, t, 0, 0, 0))]
        + [slab(w) for w in cast_weights],
        scratch_shapes=[pltpu.VMEM((nk, tk, tq), F32)],
        compiler_params=_params("parallel", "parallel"),
        name="indexer",
    )(q_idx, kpe, kpo, w_t, *cast_weights)


def _attention_kernel(q_ref, k_ref, vt_ref, bias_ref, o_ref, m_ref, acc_ref, sa_ref, sb_ref, ma_ref, mb_ref,
                      *, tq, tk, group):
    qt = pl.program_id(2)
    n_kt = _n_key_tiles(qt, tq, tk)
    rows = HEAD_DIM + ONES_ROWS
    m_ref[...] = jnp.full(m_ref.shape, NEG_BIG, F32)
    acc_ref[...] = jnp.zeros(acc_ref.shape, F32)

    def scores(kt, s_ref, mx_ref):
        k0 = pl.multiple_of(kt * tk, tk)
        for g in range(group):
            cols = slice(g * HEAD_DIM, (g + 1) * HEAD_DIM)
            s = _nt_dot(k_ref[pl.ds(k0, tk), cols], q_ref[:, cols]) + bias_ref[kt]
            s_ref[g] = s
            mx_ref[g] = jnp.max(s, axis=0, keepdims=True)

    def consume(kt, s_ref, mx_ref):
        for g in range(group):
            m_old = m_ref[g]
            m_new = jnp.maximum(m_old, mx_ref[g])
            m_ref[g] = m_new
            p = jnp.exp2(s_ref[g] - m_new).astype(BF16)
            pv = _dot(vt_ref[kt, g * rows:(g + 1) * rows, :], p)
            acc_ref[g] = jnp.exp2(m_old - m_new) * acc_ref[g] + pv

    scores(0, sa_ref, ma_ref)
    n_pairs = lax.div(n_kt - 1, 2)

    def pair_body(i, carry):
        kt = 2 * i
        scores(kt + 1, sb_ref, mb_ref)
        consume(kt, sa_ref, ma_ref)
        scores(kt + 2, sa_ref, ma_ref)
        consume(kt + 1, sb_ref, mb_ref)
        return carry
    lax.fori_loop(0, n_pairs, pair_body, 0)
    kt = 2 * n_pairs

    @pl.when(n_kt - kt == 1)
    def _():
        consume(kt, sa_ref, ma_ref)

    @pl.when(n_kt - kt == 2)
    def _():
        scores(kt + 1, sb_ref, mb_ref)
        consume(kt, sa_ref, ma_ref)
        consume(kt + 1, sb_ref, mb_ref)

    for g in range(group):
        acc = acc_ref[g]
        out = acc[:HEAD_DIM] / acc[HEAD_DIM:HEAD_DIM + 1]
        o_ref[:, g * HEAD_DIM:(g + 1) * HEAD_DIM] = out.T.astype(o_ref.dtype)


def _attention(q, k, vt, bias, tq, tk, group=8):
    bsz, nk, vrows, _ = vt.shape
    rows = HEAD_DIM + ONES_ROWS
    hd = vrows // rows * HEAD_DIM
    seq = nk * tk
    nq = seq // tq
    gw = group * HEAD_DIM
    return pl.pallas_call(
        functools.partial(_attention_kernel, tq=tq, tk=tk, group=group),
        out_shape=jax.ShapeDtypeStruct((bsz * seq, hd), BF16),
        grid=(bsz, hd // gw, nq),
        in_specs=[
            pl.BlockSpec((tq, gw), lambda b, g, t: (b * nq + t, g)),
            pl.BlockSpec((None, seq, gw), lambda b, g, t: (b, 0, g)),
            pl.BlockSpec((None, nk, group * rows, tk), lambda b, g, t: (b, 0, g, 0)),
            pl.BlockSpec((None, None, nk, tk, tq), lambda b, g, t: (b, t, 0, 0, 0)),
        ],
        out_specs=pl.BlockSpec((tq, gw), lambda b, g, t: (b * nq + t, g)),
        scratch_shapes=[pltpu.VMEM((group, 1, tq), F32), pltpu.VMEM((group, rows, tq), F32),
                        pltpu.VMEM((group, tk, tq), F32), pltpu.VMEM((group, tk, tq), F32),
                        pltpu.VMEM((group, 1, tq), F32), pltpu.VMEM((group, 1, tq), F32)],
        compiler_params=_params("parallel", "parallel", "arbitrary"),
        name="attention",
    )(q, k.reshape(bsz, seq, hd), vt, bias)


def _merge_kernel(xn_ref, ya_ref, yb_ref, wga_ref, wgb_ref, woa_ref, wob_ref, o_ref):
    xn = xn_ref[...]
    ga = jax.nn.sigmoid(_dot(xn, wga_ref[...]))
    gb = jax.nn.sigmoid(_dot(xn, wgb_ref[...]))
    o_ref[...] = (ga * _dot(ya_ref[...], woa_ref[...]) + gb * _dot(yb_ref[...], wob_ref[...])).astype(o_ref.dtype)


def _merge(xn, ya, yb, w_ga, w_gb, w_oa, w_ob, tm=1024, tn=512):
    m, d = xn.shape
    row = lambda k: pl.BlockSpec((tm, k), lambda i, j: (i, 0))
    colw = lambda k: pl.BlockSpec((k, tn), lambda i, j: (0, j))
    return pl.pallas_call(
        _merge_kernel,
        out_shape=jax.ShapeDtypeStruct((m, d), BF16),
        grid=(m // tm, d // tn),
        in_specs=[row(d), row(ya.shape[1]), row(yb.shape[1]),
                  colw(d), colw(d), colw(ya.shape[1]), colw(yb.shape[1])],
        out_specs=pl.BlockSpec((tm, tn), lambda i, j: (i, j)),
        compiler_params=_params("parallel", "arbitrary"),
        name="merge",
    )(xn, ya, yb, w_ga, w_gb, w_oa, w_ob)


def _out_kernel(x_ref, mix_ref, w_ref, g_ref, h_ref, hn_ref):
    h = x_ref[...] + _dot(mix_ref[...], w_ref[...])
    h_ref[...] = h
    hn_ref[...] = _rms(h, g_ref[...]).astype(hn_ref.dtype)


def _out_proj(x, mixed, w_out, g, tm=512):
    m, d = x.shape
    row = pl.BlockSpec((tm, d), lambda i: (i, 0))
    return pl.pallas_call(
        _out_kernel,
        out_shape=(jax.ShapeDtypeStruct((m, d), F32), jax.ShapeDtypeStruct((m, d), BF16)),
        grid=(m // tm,),
        in_specs=[row, row, pl.BlockSpec((d, d), lambda i: (0, 0)), pl.BlockSpec((1, d), lambda i: (0, 0))],
        out_specs=(row, row),
        compiler_params=_params("parallel"),
        name="out_proj",
    )(x, mixed, w_out, g.reshape(1, d))


def _ffn_kernel(hn_ref, h_ref, wg_ref, wu_ref, wd_ref, g_ref, o_ref, acc_ref, *, final_norm):
    j = pl.program_id(1)

    @pl.when(j == 0)
    def _():
        acc_ref[...] = jnp.zeros(acc_ref.shape, F32)

    hn = hn_ref[...]
    t = (jax.nn.silu(_dot(hn, wg_ref[...])) * _dot(hn, wu_ref[...])).astype(BF16)
    acc_ref[...] += _dot(t, wd_ref[...])

    @pl.when(j == pl.num_programs(1) - 1)
    def _():
        h2 = h_ref[...] + acc_ref[...]
        o_ref[...] = _rms(h2, g_ref[...]) if final_norm else h2


def _ffn(hn, h, w_gate, w_up, w_down, g, final_norm, tm=512, tf=512):
    m, d = hn.shape
    f = w_gate.shape[1]
    row = pl.BlockSpec((tm, d), lambda i, j: (i, 0))
    return pl.pallas_call(
        functools.partial(_ffn_kernel, final_norm=final_norm),
        out_shape=jax.ShapeDtypeStruct((m, d), F32),
        grid=(m // tm, f // tf),
        in_specs=[row, row,
                  pl.BlockSpec((d, tf), lambda i, j: (0, j)),
                  pl.BlockSpec((d, tf), lambda i, j: (0, j)),
                  pl.BlockSpec((tf, d), lambda i, j: (j, 0)),
                  pl.BlockSpec((1, d), lambda i, j: (0, 0))],
        out_specs=row,
        scratch_shapes=[pltpu.VMEM((tm, d), F32)],
        compiler_params=_params("parallel", "arbitrary"),
        name="ffn",
    )(hn, h, w_gate, w_up, w_down, g.reshape(1, d))


def kernel(x, norm1_g, w_in, a_ln_g, a_ln_b, a_w_s, a_b_s, kv_norm_g, w_uk, w_uv, w_oa, w_ob, w_out, norm2_g,
           w_ff_gate, w_ff_up, w_ff_down, final_g):
    bsz, seq, d = x.shape
    m = bsz * seq
    depth = norm1_g.shape[0]
    a_width = a_ln_g.shape[1]
    lat = kv_norm_g.shape[1]
    heads, head_dim = w_uk.shape[2], w_uk.shape[3]
    assert head_dim == HEAD_DIM
    hd = heads * head_dim
    qi = IDX_HEADS * IDX_DIM
    c_q = 2 * a_width
    c_kv = c_q + hd
    c_qi = c_kv + lat
    c_ki = c_qi + qi
    c_wi = c_ki + IDX_DIM
    c_g = c_wi + IDX_HEADS

    h = x.reshape(m, d)
    for l in range(depth):
        front = c_ki + LANES
        w = w_in[l][:, :front].astype(BF16)
        w_ga = w_in[l][:, c_g:c_g + d].astype(BF16)
        w_gb = w_in[l][:, c_g + d:].astype(BF16)

        y_a, xn = _branch_a(h, norm1_g[l], w, a_ln_g[l], a_ln_b[l], a_w_s[l], a_b_s[l])
        q, q_idx, kw = _q_projections(xn, w, c_q, hd, c_qi, qi, c_ki, HEAD_DIM ** -0.5 * LOG2E)
        k_tok, v_t = _kv_proj(xn, w, c_kv, kv_norm_g[l], w_uk[l].reshape(lat, hd).astype(BF16),
                              w_uv[l].reshape(lat, hd).T.astype(BF16), bsz, seq, ATT_TK)
        k_idx = kw[:, :IDX_DIM].astype(BF16).reshape(bsz, seq, IDX_DIM)
        zeros = jnp.zeros_like(k_idx)
        kpe = jnp.concatenate([k_idx, zeros], axis=-1)
        kpo = jnp.concatenate([zeros, k_idx], axis=-1)
        w_t = kw[:, IDX_DIM:IDX_DIM + IDX_HEADS].reshape(bsz, seq, IDX_HEADS).transpose(0, 2, 1)
        bias, w_oa_b, w_ob_b, w_out_b, w_fg_b, w_fu_b, w_fd_b = _indexer(
            q_idx, kpe, kpo, w_t, ATT_TQ, ATT_TK,
            [w_oa[l], w_ob[l], w_out[l], w_ff_gate[l], w_ff_up[l], w_ff_down[l]])
        y_b = _attention(q, k_tok, v_t, bias, ATT_TQ, ATT_TK)
        mixed = _merge(xn, y_a, y_b, w_ga, w_gb, w_oa_b, w_ob_b)
        h, hn = _out_proj(h, mixed, w_out_b, norm2_g[l])
        h = _ffn(hn, h, w_fg_b, w_fu_b, w_fd_b, final_g, final_norm=(l == depth - 1))
    return h.reshape(bsz, seq, d)
```

```python
import functools
import math

import jax
import jax.numpy as jnp
from jax import lax
from jax.experimental import pallas as pl
from jax.experimental.pallas import tpu as pltpu

EPS = 1e-6
CHUNK = 64
A_GROUPS = 8
A_BLOCK = 128
LANES = 128
HEAD_DIM = 128
IDX_DIM = 64
IDX_HEADS = 16
TOPK_MAX = 256
ONES_ROWS = 16
LOG2E = 1.4426950408889634

VMEM_LIMIT_BYTES = 56 * 1024 * 1024
INT_MIN = -2147483648
ATT_TQ = 256
ATT_TK = 512
NEG_BIG = -1e30

F32 = jnp.float32
BF16 = jnp.bfloat16


def _params(*sem, flags=None):
    return pltpu.CompilerParams(dimension_semantics=sem, vmem_limit_bytes=VMEM_LIMIT_BYTES, flags=flags)


def _nt_dot(a, b):
    return lax.dot_general(a, b, (((1,), (1,)), ((), ())), preferred_element_type=F32)


def _dot(a, b):
    return jnp.dot(a, b, preferred_element_type=F32)


def _rms(x, g):
    return x * lax.rsqrt(jnp.mean(x * x, axis=-1, keepdims=True) + EPS) * g


def _col_specs(rows, col0, n):
    bw = math.gcd(col0, n) if col0 else n
    assert bw % LANES == 0
    return [pl.BlockSpec((rows, bw), functools.partial(lambda i, jb: (0, jb), jb=col0 // bw + c)) for c in range(n // bw)]


def _qproj_kernel(xn_ref, *refs, n_q, n_qi, q_scale):
    wq, wqi, wkw = refs[:n_q], refs[n_q:n_q + n_qi], refs[n_q + n_qi]
    q_ref, qi_ref, kw_ref = refs[n_q + n_qi + 1:]
    xn = xn_ref[...]
    for c, w_ref in enumerate(wq):
        bw = w_ref.shape[1]
        q_ref[:, c * bw:(c + 1) * bw] = (_dot(xn, w_ref[...]) * q_scale).astype(q_ref.dtype)
    for c, w_ref in enumerate(wqi):
        bw = w_ref.shape[1]
        qi_ref[:, c * bw:(c + 1) * bw] = _dot(xn, w_ref[...]).astype(qi_ref.dtype)
    kw_ref[...] = _dot(xn, wkw[...])


def _q_projections(xn, w, c_q, n_q, c_qi, n_qi, c_kw, q_scale, tm=512):
    m, d = xn.shape
    q_specs, qi_specs, kw_specs = _col_specs(d, c_q, n_q), _col_specs(d, c_qi, n_qi), _col_specs(d, c_kw, LANES)
    row = lambda n: pl.BlockSpec((tm, n), lambda i: (i, 0))
    return pl.pallas_call(
        functools.partial(_qproj_kernel, n_q=len(q_specs), n_qi=len(qi_specs), q_scale=q_scale),
        out_shape=(jax.ShapeDtypeStruct((m, n_q), BF16), jax.ShapeDtypeStruct((m, n_qi), BF16),
                   jax.ShapeDtypeStruct((m, LANES), F32)),
        grid=(m // tm,),
        in_specs=[row(d)] + q_specs + qi_specs + kw_specs,
        out_specs=(row(n_q), row(n_qi), row(LANES)),
        compiler_params=_params("parallel"),
        name="q_proj",
    )(xn, *([w] * (len(q_specs) + len(qi_specs) + 1)))


def _branch_a_kernel(x_ref, g1_ref, w_ref, lng_ref, lnb_ref, ws_ref, bst_ref, o_ref, xn_ref, z_ref, *, tm, tn, width):
    j = pl.program_id(1)

    @pl.when(j == 0)
    def _():
        xn_ref[...] = _rms(x_ref[...], g1_ref[...]).astype(xn_ref.dtype)

    z_ref[j] = jax.nn.gelu(_dot(xn_ref[...], w_ref[...]))

    @pl.when(j == pl.num_programs(1) - 1)
    def _():
        per_half = width // tn
        gw = width // A_GROUPS
        row = lax.broadcasted_iota(jnp.int32, (A_BLOCK, A_BLOCK), 0)
        col = lax.broadcasted_iota(jnp.int32, (A_BLOCK, A_BLOCK), 1)
        causal = (col // CHUNK) <= (row // CHUNK)
        for r in range(tm // A_BLOCK):
            rows = slice(r * A_BLOCK, (r + 1) * A_BLOCK)
            v = jnp.concatenate([z_ref[per_half + c, rows, :] for c in range(per_half)], axis=-1)
            vc = v - jnp.mean(v, axis=-1, keepdims=True)
            vn = vc * lax.rsqrt(jnp.mean(vc * vc, axis=-1, keepdims=True) + EPS)
            vn = (vn * lng_ref[...] + lnb_ref[...]).astype(BF16)
            for g in range(A_GROUPS):
                wm = jnp.where(causal, ws_ref[g], 0.0).astype(BF16)
                sv = _dot(wm, vn[:, g * gw:(g + 1) * gw]) + bst_ref[:, g:g + 1]
                c, off = divmod(g * gw, tn)
                u = z_ref[c, rows, off:off + gw]
                o_ref[rows, g * gw:(g + 1) * gw] = (u * sv).astype(o_ref.dtype)


def _branch_a(x, g1, w_za, ln_g, ln_b, w_s, b_s, tm=512, tn=1024):
    m, d = x.shape
    width = ln_g.shape[0]
    assert (width // A_GROUPS) <= tn and tn % (width // A_GROUPS) == 0
    kern = functools.partial(_branch_a_kernel, tm=tm, tn=tn, width=width)
    return pl.pallas_call(
        kern,
        out_shape=(jax.ShapeDtypeStruct((m, width), BF16), jax.ShapeDtypeStruct((m, d), BF16)),
        grid=(m // tm, 2 * width // tn),
        in_specs=[
            pl.BlockSpec((tm, d), lambda i, j: (i, 0)),
            pl.BlockSpec((1, d), lambda i, j: (0, 0)),
            pl.BlockSpec((d, tn), lambda i, j: (0, j)),
            pl.BlockSpec((1, width), lambda i, j: (0, 0)),
            pl.BlockSpec((1, width), lambda i, j: (0, 0)),
            pl.BlockSpec((A_GROUPS, A_BLOCK, A_BLOCK), lambda i, j: (0, 0, 0)),
            pl.BlockSpec((A_BLOCK, A_GROUPS), lambda i, j: (0, 0)),
        ],
        out_specs=(pl.BlockSpec((tm, width), lambda i, j: (i, 0)), pl.BlockSpec((tm, d), lambda i, j: (i, 0))),
        scratch_shapes=[pltpu.VMEM((2 * width // tn, tm, tn), F32)],
        compiler_params=_params("parallel", "arbitrary"),
        name="branch_a",
    )(x, g1.reshape(1, d), w_za, ln_g.reshape(1, width), ln_b.reshape(1, width), w_s, b_s.T)


def _kv_kernel(xn_ref, wc_ref, g_ref, wuk_ref, wuvt_ref, k_ref, vt_ref):
    c = _dot(xn_ref[...], wc_ref[...])
    cn = _rms(c, g_ref[...]).astype(BF16)
    k_ref[...] = _dot(cn, wuk_ref[...]).astype(k_ref.dtype)
    vt = _nt_dot(wuvt_ref[...], cn).astype(vt_ref.dtype)
    rows = HEAD_DIM + ONES_ROWS
    for h in range(vt.shape[0] // HEAD_DIM):
        vt_ref[h * rows:h * rows + HEAD_DIM, :] = vt[h * HEAD_DIM:(h + 1) * HEAD_DIM]
        vt_ref[h * rows + HEAD_DIM:(h + 1) * rows, :] = jnp.ones((ONES_ROWS, vt.shape[1]), vt_ref.dtype)


def _kv_proj(xn, w_c, col0, g, w_uk, w_uv_t, bsz, seq, tk):
    m, d = xn.shape
    lat, hd = w_uk.shape
    assert col0 % lat == 0
    j0 = col0 // lat
    nt = seq // tk
    vrows = hd // HEAD_DIM * (HEAD_DIM + ONES_ROWS)
    return pl.pallas_call(
        _kv_kernel,
        out_shape=(jax.ShapeDtypeStruct((m, hd), BF16), jax.ShapeDtypeStruct((bsz, nt, vrows, tk), BF16)),
        grid=(m // tk,),
        in_specs=[
            pl.BlockSpec((tk, d), lambda i: (i, 0)),
            pl.BlockSpec((d, lat), lambda i: (0, j0)),
            pl.BlockSpec((1, lat), lambda i: (0, 0)),
            pl.BlockSpec((lat, hd), lambda i: (0, 0)),
            pl.BlockSpec((hd, lat), lambda i: (0, 0)),
        ],
        out_specs=(pl.BlockSpec((tk, hd), lambda i: (i, 0)),
                   pl.BlockSpec((None, None, vrows, tk), lambda i: (i // nt, i % nt, 0, 0))),
        compiler_params=_params("parallel"),
        name="kv_proj",
    )(xn, w_c, g.reshape(1, lat), w_uk, w_uv_t)


def _n_key_tiles(qt, tq, tk):
    return lax.div(qt * tq + tq + tk - 1, tk)


def _indexer_kernel(qi_ref, kpe_ref, kpo_ref, wt_ref, *refs, tq, tk, k_sel, n_cast):
    cast_in, o_ref, cast_out, sc_ref = refs[:n_cast], refs[n_cast], refs[n_cast + 1:2 * n_cast + 1], refs[-1]
    for w_ref, wb_ref in zip(cast_in, cast_out):
        wb_ref[...] = w_ref[...].astype(wb_ref.dtype)

    qt = pl.program_id(1)
    n_kt = _n_key_tiles(qt, tq, tk)
    nk = sc_ref.shape[0]
    neg_inf = -jnp.inf

    def count(pred):
        rows = 16
        def body(kt, cnt):
            hit = jnp.where(pred(sc_ref[kt], kt), 1.0, 0.0)
            return cnt + jnp.sum(hit.reshape(tk // rows, rows, tq), axis=0)
        cnt = lax.fori_loop(0, n_kt, body, jnp.zeros((rows, tq), F32))
        return jnp.sum(cnt, axis=0, keepdims=True)

    def key_pos(kt):
        return kt * tk + lax.broadcasted_iota(jnp.int32, (tk, tq), 0)

    wt = wt_ref[...] * (IDX_DIM ** -0.5 * IDX_HEADS ** -0.5)
    q_chunk = (qt * tq + lax.broadcasted_iota(jnp.int32, (1, tq), 1)) // CHUNK

    def score_tile(kt):
        k0 = pl.multiple_of(kt * tk, tk)
        ke = kpe_ref[pl.ds(k0, tk), :]
        ko = kpo_ref[pl.ds(k0, tk), :]
        acc = jnp.zeros((tk, tq), F32)
        for j in range(IDX_HEADS // 2):
            qp = qi_ref[:, j * 128:(j + 1) * 128]
            acc = acc + wt[2 * j:2 * j + 1, :] * jnp.maximum(_nt_dot(ke, qp), 0.0)
            acc = acc + wt[2 * j + 1:2 * j + 2, :] * jnp.maximum(_nt_dot(ko, qp), 0.0)
        k_chunk = (k0 + lax.broadcasted_iota(jnp.int32, (tk, 1), 0)) // CHUNK
        sc_ref[kt] = jnp.where(k_chunk <= q_chunk, acc, neg_inf)

    def score_pair(i, carry):
        score_tile(2 * i)
        score_tile(2 * i + 1)
        return carry
    lax.fori_loop(0, lax.div(n_kt, 2), score_pair, 0)

    @pl.when(lax.rem(n_kt, 2) == 1)
    def _():
        score_tile(n_kt - 1)

    def key_to_float(key):
        return lax.bitcast_convert_type(key ^ ((key >> 31) & 0x7FFFFFFF), F32)

    def bit_body(i, carry):
        tau, n_tau = carry
        cand = tau ^ jnp.left_shift(jnp.int32(1), 31 - i)
        cand_f = key_to_float(cand)
        n = count(lambda s, kt: s >= cand_f)
        accept = n >= k_sel
        return jnp.where(accept, cand, tau), jnp.where(accept, n, n_tau)
    every = jnp.broadcast_to((n_kt * tk).astype(F32), (1, tq))
    tau, n_tau = lax.fori_loop(0, 32, bit_body, (jnp.full((1, tq), INT_MIN, jnp.int32), every))
    key_neg_inf = INT_MIN + 0x7FFFFF
    tau_f = jnp.where(tau <= key_neg_inf, neg_inf, key_to_float(tau))
    has_ties = jnp.max(n_tau) > k_sel

    def write_mask(sel):
        def body(kt, carry):
            s = sc_ref[kt]
            keep = sel(s, kt) & (s > neg_inf)
            o_ref[kt] = jnp.where(keep, 0.0, NEG_BIG).astype(o_ref.dtype)
            return carry
        lax.fori_loop(0, n_kt, body, 0)

    @pl.when(jnp.logical_not(has_ties))
    def _():
        write_mask(lambda s, kt: s >= tau_f)

    @pl.when(has_ties)
    def _():
        need = k_sel - count(lambda s, kt: s > tau_f)

        def idx_body(i, cut):
            cand = cut + jnp.left_shift(jnp.int32(1), 12 - i)
            n = count(lambda s, kt: (s == tau_f) & (key_pos(kt) < cand))
            return jnp.where(n <= need, cand, cut)
        cut = lax.fori_loop(0, 13, idx_body, jnp.zeros((1, tq), jnp.int32))
        write_mask(lambda s, kt: (s > tau_f) | ((s == tau_f) & (key_pos(kt) < cut)))

    def fill_body(kt, carry):
        o_ref[kt] = jnp.full((tk, tq), NEG_BIG, o_ref.dtype)
        return carry
    lax.fori_loop(n_kt, nk, fill_body, 0)


def _indexer(q_idx, kpe, kpo, w_t, tq, tk, cast_weights):
    bsz, seq, _ = kpe.shape
    nq, nk = seq // tq, seq // tk
    k_sel = min(TOPK_MAX, seq // 4)
    assert tq >= k_sel and tq % CHUNK == 0 and tk % tq == 0
    qi_cols = q_idx.shape[1]
    steps = bsz * nq
    bf16_sublanes = 16
    assert all(w.shape[0] % (steps * bf16_sublanes) == 0 for w in cast_weights)
    slab = lambda w: pl.BlockSpec((w.shape[0] // steps, w.shape[1]), lambda b, t: (b * nq + t, 0))
    n_cast = len(cast_weights)
    return pl.pallas_call(
        functools.partial(_indexer_kernel, tq=tq, tk=tk, k_sel=k_sel, n_cast=n_cast),
        out_shape=[jax.ShapeDtypeStruct((bsz, nq, nk, tk, tq), F32)]
        + [jax.ShapeDtypeStruct(w.shape, BF16) for w in cast_weights],
        grid=(bsz, nq),
        in_specs=[
            pl.BlockSpec((tq, qi_cols), lambda b, t: (b * nq + t, 0)),
            pl.BlockSpec((None, seq, 2 * IDX_DIM), lambda b, t: (b, 0, 0)),
            pl.BlockSpec((None, seq, 2 * IDX_DIM), lambda b, t: (b, 0, 0)),
            pl.BlockSpec((None, IDX_HEADS, tq), lambda b, t: (b, 0, t)),
        ] + [slab(w) for w in cast_weights],
        out_specs=[pl.BlockSpec((None, None, nk, tk, tq), lambda b, t: (b, t, 0, 0, 0))]
        + [slab(w) for w in cast_weights],
        scratch_shapes=[pltpu.VMEM((nk, tk, tq), F32)],
        compiler_params=_params("parallel", "parallel"),
        name="indexer",
    )(q_idx, kpe, kpo, w_t, *cast_weights)


def _attention_kernel(q_ref, k_ref, vt_ref, bias_ref, o_ref, m_ref, acc_ref, sa_ref, sb_ref, ma_ref, mb_ref,
                      *, tq, tk, group):
    qt = pl.program_id(2)
    n_kt = _n_key_tiles(qt, tq, tk)
    rows = HEAD_DIM + ONES_ROWS
    m_ref[...] = jnp.full(m_ref.shape, NEG_BIG, F32)
    acc_ref[...] = jnp.zeros(acc_ref.shape, F32)

    def scores(kt, s_ref, mx_ref):
        k0 = pl.multiple_of(kt * tk, tk)
        for g in range(group):
            cols = slice(g * HEAD_DIM, (g + 1) * HEAD_DIM)
            s = _nt_dot(k_ref[pl.ds(k0, tk), cols], q_ref[:, cols]) + bias_ref[kt]
            s_ref[g] = s
            mx_ref[g] = jnp.max(s, axis=0, keepdims=True)

    def consume(kt, s_ref, mx_ref):
        for g in range(group):
            m_old = m_ref[g]
            m_new = jnp.maximum(m_old, mx_ref[g])
            m_ref[g] = m_new
            p = jnp.exp2(s_ref[g] - m_new).astype(BF16)
            pv = _dot(vt_ref[kt, g * rows:(g + 1) * rows, :], p)
            acc_ref[g] = jnp.exp2(m_old - m_new) * acc_ref[g] + pv

    scores(0, sa_ref, ma_ref)
    n_pairs = lax.div(n_kt - 1, 2)

    def pair_body(i, carry):
        kt = 2 * i
        scores(kt + 1, sb_ref, mb_ref)
        consume(kt, sa_ref, ma_ref)
        scores(kt + 2, sa_ref, ma_ref)
        consume(kt + 1, sb_ref, mb_ref)
        return carry
    lax.fori_loop(0, n_pairs, pair_body, 0)
    kt = 2 * n_pairs

    @pl.when(n_kt - kt == 1)
    def _():
        consume(kt, sa_ref, ma_ref)

    @pl.when(n_kt - kt == 2)
    def _():
        scores(kt + 1, sb_ref, mb_ref)
        consume(kt, sa_ref, ma_ref)
        consume(kt + 1, sb_ref, mb_ref)

    for g in range(group):
        acc = acc_ref[g]
        out = acc[:HEAD_DIM] / acc[HEAD_DIM:HEAD_DIM + 1]
        o_ref[:, g * HEAD_DIM:(g + 1) * HEAD_DIM] = out.T.astype(o_ref.dtype)


def _attention(q, k, vt, bias, tq, tk, group=8):
    bsz, nk, vrows, _ = vt.shape
    rows = HEAD_DIM + ONES_ROWS
    hd = vrows // rows * HEAD_DIM
    seq = nk * tk
    nq = seq // tq
    gw = group * HEAD_DIM
    return pl.pallas_call(
        functools.partial(_attention_kernel, tq=tq, tk=tk, group=group),
        out_shape=jax.ShapeDtypeStruct((bsz * seq, hd), BF16),
        grid=(bsz, hd // gw, nq),
        in_specs=[
            pl.BlockSpec((tq, gw), lambda b, g, t: (b * nq + t, g)),
            pl.BlockSpec((None, seq, gw), lambda b, g, t: (b, 0, g)),
            pl.BlockSpec((None, nk, group * rows, tk), lambda b, g, t: (b, 0, g, 0)),
            pl.BlockSpec((None, None, nk, tk, tq), lambda b, g, t: (b, t, 0, 0, 0)),
        ],
        out_specs=pl.BlockSpec((tq, gw), lambda b, g, t: (b * nq + t, g)),
        scratch_shapes=[pltpu.VMEM((group, 1, tq), F32), pltpu.VMEM((group, rows, tq), F32),
                        pltpu.VMEM((group, tk, tq), F32), pltpu.VMEM((group, tk, tq), F32),
                        pltpu.VMEM((group, 1, tq), F32), pltpu.VMEM((group, 1, tq), F32)],
        compiler_params=_params("parallel", "parallel", "arbitrary"),
        name="attention",
    )(q, k.reshape(bsz, seq, hd), vt, bias)


def _merge_kernel(xn_ref, ya_ref, yb_ref, wga_ref, wgb_ref, woa_ref, wob_ref, o_ref):
    xn = xn_ref[...]
    ga = jax.nn.sigmoid(_dot(xn, wga_ref[...]))
    gb = jax.nn.sigmoid(_dot(xn, wgb_ref[...]))
    o_ref[...] = (ga * _dot(ya_ref[...], woa_ref[...]) + gb * _dot(yb_ref[...], wob_ref[...])).astype(o_ref.dtype)


def _merge(xn, ya, yb, w_ga, w_gb, w_oa, w_ob, tm=1024, tn=512):
    m, d = xn.shape
    row = lambda k: pl.BlockSpec((tm, k), lambda i, j: (i, 0))
    colw = lambda k: pl.BlockSpec((k, tn), lambda i, j: (0, j))
    return pl.pallas_call(
        _merge_kernel,
        out_shape=jax.ShapeDtypeStruct((m, d), BF16),
        grid=(m // tm, d // tn),
        in_specs=[row(d), row(ya.shape[1]), row(yb.shape[1]),
                  colw(d), colw(d), colw(ya.shape[1]), colw(yb.shape[1])],
        out_specs=pl.BlockSpec((tm, tn), lambda i, j: (i, j)),
        compiler_params=_params("parallel", "arbitrary"),
        name="merge",
    )(xn, ya, yb, w_ga, w_gb, w_oa, w_ob)


def _out_kernel(x_ref, mix_ref, w_ref, g_ref, h_ref, hn_ref):
    h = x_ref[...] + _dot(mix_ref[...], w_ref[...])
    h_ref[...] = h
    hn_ref[...] = _rms(h, g_ref[...]).astype(hn_ref.dtype)


def _out_proj(x, mixed, w_out, g, tm=512):
    m, d = x.shape
    row = pl.BlockSpec((tm, d), lambda i: (i, 0))
    return pl.pallas_call(
        _out_kernel,
        out_shape=(jax.ShapeDtypeStruct((m, d), F32), jax.ShapeDtypeStruct((m, d), BF16)),
        grid=(m // tm,),
        in_specs=[row, row, pl.BlockSpec((d, d), lambda i: (0, 0)), pl.BlockSpec((1, d), lambda i: (0, 0))],
        out_specs=(row, row),
        compiler_params=_params("parallel"),
        name="out_proj",
    )(x, mixed, w_out, g.reshape(1, d))


def _ffn_kernel(hn_ref, h_ref, wg_ref, wu_ref, wd_ref, g_ref, o_ref, acc_ref, *, final_norm):
    j = pl.program_id(1)

    @pl.when(j == 0)
    def _():
        acc_ref[...] = jnp.zeros(acc_ref.shape, F32)

    hn = hn_ref[...]
    t = (jax.nn.silu(_dot(hn, wg_ref[...])) * _dot(hn, wu_ref[...])).astype(BF16)
    acc_ref[...] += _dot(t, wd_ref[...])

    @pl.when(j == pl.num_programs(1) - 1)
    def _():
        h2 = h_ref[...] + acc_ref[...]
        o_ref[...] = _rms(h2, g_ref[...]) if final_norm else h2


def _ffn(hn, h, w_gate, w_up, w_down, g, final_norm, tm=512, tf=512):
    m, d = hn.shape
    f = w_gate.shape[1]
    row = pl.BlockSpec((tm, d), lambda i, j: (i, 0))
    return pl.pallas_call(
        functools.partial(_ffn_kernel, final_norm=final_norm),
        out_shape=jax.ShapeDtypeStruct((m, d), F32),
        grid=(m // tm, f // tf),
        in_specs=[row, row,
                  pl.BlockSpec((d, tf), lambda i, j: (0, j)),
                  pl.BlockSpec((d, tf), lambda i, j: (0, j)),
                  pl.BlockSpec((tf, d), lambda i, j: (j, 0)),
                  pl.BlockSpec((1, d), lambda i, j: (0, 0))],
        out_specs=row,
        scratch_shapes=[pltpu.VMEM((tm, d), F32)],
        compiler_params=_params("parallel", "arbitrary"),
        name="ffn",
    )(hn, h, w_gate, w_up, w_down, g.reshape(1, d))


def kernel(x, norm1_g, w_in, a_ln_g, a_ln_b, a_w_s, a_b_s, kv_norm_g, w_uk, w_uv, w_oa, w_ob, w_out, norm2_g,
           w_ff_gate, w_ff_up, w_ff_down, final_g):
    bsz, seq, d = x.shape
    m = bsz * seq
    depth = norm1_g.shape[0]
    a_width = a_ln_g.shape[1]
    lat = kv_norm_g.shape[1]
    heads, head_dim = w_uk.shape[2], w_uk.shape[3]
    assert head_dim == HEAD_DIM
    hd = heads * head_dim
    qi = IDX_HEADS * IDX_DIM
    c_q = 2 * a_width
    c_kv = c_q + hd
    c_qi = c_kv + lat
    c_ki = c_qi + qi
    c_wi = c_ki + IDX_DIM
    c_g = c_wi + IDX_HEADS

    h = x.reshape(m, d)
    for l in range(depth):
        w = w_in[l].astype(BF16)
        w_ga = w[:, c_g:c_g + d]
        w_gb = w[:, c_g + d:]

        y_a, xn = _branch_a(h, norm1_g[l], w, a_ln_g[l], a_ln_b[l], a_w_s[l], a_b_s[l])
        q, q_idx, kw = _q_projections(xn, w, c_q, hd, c_qi, qi, c_ki, HEAD_DIM ** -0.5 * LOG2E)
        k_tok, v_t = _kv_proj(xn, w, c_kv, kv_norm_g[l], w_uk[l].reshape(lat, hd).astype(BF16),
                              w_uv[l].reshape(lat, hd).T.astype(BF16), bsz, seq, ATT_TK)
        k_idx = kw[:, :IDX_DIM].astype(BF16).reshape(bsz, seq, IDX_DIM)
        zeros = jnp.zeros_like(k_idx)
        kpe = jnp.concatenate([k_idx, zeros], axis=-1)
        kpo = jnp.concatenate([zeros, k_idx], axis=-1)
        w_t = kw[:, IDX_DIM:IDX_DIM + IDX_HEADS].reshape(bsz, seq, IDX_HEADS).transpose(0, 2, 1)
        bias, w_oa_b, w_ob_b, w_out_b, w_fg_b, w_fu_b, w_fd_b = _indexer(
            q_idx, kpe, kpo, w_t, ATT_TQ, ATT_TK,
            [w_oa[l], w_ob[l], w_out[l], w_ff_gate[l], w_ff_up[l], w_ff_down[l]])
        y_b = _attention(q, k_tok, v_t, bias, ATT_TQ, ATT_TK)
        mixed = _merge(xn, y_a, y_b, w_ga, w_gb, w_oa_b, w_ob_b)
        h, hn = _out_proj(h, mixed, w_out_b, norm2_g[l])
        h = _ffn(hn, h, w_fg_b, w_fu_b, w_fd_b, final_g, final_norm=(l == depth - 1))
    return h.reshape(bsz, seq, d)
```

```python
import functools
import math

import jax
import jax.numpy as jnp
from jax import lax
from jax.experimental import pallas as pl
from jax.experimental.pallas import tpu as pltpu

EPS = 1e-6
CHUNK = 64
A_GROUPS = 8
A_BLOCK = 128
LANES = 128
BF16_SUBLANES = 16
HEAD_DIM = 128
IDX_DIM = 64
IDX_HEADS = 16
TOPK_MAX = 256
ONES_ROWS = 16
LOG2E = 1.4426950408889634

VMEM_LIMIT_BYTES = 56 * 1024 * 1024
INT_MIN = -2147483648
ATT_TQ = 256
ATT_TK = 512
NEG_BIG = -1e30

F32 = jnp.float32
BF16 = jnp.bfloat16


def _params(*sem, flags=None):
    return pltpu.CompilerParams(dimension_semantics=sem, vmem_limit_bytes=VMEM_LIMIT_BYTES, flags=flags)


def _nt_dot(a, b):
    return lax.dot_general(a, b, (((1,), (1,)), ((), ())), preferred_element_type=F32)


def _dot(a, b):
    return jnp.dot(a, b, preferred_element_type=F32)


def _rms(x, g):
    return x * lax.rsqrt(jnp.mean(x * x, axis=-1, keepdims=True) + EPS) * g


def _row_specs(cols, row0, n):
    bh = math.gcd(row0, n) if row0 else n
    assert bh % LANES == 0
    return [pl.BlockSpec((bh, cols), functools.partial(lambda i, ib: (ib, 0), ib=row0 // bh + c)) for c in range(n // bh)]


def _qproj_kernel(xn_ref, *refs, n_q, n_qi, q_scale):
    wq, wqi, wkw = refs[:n_q], refs[n_q:n_q + n_qi], refs[n_q + n_qi]
    q_ref, qi_ref, kw_ref = refs[n_q + n_qi + 1:]
    xn = xn_ref[...]
    for c, w_ref in enumerate(wq):
        bw = w_ref.shape[0]
        q_ref[:, c * bw:(c + 1) * bw] = (_nt_dot(xn, w_ref[...]) * q_scale).astype(q_ref.dtype)
    for c, w_ref in enumerate(wqi):
        bw = w_ref.shape[0]
        qi_ref[:, c * bw:(c + 1) * bw] = _nt_dot(xn, w_ref[...]).astype(qi_ref.dtype)
    kw_ref[...] = _nt_dot(xn, wkw[...])


def _q_projections(xn, w, c_q, n_q, c_qi, n_qi, c_kw, q_scale, tm=512):
    m, d = xn.shape
    q_specs, qi_specs, kw_specs = _row_specs(d, c_q, n_q), _row_specs(d, c_qi, n_qi), _row_specs(d, c_kw, LANES)
    row = lambda n: pl.BlockSpec((tm, n), lambda i: (i, 0))
    return pl.pallas_call(
        functools.partial(_qproj_kernel, n_q=len(q_specs), n_qi=len(qi_specs), q_scale=q_scale),
        out_shape=(jax.ShapeDtypeStruct((m, n_q), BF16), jax.ShapeDtypeStruct((m, n_qi), BF16),
                   jax.ShapeDtypeStruct((m, LANES), F32)),
        grid=(m // tm,),
        in_specs=[row(d)] + q_specs + qi_specs + kw_specs,
        out_specs=(row(n_q), row(n_qi), row(LANES)),
        compiler_params=_params("parallel"),
        name="q_proj",
    )(xn, *([w] * (len(q_specs) + len(qi_specs) + 1)))


def _branch_a_kernel(x_ref, g1_ref, w_ref, lng_ref, lnb_ref, ws_ref, bst_ref, o_ref, xn_ref, z_ref, *, tm, tn, width):
    j = pl.program_id(1)

    @pl.when(j == 0)
    def _():
        xn_ref[...] = _rms(x_ref[...], g1_ref[...]).astype(xn_ref.dtype)

    z_ref[j] = jax.nn.gelu(_nt_dot(xn_ref[...], w_ref[...]))

    @pl.when(j == pl.num_programs(1) - 1)
    def _():
        per_half = width // tn
        gw = width // A_GROUPS
        row = lax.broadcasted_iota(jnp.int32, (A_BLOCK, A_BLOCK), 0)
        col = lax.broadcasted_iota(jnp.int32, (A_BLOCK, A_BLOCK), 1)
        causal = (col // CHUNK) <= (row // CHUNK)
        for r in range(tm // A_BLOCK):
            rows = slice(r * A_BLOCK, (r + 1) * A_BLOCK)
            v = jnp.concatenate([z_ref[per_half + c, rows, :] for c in range(per_half)], axis=-1)
            vc = v - jnp.mean(v, axis=-1, keepdims=True)
            vn = vc * lax.rsqrt(jnp.mean(vc * vc, axis=-1, keepdims=True) + EPS)
            vn = (vn * lng_ref[...] + lnb_ref[...]).astype(BF16)
            for g in range(A_GROUPS):
                wm = jnp.where(causal, ws_ref[g], 0.0).astype(BF16)
                sv = _dot(wm, vn[:, g * gw:(g + 1) * gw]) + bst_ref[:, g:g + 1]
                c, off = divmod(g * gw, tn)
                u = z_ref[c, rows, off:off + gw]
                o_ref[rows, g * gw:(g + 1) * gw] = (u * sv).astype(o_ref.dtype)


def _branch_a(x, g1, w_za, ln_g, ln_b, w_s, b_s, tm=512, tn=1024):
    m, d = x.shape
    width = ln_g.shape[0]
    assert (width // A_GROUPS) <= tn and tn % (width // A_GROUPS) == 0
    kern = functools.partial(_branch_a_kernel, tm=tm, tn=tn, width=width)
    return pl.pallas_call(
        kern,
        out_shape=(jax.ShapeDtypeStruct((m, width), BF16), jax.ShapeDtypeStruct((m, d), BF16)),
        grid=(m // tm, 2 * width // tn),
        in_specs=[
            pl.BlockSpec((tm, d), lambda i, j: (i, 0)),
            pl.BlockSpec((1, d), lambda i, j: (0, 0)),
            pl.BlockSpec((tn, d), lambda i, j: (j, 0)),
            pl.BlockSpec((1, width), lambda i, j: (0, 0)),
            pl.BlockSpec((1, width), lambda i, j: (0, 0)),
            pl.BlockSpec((A_GROUPS, A_BLOCK, A_BLOCK), lambda i, j: (0, 0, 0)),
            pl.BlockSpec((A_BLOCK, A_GROUPS), lambda i, j: (0, 0)),
        ],
        out_specs=(pl.BlockSpec((tm, width), lambda i, j: (i, 0)), pl.BlockSpec((tm, d), lambda i, j: (i, 0))),
        scratch_shapes=[pltpu.VMEM((2 * width // tn, tm, tn), F32)],
        compiler_params=_params("parallel", "arbitrary"),
        name="branch_a",
    )(x, g1.reshape(1, d), w_za, ln_g.reshape(1, width), ln_b.reshape(1, width), w_s, b_s.T)


def _kv_kernel(xn_ref, wc_ref, g_ref, wuk_ref, wuvt_ref, k_ref, vt_ref):
    c = _nt_dot(xn_ref[...], wc_ref[...])
    cn = _rms(c, g_ref[...]).astype(BF16)
    k_ref[...] = _dot(cn, wuk_ref[...]).astype(k_ref.dtype)
    vt = _nt_dot(wuvt_ref[...], cn).astype(vt_ref.dtype)
    rows = HEAD_DIM + ONES_ROWS
    for h in range(vt.shape[0] // HEAD_DIM):
        vt_ref[h * rows:h * rows + HEAD_DIM, :] = vt[h * HEAD_DIM:(h + 1) * HEAD_DIM]
        vt_ref[h * rows + HEAD_DIM:(h + 1) * rows, :] = jnp.ones((ONES_ROWS, vt.shape[1]), vt_ref.dtype)


def _kv_proj(xn, w_c, col0, g, w_uk, w_uv_t, bsz, seq, tk):
    m, d = xn.shape
    lat, hd = w_uk.shape
    assert col0 % lat == 0
    j0 = col0 // lat
    nt = seq // tk
    vrows = hd // HEAD_DIM * (HEAD_DIM + ONES_ROWS)
    return pl.pallas_call(
        _kv_kernel,
        out_shape=(jax.ShapeDtypeStruct((m, hd), BF16), jax.ShapeDtypeStruct((bsz, nt, vrows, tk), BF16)),
        grid=(m // tk,),
        in_specs=[
            pl.BlockSpec((tk, d), lambda i: (i, 0)),
            pl.BlockSpec((lat, d), lambda i: (j0, 0)),
            pl.BlockSpec((1, lat), lambda i: (0, 0)),
            pl.BlockSpec((lat, hd), lambda i: (0, 0)),
            pl.BlockSpec((hd, lat), lambda i: (0, 0)),
        ],
        out_specs=(pl.BlockSpec((tk, hd), lambda i: (i, 0)),
                   pl.BlockSpec((None, None, vrows, tk), lambda i: (i // nt, i % nt, 0, 0))),
        compiler_params=_params("parallel"),
        name="kv_proj",
    )(xn, w_c, g.reshape(1, lat), w_uk, w_uv_t)


def _n_key_tiles(qt, tq, tk):
    return lax.div(qt * tq + tq + tk - 1, tk)


def _indexer_kernel(qi_ref, kpe_ref, kpo_ref, wt_ref, *refs, tq, tk, k_sel, n_cast):
    cast_in, o_ref, cast_out, sc_ref = refs[:n_cast], refs[n_cast], refs[n_cast + 1:2 * n_cast + 1], refs[-1]
    for w_ref, wb_ref in zip(cast_in, cast_out):
        wb_ref[...] = w_ref[...].astype(wb_ref.dtype)

    qt = pl.program_id(1)
    n_kt = _n_key_tiles(qt, tq, tk)
    nk = sc_ref.shape[0]
    neg_inf = -jnp.inf

    def count(pred):
        rows = 16
        def body(kt, cnt):
            hit = jnp.where(pred(sc_ref[kt], kt), 1.0, 0.0)
            return cnt + jnp.sum(hit.reshape(tk // rows, rows, tq), axis=0)
        cnt = lax.fori_loop(0, n_kt, body, jnp.zeros((rows, tq), F32))
        return jnp.sum(cnt, axis=0, keepdims=True)

    def key_pos(kt):
        return kt * tk + lax.broadcasted_iota(jnp.int32, (tk, tq), 0)

    wt = wt_ref[...] * (IDX_DIM ** -0.5 * IDX_HEADS ** -0.5)
    q_chunk = (qt * tq + lax.broadcasted_iota(jnp.int32, (1, tq), 1)) // CHUNK

    def score_tile(kt):
        k0 = pl.multiple_of(kt * tk, tk)
        ke = kpe_ref[pl.ds(k0, tk), :]
        ko = kpo_ref[pl.ds(k0, tk), :]
        acc = jnp.zeros((tk, tq), F32)
        for j in range(IDX_HEADS // 2):
            qp = qi_ref[:, j * 128:(j + 1) * 128]
            acc = acc + wt[2 * j:2 * j + 1, :] * jnp.maximum(_nt_dot(ke, qp), 0.0)
            acc = acc + wt[2 * j + 1:2 * j + 2, :] * jnp.maximum(_nt_dot(ko, qp), 0.0)
        k_chunk = (k0 + lax.broadcasted_iota(jnp.int32, (tk, 1), 0)) // CHUNK
        sc_ref[kt] = jnp.where(k_chunk <= q_chunk, acc, neg_inf)

    def score_pair(i, carry):
        score_tile(2 * i)
        score_tile(2 * i + 1)
        return carry
    lax.fori_loop(0, lax.div(n_kt, 2), score_pair, 0)

    @pl.when(lax.rem(n_kt, 2) == 1)
    def _():
        score_tile(n_kt - 1)

    def key_to_float(key):
        return lax.bitcast_convert_type(key ^ ((key >> 31) & 0x7FFFFFFF), F32)

    def bit_body(i, carry):
        tau, n_tau = carry
        cand = tau ^ jnp.left_shift(jnp.int32(1), 31 - i)
        cand_f = key_to_float(cand)
        n = count(lambda s, kt: s >= cand_f)
        accept = n >= k_sel
        return jnp.where(accept, cand, tau), jnp.where(accept, n, n_tau)
    every = jnp.broadcast_to((n_kt * tk).astype(F32), (1, tq))
    tau, n_tau = lax.fori_loop(0, 32, bit_body, (jnp.full((1, tq), INT_MIN, jnp.int32), every))
    key_neg_inf = INT_MIN + 0x7FFFFF
    tau_f = jnp.where(tau <= key_neg_inf, neg_inf, key_to_float(tau))
    has_ties = jnp.max(n_tau) > k_sel

    def write_mask(sel):
        def body(kt, carry):
            s = sc_ref[kt]
            keep = sel(s, kt) & (s > neg_inf)
            o_ref[kt] = jnp.where(keep, 0.0, NEG_BIG).astype(o_ref.dtype)
            return carry
        lax.fori_loop(0, n_kt, body, 0)

    @pl.when(jnp.logical_not(has_ties))
    def _():
        write_mask(lambda s, kt: s >= tau_f)

    @pl.when(has_ties)
    def _():
        need = k_sel - count(lambda s, kt: s > tau_f)

        def idx_body(i, cut):
            cand = cut + jnp.left_shift(jnp.int32(1), 12 - i)
            n = count(lambda s, kt: (s == tau_f) & (key_pos(kt) < cand))
            return jnp.where(n <= need, cand, cut)
        cut = lax.fori_loop(0, 13, idx_body, jnp.zeros((1, tq), jnp.int32))
        write_mask(lambda s, kt: (s > tau_f) | ((s == tau_f) & (key_pos(kt) < cut)))

    def fill_body(kt, carry):
        o_ref[kt] = jnp.full((tk, tq), NEG_BIG, o_ref.dtype)
        return carry
    lax.fori_loop(n_kt, nk, fill_body, 0)


def _indexer(q_idx, kpe, kpo, w_t, tq, tk, cast_weights):
    bsz, seq, _ = kpe.shape
    nq, nk = seq // tq, seq // tk
    k_sel = min(TOPK_MAX, seq // 4)
    assert tq >= k_sel and tq % CHUNK == 0 and tk % tq == 0
    qi_cols = q_idx.shape[1]
    steps = bsz * nq
    assert all(w.shape[0] % (steps * BF16_SUBLANES) == 0 for w in cast_weights)
    slab = lambda w: pl.BlockSpec((w.shape[0] // steps, w.shape[1]), lambda b, t: (b * nq + t, 0))
    n_cast = len(cast_weights)
    return pl.pallas_call(
        functools.partial(_indexer_kernel, tq=tq, tk=tk, k_sel=k_sel, n_cast=n_cast),
        out_shape=[jax.ShapeDtypeStruct((bsz, nq, nk, tk, tq), F32)]
        + [jax.ShapeDtypeStruct(w.shape, BF16) for w in cast_weights],
        grid=(bsz, nq),
        in_specs=[
            pl.BlockSpec((tq, qi_cols), lambda b, t: (b * nq + t, 0)),
            pl.BlockSpec((None, seq, 2 * IDX_DIM), lambda b, t: (b, 0, 0)),
            pl.BlockSpec((None, seq, 2 * IDX_DIM), lambda b, t: (b, 0, 0)),
            pl.BlockSpec((None, IDX_HEADS, tq), lambda b, t: (b, 0, t)),
        ] + [slab(w) for w in cast_weights],
        out_specs=[pl.BlockSpec((None, None, nk, tk, tq), lambda b, t: (b, t, 0, 0, 0))]
        + [slab(w) for w in cast_weights],
        scratch_shapes=[pltpu.VMEM((nk, tk, tq), F32)],
        compiler_params=_params("parallel", "parallel"),
        name="indexer",
    )(q_idx, kpe, kpo, w_t, *cast_weights)


def _attention_kernel(q_ref, k_ref, vt_ref, bias_ref, o_ref, m_ref, acc_ref, sa_ref, sb_ref, ma_ref, mb_ref,
                      *, tq, tk, group):
    qt = pl.program_id(2)
    n_kt = _n_key_tiles(qt, tq, tk)
    rows = HEAD_DIM + ONES_ROWS
    m_ref[...] = jnp.full(m_ref.shape, NEG_BIG, F32)
    acc_ref[...] = jnp.zeros(acc_ref.shape, F32)

    def scores(kt, s_ref, mx_ref):
        k0 = pl.multiple_of(kt * tk, tk)
        for g in range(group):
            cols = slice(g * HEAD_DIM, (g + 1) * HEAD_DIM)
            s = _nt_dot(k_ref[pl.ds(k0, tk), cols], q_ref[:, cols]) + bias_ref[kt]
            s_ref[g] = s
            mx_ref[g] = jnp.max(s, axis=0, keepdims=True)

    def consume(kt, s_ref, mx_ref):
        for g in range(group):
            m_old = m_ref[g]
            m_new = jnp.maximum(m_old, mx_ref[g])
            m_ref[g] = m_new
            p = jnp.exp2(s_ref[g] - m_new).astype(BF16)
            pv = _dot(vt_ref[kt, g * rows:(g + 1) * rows, :], p)
            acc_ref[g] = jnp.exp2(m_old - m_new) * acc_ref[g] + pv

    scores(0, sa_ref, ma_ref)
    n_pairs = lax.div(n_kt - 1, 2)

    def pair_body(i, carry):
        kt = 2 * i
        scores(kt + 1, sb_ref, mb_ref)
        consume(kt, sa_ref, ma_ref)
        scores(kt + 2, sa_ref, ma_ref)
        consume(kt + 1, sb_ref, mb_ref)
        return carry
    lax.fori_loop(0, n_pairs, pair_body, 0)
    kt = 2 * n_pairs

    @pl.when(n_kt - kt == 1)
    def _():
        consume(kt, sa_ref, ma_ref)

    @pl.when(n_kt - kt == 2)
    def _():
        scores(kt + 1, sb_ref, mb_ref)
        consume(kt, sa_ref, ma_ref)
        consume(kt + 1, sb_ref, mb_ref)

    for g in range(group):
        acc = acc_ref[g]
        out = acc[:HEAD_DIM] / acc[HEAD_DIM:HEAD_DIM + 1]
        o_ref[:, g * HEAD_DIM:(g + 1) * HEAD_DIM] = out.T.astype(o_ref.dtype)


def _attention(q, k, vt, bias, tq, tk, group=8):
    bsz, nk, vrows, _ = vt.shape
    rows = HEAD_DIM + ONES_ROWS
    hd = vrows // rows * HEAD_DIM
    seq = nk * tk
    nq = seq // tq
    gw = group * HEAD_DIM
    return pl.pallas_call(
        functools.partial(_attention_kernel, tq=tq, tk=tk, group=group),
        out_shape=jax.ShapeDtypeStruct((bsz * seq, hd), BF16),
        grid=(bsz, hd // gw, nq),
        in_specs=[
            pl.BlockSpec((tq, gw), lambda b, g, t: (b * nq + t, g)),
            pl.BlockSpec((None, seq, gw), lambda b, g, t: (b, 0, g)),
            pl.BlockSpec((None, nk, group * rows, tk), lambda b, g, t: (b, 0, g, 0)),
            pl.BlockSpec((None, None, nk, tk, tq), lambda b, g, t: (b, t, 0, 0, 0)),
        ],
        out_specs=pl.BlockSpec((tq, gw), lambda b, g, t: (b * nq + t, g)),
        scratch_shapes=[pltpu.VMEM((group, 1, tq), F32), pltpu.VMEM((group, rows, tq), F32),
                        pltpu.VMEM((group, tk, tq), F32), pltpu.VMEM((group, tk, tq), F32),
                        pltpu.VMEM((group, 1, tq), F32), pltpu.VMEM((group, 1, tq), F32)],
        compiler_params=_params("parallel", "parallel", "arbitrary"),
        name="attention",
    )(q, k.reshape(bsz, seq, hd), vt, bias)


def _merge_kernel(xn_ref, ya_ref, yb_ref, wga_ref, wgb_ref, woa_ref, wob_ref, o_ref):
    xn = xn_ref[...]
    ga = jax.nn.sigmoid(_nt_dot(xn, wga_ref[...]))
    gb = jax.nn.sigmoid(_nt_dot(xn, wgb_ref[...]))
    o_ref[...] = (ga * _dot(ya_ref[...], woa_ref[...]) + gb * _dot(yb_ref[...], wob_ref[...])).astype(o_ref.dtype)


def _merge(xn, ya, yb, w_t, row_ga, row_gb, w_oa, w_ob, tm=1024, tn=512):
    m, d = xn.shape
    row = lambda k: pl.BlockSpec((tm, k), lambda i, j: (i, 0))
    colw = lambda k: pl.BlockSpec((k, tn), lambda i, j: (0, j))
    assert row_ga % BF16_SUBLANES == 0 and row_gb % BF16_SUBLANES == 0
    gate = lambda r0: pl.BlockSpec((pl.Element(tn), pl.Element(d)),
                                   lambda i, j: (pl.multiple_of(r0 + j * tn, BF16_SUBLANES), 0))
    return pl.pallas_call(
        _merge_kernel,
        out_shape=jax.ShapeDtypeStruct((m, d), BF16),
        grid=(m // tm, d // tn),
        in_specs=[row(d), row(ya.shape[1]), row(yb.shape[1]),
                  gate(row_ga), gate(row_gb), colw(ya.shape[1]), colw(yb.shape[1])],
        out_specs=pl.BlockSpec((tm, tn), lambda i, j: (i, j)),
        compiler_params=_params("parallel", "arbitrary"),
        name="merge",
    )(xn, ya, yb, w_t, w_t, w_oa, w_ob)


def _out_kernel(x_ref, mix_ref, w_ref, g_ref, h_ref, hn_ref):
    h = x_ref[...] + _dot(mix_ref[...], w_ref[...])
    h_ref[...] = h
    hn_ref[...] = _rms(h, g_ref[...]).astype(hn_ref.dtype)


def _out_proj(x, mixed, w_out, g, tm=512):
    m, d = x.shape
    row = pl.BlockSpec((tm, d), lambda i: (i, 0))
    return pl.pallas_call(
        _out_kernel,
        out_shape=(jax.ShapeDtypeStruct((m, d), F32), jax.ShapeDtypeStruct((m, d), BF16)),
        grid=(m // tm,),
        in_specs=[row, row, pl.BlockSpec((d, d), lambda i: (0, 0)), pl.BlockSpec((1, d), lambda i: (0, 0))],
        out_specs=(row, row),
        compiler_params=_params("parallel"),
        name="out_proj",
    )(x, mixed, w_out, g.reshape(1, d))


def _ffn_kernel(hn_ref, h_ref, wg_ref, wu_ref, wd_ref, g_ref, o_ref, acc_ref, *, final_norm):
    j = pl.program_id(1)

    @pl.when(j == 0)
    def _():
        acc_ref[...] = jnp.zeros(acc_ref.shape, F32)

    hn = hn_ref[...]
    t = (jax.nn.silu(_dot(hn, wg_ref[...])) * _dot(hn, wu_ref[...])).astype(BF16)
    acc_ref[...] += _dot(t, wd_ref[...])

    @pl.when(j == pl.num_programs(1) - 1)
    def _():
        h2 = h_ref[...] + acc_ref[...]
        o_ref[...] = _rms(h2, g_ref[...]) if final_norm else h2


def _ffn(hn, h, w_gate, w_up, w_down, g, final_norm, tm=512, tf=512):
    m, d = hn.shape
    f = w_gate.shape[1]
    row = pl.BlockSpec((tm, d), lambda i, j: (i, 0))
    return pl.pallas_call(
        functools.partial(_ffn_kernel, final_norm=final_norm),
        out_shape=jax.ShapeDtypeStruct((m, d), F32),
        grid=(m // tm, f // tf),
        in_specs=[row, row,
                  pl.BlockSpec((d, tf), lambda i, j: (0, j)),
                  pl.BlockSpec((d, tf), lambda i, j: (0, j)),
                  pl.BlockSpec((tf, d), lambda i, j: (j, 0)),
                  pl.BlockSpec((1, d), lambda i, j: (0, 0))],
        out_specs=row,
        scratch_shapes=[pltpu.VMEM((tm, d), F32)],
        compiler_params=_params("parallel", "arbitrary"),
        name="ffn",
    )(hn, h, w_gate, w_up, w_down, g.reshape(1, d))


def kernel(x, norm1_g, w_in, a_ln_g, a_ln_b, a_w_s, a_b_s, kv_norm_g, w_uk, w_uv, w_oa, w_ob, w_out, norm2_g,
           w_ff_gate, w_ff_up, w_ff_down, final_g):
    bsz, seq, d = x.shape
    m = bsz * seq
    depth = norm1_g.shape[0]
    a_width = a_ln_g.shape[1]
    lat = kv_norm_g.shape[1]
    heads, head_dim = w_uk.shape[2], w_uk.shape[3]
    assert head_dim == HEAD_DIM
    hd = heads * head_dim
    qi = IDX_HEADS * IDX_DIM
    c_q = 2 * a_width
    c_kv = c_q + hd
    c_qi = c_kv + lat
    c_ki = c_qi + qi
    c_wi = c_ki + IDX_DIM
    c_g = c_wi + IDX_HEADS

    h = x.reshape(m, d)
    for l in range(depth):
        w = w_in[l].T.astype(BF16)

        y_a, xn = _branch_a(h, norm1_g[l], w, a_ln_g[l], a_ln_b[l], a_w_s[l], a_b_s[l])
        q, q_idx, kw = _q_projections(xn, w, c_q, hd, c_qi, qi, c_ki, HEAD_DIM ** -0.5 * LOG2E)
        k_tok, v_t = _kv_proj(xn, w, c_kv, kv_norm_g[l], w_uk[l].reshape(lat, hd).astype(BF16),
                              w_uv[l].reshape(lat, hd).T.astype(BF16), bsz, seq, ATT_TK)
        k_idx = kw[:, :IDX_DIM].astype(BF16).reshape(bsz, seq, IDX_DIM)
        zeros = jnp.zeros_like(k_idx)
        kpe = jnp.concatenate([k_idx, zeros], axis=-1)
        kpo = jnp.concatenate([zeros, k_idx], axis=-1)
        w_t = kw[:, IDX_DIM:IDX_DIM + IDX_HEADS].reshape(bsz, seq, IDX_HEADS).transpose(0, 2, 1)
        bias, w_oa_b, w_ob_b, w_out_b, w_fg_b, w_fu_b, w_fd_b = _indexer(
            q_idx, kpe, kpo, w_t, ATT_TQ, ATT_TK,
            [w_oa[l], w_ob[l], w_out[l], w_ff_gate[l], w_ff_up[l], w_ff_down[l]])
        y_b = _attention(q, k_tok, v_t, bias, ATT_TQ, ATT_TK)
        mixed = _merge(xn, y_a, y_b, w, c_g, c_g + d, w_oa_b, w_ob_b)
        h, hn = _out_proj(h, mixed, w_out_b, norm2_g[l])
        h = _ffn(hn, h, w_fg_b, w_fu_b, w_fd_b, final_g, final_norm=(l == depth - 1))
    return h.reshape(bsz, seq, d)
```

```python
import functools
import math

import jax
import jax.numpy as jnp
from jax import lax
from jax.experimental import pallas as pl
from jax.experimental.pallas import tpu as pltpu

EPS = 1e-6
CHUNK = 64
A_GROUPS = 8
A_BLOCK = 128
LANES = 128
BF16_SUBLANES = 16
HEAD_DIM = 128
IDX_DIM = 64
IDX_HEADS = 16
TOPK_MAX = 256
ONES_ROWS = 16
LOG2E = 1.4426950408889634

VMEM_LIMIT_BYTES = 56 * 1024 * 1024
INT_MIN = -2147483648
ATT_TQ = 256
ATT_TK = 512
NEG_BIG = -1e30

F32 = jnp.float32
BF16 = jnp.bfloat16


def _params(*sem, flags=None):
    return pltpu.CompilerParams(dimension_semantics=sem, vmem_limit_bytes=VMEM_LIMIT_BYTES, flags=flags)


def _nt_dot(a, b):
    return lax.dot_general(a, b, (((1,), (1,)), ((), ())), preferred_element_type=F32)


def _dot(a, b):
    return jnp.dot(a, b, preferred_element_type=F32)


def _rms(x, g):
    return x * lax.rsqrt(jnp.mean(x * x, axis=-1, keepdims=True) + EPS) * g


def _row_specs(cols, row0, n):
    bh = math.gcd(row0, n) if row0 else n
    assert bh % LANES == 0
    return [pl.BlockSpec((bh, cols), functools.partial(lambda i, ib: (ib, 0), ib=row0 // bh + c)) for c in range(n // bh)]


def _qproj_kernel(xn_ref, *refs, n_q, n_qi, q_scale):
    wq, wqi, wkw = refs[:n_q], refs[n_q:n_q + n_qi], refs[n_q + n_qi]
    q_ref, qi_ref, kpe_ref, kpo_ref, wt_ref = refs[n_q + n_qi + 1:]
    xn = xn_ref[...]
    for c, w_ref in enumerate(wq):
        bw = w_ref.shape[0]
        q_ref[:, c * bw:(c + 1) * bw] = (_nt_dot(xn, w_ref[...]) * q_scale).astype(q_ref.dtype)
    for c, w_ref in enumerate(wqi):
        bw = w_ref.shape[0]
        qi_ref[:, c * bw:(c + 1) * bw] = _nt_dot(xn, w_ref[...]).astype(qi_ref.dtype)
    kw = _nt_dot(xn, wkw[...])
    lane = lax.broadcasted_iota(jnp.int32, kw.shape, 1)
    kpe_ref[...] = jnp.where(lane < IDX_DIM, kw, 0.0).astype(kpe_ref.dtype)
    kpo_ref[...] = jnp.where(lane >= LANES - IDX_DIM, pltpu.roll(kw, LANES - IDX_DIM, 1), 0.0).astype(kpo_ref.dtype)
    wt_ref[...] = kw.T[IDX_DIM:IDX_DIM + IDX_HEADS, :]


def _q_projections(xn, w, c_q, n_q, c_qi, n_qi, c_kw, q_scale, bsz, tm=512):
    m, d = xn.shape
    seq = m // bsz
    nt = seq // tm
    assert 2 * IDX_DIM == LANES
    pad_spec = pl.BlockSpec((None, tm, LANES), lambda i: (i // nt, i % nt, 0))
    q_specs, qi_specs, kw_specs = _row_specs(d, c_q, n_q), _row_specs(d, c_qi, n_qi), _row_specs(d, c_kw, LANES)
    row = lambda n: pl.BlockSpec((tm, n), lambda i: (i, 0))
    return pl.pallas_call(
        functools.partial(_qproj_kernel, n_q=len(q_specs), n_qi=len(qi_specs), q_scale=q_scale),
        out_shape=(jax.ShapeDtypeStruct((m, n_q), BF16), jax.ShapeDtypeStruct((m, n_qi), BF16),
                   jax.ShapeDtypeStruct((bsz, seq, LANES), BF16), jax.ShapeDtypeStruct((bsz, seq, LANES), BF16),
                   jax.ShapeDtypeStruct((bsz, IDX_HEADS, seq), F32)),
        grid=(m // tm,),
        in_specs=[row(d)] + q_specs + qi_specs + kw_specs,
        out_specs=(row(n_q), row(n_qi), pad_spec, pad_spec,
                   pl.BlockSpec((None, IDX_HEADS, tm), lambda i: (i // nt, 0, i % nt))),
        compiler_params=_params("parallel"),
        name="q_proj",
    )(xn, *([w] * (len(q_specs) + len(qi_specs) + 1)))


def _branch_a_kernel(x_ref, g1_ref, w_ref, lng_ref, lnb_ref, ws_ref, bst_ref, o_ref, xn_ref, z_ref, *, tm, tn, width):
    j = pl.program_id(1)

    @pl.when(j == 0)
    def _():
        xn_ref[...] = _rms(x_ref[...], g1_ref[...]).astype(xn_ref.dtype)

    z_ref[j] = jax.nn.gelu(_nt_dot(xn_ref[...], w_ref[...]))

    @pl.when(j == pl.num_programs(1) - 1)
    def _():
        per_half = width // tn
        gw = width // A_GROUPS
        row = lax.broadcasted_iota(jnp.int32, (A_BLOCK, A_BLOCK), 0)
        col = lax.broadcasted_iota(jnp.int32, (A_BLOCK, A_BLOCK), 1)
        causal = (col // CHUNK) <= (row // CHUNK)
        for r in range(tm // A_BLOCK):
            rows = slice(r * A_BLOCK, (r + 1) * A_BLOCK)
            v = jnp.concatenate([z_ref[per_half + c, rows, :] for c in range(per_half)], axis=-1)
            vc = v - jnp.mean(v, axis=-1, keepdims=True)
            vn = vc * lax.rsqrt(jnp.mean(vc * vc, axis=-1, keepdims=True) + EPS)
            vn = (vn * lng_ref[...] + lnb_ref[...]).astype(BF16)
            for g in range(A_GROUPS):
                wm = jnp.where(causal, ws_ref[g], 0.0).astype(BF16)
                sv = _dot(wm, vn[:, g * gw:(g + 1) * gw]) + bst_ref[:, g:g + 1]
                c, off = divmod(g * gw, tn)
                u = z_ref[c, rows, off:off + gw]
                o_ref[rows, g * gw:(g + 1) * gw] = (u * sv).astype(o_ref.dtype)


def _branch_a(x, g1, w_za, ln_g, ln_b, w_s, b_s, tm=512, tn=1024):
    m, d = x.shape
    width = ln_g.shape[0]
    assert (width // A_GROUPS) <= tn and tn % (width // A_GROUPS) == 0
    kern = functools.partial(_branch_a_kernel, tm=tm, tn=tn, width=width)
    return pl.pallas_call(
        kern,
        out_shape=(jax.ShapeDtypeStruct((m, width), BF16), jax.ShapeDtypeStruct((m, d), BF16)),
        grid=(m // tm, 2 * width // tn),
        in_specs=[
            pl.BlockSpec((tm, d), lambda i, j: (i, 0)),
            pl.BlockSpec((1, d), lambda i, j: (0, 0)),
            pl.BlockSpec((tn, d), lambda i, j: (j, 0)),
            pl.BlockSpec((1, width), lambda i, j: (0, 0)),
            pl.BlockSpec((1, width), lambda i, j: (0, 0)),
            pl.BlockSpec((A_GROUPS, A_BLOCK, A_BLOCK), lambda i, j: (0, 0, 0)),
            pl.BlockSpec((A_BLOCK, A_GROUPS), lambda i, j: (0, 0)),
        ],
        out_specs=(pl.BlockSpec((tm, width), lambda i, j: (i, 0)), pl.BlockSpec((tm, d), lambda i, j: (i, 0))),
        scratch_shapes=[pltpu.VMEM((2 * width // tn, tm, tn), F32)],
        compiler_params=_params("parallel", "arbitrary"),
        name="branch_a",
    )(x, g1.reshape(1, d), w_za, ln_g.reshape(1, width), ln_b.reshape(1, width), w_s, b_s.T)


def _kv_kernel(xn_ref, wc_ref, g_ref, wuk_ref, wuvt_ref, k_ref, vt_ref):
    c = _nt_dot(xn_ref[...], wc_ref[...])
    cn = _rms(c, g_ref[...]).astype(BF16)
    k_ref[...] = _dot(cn, wuk_ref[...]).astype(k_ref.dtype)
    vt = _nt_dot(wuvt_ref[...], cn).astype(vt_ref.dtype)
    rows = HEAD_DIM + ONES_ROWS
    for h in range(vt.shape[0] // HEAD_DIM):
        vt_ref[h * rows:h * rows + HEAD_DIM, :] = vt[h * HEAD_DIM:(h + 1) * HEAD_DIM]
        vt_ref[h * rows + HEAD_DIM:(h + 1) * rows, :] = jnp.ones((ONES_ROWS, vt.shape[1]), vt_ref.dtype)


def _kv_proj(xn, w_c, col0, g, w_uk, w_uv_t, bsz, seq, tk):
    m, d = xn.shape
    lat, hd = w_uk.shape
    assert col0 % lat == 0
    j0 = col0 // lat
    nt = seq // tk
    vrows = hd // HEAD_DIM * (HEAD_DIM + ONES_ROWS)
    return pl.pallas_call(
        _kv_kernel,
        out_shape=(jax.ShapeDtypeStruct((m, hd), BF16), jax.ShapeDtypeStruct((bsz, nt, vrows, tk), BF16)),
        grid=(m // tk,),
        in_specs=[
            pl.BlockSpec((tk, d), lambda i: (i, 0)),
            pl.BlockSpec((lat, d), lambda i: (j0, 0)),
            pl.BlockSpec((1, lat), lambda i: (0, 0)),
            pl.BlockSpec((lat, hd), lambda i: (0, 0)),
            pl.BlockSpec((hd, lat), lambda i: (0, 0)),
        ],
        out_specs=(pl.BlockSpec((tk, hd), lambda i: (i, 0)),
                   pl.BlockSpec((None, None, vrows, tk), lambda i: (i // nt, i % nt, 0, 0))),
        compiler_params=_params("parallel"),
        name="kv_proj",
    )(xn, w_c, g.reshape(1, lat), w_uk, w_uv_t)


def _n_key_tiles(qt, tq, tk):
    return lax.div(qt * tq + tq + tk - 1, tk)


def _indexer_kernel(qi_ref, kpe_ref, kpo_ref, wt_ref, *refs, tq, tk, k_sel, n_cast):
    cast_in, o_ref, cast_out, sc_ref = refs[:n_cast], refs[n_cast], refs[n_cast + 1:2 * n_cast + 1], refs[-1]
    for w_ref, wb_ref in zip(cast_in, cast_out):
        wb_ref[...] = w_ref[...].astype(wb_ref.dtype)

    qt = pl.program_id(1)
    n_kt = _n_key_tiles(qt, tq, tk)
    nk = sc_ref.shape[0]
    neg_inf = -jnp.inf

    def count(pred):
        rows = 16
        def body(kt, cnt):
            hit = jnp.where(pred(sc_ref[kt], kt), 1.0, 0.0)
            return cnt + jnp.sum(hit.reshape(tk // rows, rows, tq), axis=0)
        cnt = lax.fori_loop(0, n_kt, body, jnp.zeros((rows, tq), F32))
        return jnp.sum(cnt, axis=0, keepdims=True)

    def key_pos(kt):
        return kt * tk + lax.broadcasted_iota(jnp.int32, (tk, tq), 0)

    wt = wt_ref[...] * (IDX_DIM ** -0.5 * IDX_HEADS ** -0.5)
    q_chunk = (qt * tq + lax.broadcasted_iota(jnp.int32, (1, tq), 1)) // CHUNK

    def score_tile(kt):
        k0 = pl.multiple_of(kt * tk, tk)
        ke = kpe_ref[pl.ds(k0, tk), :]
        ko = kpo_ref[pl.ds(k0, tk), :]
        acc = jnp.zeros((tk, tq), F32)
        for j in range(IDX_HEADS // 2):
            qp = qi_ref[:, j * 128:(j + 1) * 128]
            acc = acc + wt[2 * j:2 * j + 1, :] * jnp.maximum(_nt_dot(ke, qp), 0.0)
            acc = acc + wt[2 * j + 1:2 * j + 2, :] * jnp.maximum(_nt_dot(ko, qp), 0.0)
        k_chunk = (k0 + lax.broadcasted_iota(jnp.int32, (tk, 1), 0)) // CHUNK
        sc_ref[kt] = jnp.where(k_chunk <= q_chunk, acc, neg_inf)

    def score_pair(i, carry):
        score_tile(2 * i)
        score_tile(2 * i + 1)
        return carry
    lax.fori_loop(0, lax.div(n_kt, 2), score_pair, 0)

    @pl.when(lax.rem(n_kt, 2) == 1)
    def _():
        score_tile(n_kt - 1)

    def key_to_float(key):
        return lax.bitcast_convert_type(key ^ ((key >> 31) & 0x7FFFFFFF), F32)

    def bit_body(i, carry):
        tau, n_tau = carry
        cand = tau ^ jnp.left_shift(jnp.int32(1), 31 - i)
        cand_f = key_to_float(cand)
        n = count(lambda s, kt: s >= cand_f)
        accept = n >= k_sel
        return jnp.where(accept, cand, tau), jnp.where(accept, n, n_tau)
    every = jnp.broadcast_to((n_kt * tk).astype(F32), (1, tq))
    tau, n_tau = lax.fori_loop(0, 32, bit_body, (jnp.full((1, tq), INT_MIN, jnp.int32), every))
    key_neg_inf = INT_MIN + 0x7FFFFF
    tau_f = jnp.where(tau <= key_neg_inf, neg_inf, key_to_float(tau))
    has_ties = jnp.max(n_tau) > k_sel

    def write_mask(sel):
        def body(kt, carry):
            s = sc_ref[kt]
            keep = sel(s, kt) & (s > neg_inf)
            o_ref[kt] = jnp.where(keep, 0.0, NEG_BIG).astype(o_ref.dtype)
            return carry
        lax.fori_loop(0, n_kt, body, 0)

    @pl.when(jnp.logical_not(has_ties))
    def _():
        write_mask(lambda s, kt: s >= tau_f)

    @pl.when(has_ties)
    def _():
        need = k_sel - count(lambda s, kt: s > tau_f)

        def idx_body(i, cut):
            cand = cut + jnp.left_shift(jnp.int32(1), 12 - i)
            n = count(lambda s, kt: (s == tau_f) & (key_pos(kt) < cand))
            return jnp.where(n <= need, cand, cut)
        cut = lax.fori_loop(0, 13, idx_body, jnp.zeros((1, tq), jnp.int32))
        write_mask(lambda s, kt: (s > tau_f) | ((s == tau_f) & (key_pos(kt) < cut)))

    def fill_body(kt, carry):
        o_ref[kt] = jnp.full((tk, tq), NEG_BIG, o_ref.dtype)
        return carry
    lax.fori_loop(n_kt, nk, fill_body, 0)


def _indexer(q_idx, kpe, kpo, w_t, tq, tk, cast_weights):
    bsz, seq, _ = kpe.shape
    nq, nk = seq // tq, seq // tk
    k_sel = min(TOPK_MAX, seq // 4)
    assert tq >= k_sel and tq % CHUNK == 0 and tk % tq == 0
    qi_cols = q_idx.shape[1]
    steps = bsz * nq
    assert all(w.shape[0] % (steps * BF16_SUBLANES) == 0 for w in cast_weights)
    slab = lambda w: pl.BlockSpec((w.shape[0] // steps, w.shape[1]), lambda b, t: (b * nq + t, 0))
    n_cast = len(cast_weights)
    return pl.pallas_call(
        functools.partial(_indexer_kernel, tq=tq, tk=tk, k_sel=k_sel, n_cast=n_cast),
        out_shape=[jax.ShapeDtypeStruct((bsz, nq, nk, tk, tq), F32)]
        + [jax.ShapeDtypeStruct(w.shape, BF16) for w in cast_weights],
        grid=(bsz, nq),
        in_specs=[
            pl.BlockSpec((tq, qi_cols), lambda b, t: (b * nq + t, 0)),
            pl.BlockSpec((None, seq, 2 * IDX_DIM), lambda b, t: (b, 0, 0)),
            pl.BlockSpec((None, seq, 2 * IDX_DIM), lambda b, t: (b, 0, 0)),
            pl.BlockSpec((None, IDX_HEADS, tq), lambda b, t: (b, 0, t)),
        ] + [slab(w) for w in cast_weights],
        out_specs=[pl.BlockSpec((None, None, nk, tk, tq), lambda b, t: (b, t, 0, 0, 0))]
        + [slab(w) for w in cast_weights],
        scratch_shapes=[pltpu.VMEM((nk, tk, tq), F32)],
        compiler_params=_params("parallel", "parallel"),
        name="indexer",
    )(q_idx, kpe, kpo, w_t, *cast_weights)


def _attention_kernel(q_ref, k_ref, vt_ref, bias_ref, o_ref, m_ref, acc_ref, sa_ref, sb_ref, ma_ref, mb_ref,
                      *, tq, tk, group):
    qt = pl.program_id(2)
    n_kt = _n_key_tiles(qt, tq, tk)
    rows = HEAD_DIM + ONES_ROWS
    m_ref[...] = jnp.full(m_ref.shape, NEG_BIG, F32)
    acc_ref[...] = jnp.zeros(acc_ref.shape, F32)

    def scores(kt, s_ref, mx_ref):
        k0 = pl.multiple_of(kt * tk, tk)
        for g in range(group):
            cols = slice(g * HEAD_DIM, (g + 1) * HEAD_DIM)
            s = _nt_dot(k_ref[pl.ds(k0, tk), cols], q_ref[:, cols]) + bias_ref[kt]
            s_ref[g] = s
            mx_ref[g] = jnp.max(s, axis=0, keepdims=True)

    def consume(kt, s_ref, mx_ref):
        for g in range(group):
            m_old = m_ref[g]
            m_new = jnp.maximum(m_old, mx_ref[g])
            m_ref[g] = m_new
            p = jnp.exp2(s_ref[g] - m_new).astype(BF16)
            pv = _dot(vt_ref[kt, g * rows:(g + 1) * rows, :], p)
            acc_ref[g] = jnp.exp2(m_old - m_new) * acc_ref[g] + pv

    scores(0, sa_ref, ma_ref)
    n_pairs = lax.div(n_kt - 1, 2)

    def pair_body(i, carry):
        kt = 2 * i
        scores(kt + 1, sb_ref, mb_ref)
        consume(kt, sa_ref, ma_ref)
        scores(kt + 2, sa_ref, ma_ref)
        consume(kt + 1, sb_ref, mb_ref)
        return carry
    lax.fori_loop(0, n_pairs, pair_body, 0)
    kt = 2 * n_pairs

    @pl.when(n_kt - kt == 1)
    def _():
        consume(kt, sa_ref, ma_ref)

    @pl.when(n_kt - kt == 2)
    def _():
        scores(kt + 1, sb_ref, mb_ref)
        consume(kt, sa_ref, ma_ref)
        consume(kt + 1, sb_ref, mb_ref)

    for g in range(group):
        acc = acc_ref[g]
        out = acc[:HEAD_DIM] / acc[HEAD_DIM:HEAD_DIM + 1]
        o_ref[:, g * HEAD_DIM:(g + 1) * HEAD_DIM] = out.T.astype(o_ref.dtype)


def _attention(q, k, vt, bias, tq, tk, group=8):
    bsz, nk, vrows, _ = vt.shape
    rows = HEAD_DIM + ONES_ROWS
    hd = vrows // rows * HEAD_DIM
    seq = nk * tk
    nq = seq // tq
    gw = group * HEAD_DIM
    return pl.pallas_call(
        functools.partial(_attention_kernel, tq=tq, tk=tk, group=group),
        out_shape=jax.ShapeDtypeStruct((bsz * seq, hd), BF16),
        grid=(bsz, hd // gw, nq),
        in_specs=[
            pl.BlockSpec((tq, gw), lambda b, g, t: (b * nq + t, g)),
            pl.BlockSpec((None, seq, gw), lambda b, g, t: (b, 0, g)),
            pl.BlockSpec((None, nk, group * rows, tk), lambda b, g, t: (b, 0, g, 0)),
            pl.BlockSpec((None, None, nk, tk, tq), lambda b, g, t: (b, t, 0, 0, 0)),
        ],
        out_specs=pl.BlockSpec((tq, gw), lambda b, g, t: (b * nq + t, g)),
        scratch_shapes=[pltpu.VMEM((group, 1, tq), F32), pltpu.VMEM((group, rows, tq), F32),
                        pltpu.VMEM((group, tk, tq), F32), pltpu.VMEM((group, tk, tq), F32),
                        pltpu.VMEM((group, 1, tq), F32), pltpu.VMEM((group, 1, tq), F32)],
        compiler_params=_params("parallel", "parallel", "arbitrary"),
        name="attention",
    )(q, k.reshape(bsz, seq, hd), vt, bias)


def _merge_kernel(xn_ref, ya_ref, yb_ref, wga_ref, wgb_ref, woa_ref, wob_ref, o_ref):
    xn = xn_ref[...]
    ga = jax.nn.sigmoid(_nt_dot(xn, wga_ref[...]))
    gb = jax.nn.sigmoid(_nt_dot(xn, wgb_ref[...]))
    o_ref[...] = (ga * _dot(ya_ref[...], woa_ref[...]) + gb * _dot(yb_ref[...], wob_ref[...])).astype(o_ref.dtype)


def _merge(xn, ya, yb, w_t, row_ga, row_gb, w_oa, w_ob, tm=1024, tn=512):
    m, d = xn.shape
    row = lambda k: pl.BlockSpec((tm, k), lambda i, j: (i, 0))
    colw = lambda k: pl.BlockSpec((k, tn), lambda i, j: (0, j))
    assert row_ga % BF16_SUBLANES == 0 and row_gb % BF16_SUBLANES == 0
    gate = lambda r0: pl.BlockSpec((pl.Element(tn), pl.Element(d)),
                                   lambda i, j: (pl.multiple_of(r0 + j * tn, BF16_SUBLANES), 0))
    return pl.pallas_call(
        _merge_kernel,
        out_shape=jax.ShapeDtypeStruct((m, d), BF16),
        grid=(m // tm, d // tn),
        in_specs=[row(d), row(ya.shape[1]), row(yb.shape[1]),
                  gate(row_ga), gate(row_gb), colw(ya.shape[1]), colw(yb.shape[1])],
        out_specs=pl.BlockSpec((tm, tn), lambda i, j: (i, j)),
        compiler_params=_params("parallel", "arbitrary"),
        name="merge",
    )(xn, ya, yb, w_t, w_t, w_oa, w_ob)


def _out_kernel(x_ref, mix_ref, w_ref, g_ref, h_ref, hn_ref):
    h = x_ref[...] + _dot(mix_ref[...], w_ref[...])
    h_ref[...] = h
    hn_ref[...] = _rms(h, g_ref[...]).astype(hn_ref.dtype)


def _out_proj(x, mixed, w_out, g, tm=512):
    m, d = x.shape
    row = pl.BlockSpec((tm, d), lambda i: (i, 0))
    return pl.pallas_call(
        _out_kernel,
        out_shape=(jax.ShapeDtypeStruct((m, d), F32), jax.ShapeDtypeStruct((m, d), BF16)),
        grid=(m // tm,),
        in_specs=[row, row, pl.BlockSpec((d, d), lambda i: (0, 0)), pl.BlockSpec((1, d), lambda i: (0, 0))],
        out_specs=(row, row),
        compiler_params=_params("parallel"),
        name="out_proj",
    )(x, mixed, w_out, g.reshape(1, d))


def _ffn_kernel(hn_ref, h_ref, wg_ref, wu_ref, wd_ref, g_ref, o_ref, acc_ref, *, final_norm):
    j = pl.program_id(1)

    @pl.when(j == 0)
    def _():
        acc_ref[...] = jnp.zeros(acc_ref.shape, F32)

    hn = hn_ref[...]
    t = (jax.nn.silu(_dot(hn, wg_ref[...])) * _dot(hn, wu_ref[...])).astype(BF16)
    acc_ref[...] += _dot(t, wd_ref[...])

    @pl.when(j == pl.num_programs(1) - 1)
    def _():
        h2 = h_ref[...] + acc_ref[...]
        o_ref[...] = _rms(h2, g_ref[...]) if final_norm else h2


def _ffn(hn, h, w_gate, w_up, w_down, g, final_norm, tm=512, tf=512):
    m, d = hn.shape
    f = w_gate.shape[1]
    row = pl.BlockSpec((tm, d), lambda i, j: (i, 0))
    return pl.pallas_call(
        functools.partial(_ffn_kernel, final_norm=final_norm),
        out_shape=jax.ShapeDtypeStruct((m, d), F32),
        grid=(m // tm, f // tf),
        in_specs=[row, row,
                  pl.BlockSpec((d, tf), lambda i, j: (0, j)),
                  pl.BlockSpec((d, tf), lambda i, j: (0, j)),
                  pl.BlockSpec((tf, d), lambda i, j: (j, 0)),
                  pl.BlockSpec((1, d), lambda i, j: (0, 0))],
        out_specs=row,
        scratch_shapes=[pltpu.VMEM((tm, d), F32)],
        compiler_params=_params("parallel", "arbitrary"),
        name="ffn",
    )(hn, h, w_gate, w_up, w_down, g.reshape(1, d))


def kernel(x, norm1_g, w_in, a_ln_g, a_ln_b, a_w_s, a_b_s, kv_norm_g, w_uk, w_uv, w_oa, w_ob, w_out, norm2_g,
           w_ff_gate, w_ff_up, w_ff_down, final_g):
    bsz, seq, d = x.shape
    m = bsz * seq
    depth = norm1_g.shape[0]
    a_width = a_ln_g.shape[1]
    lat = kv_norm_g.shape[1]
    heads, head_dim = w_uk.shape[2], w_uk.shape[3]
    assert head_dim == HEAD_DIM
    hd = heads * head_dim
    qi = IDX_HEADS * IDX_DIM
    c_q = 2 * a_width
    c_kv = c_q + hd
    c_qi = c_kv + lat
    c_ki = c_qi + qi
    c_wi = c_ki + IDX_DIM
    c_g = c_wi + IDX_HEADS

    h = x.reshape(m, d)
    for l in range(depth):
        w = w_in[l].T.astype(BF16)

        y_a, xn = _branch_a(h, norm1_g[l], w, a_ln_g[l], a_ln_b[l], a_w_s[l], a_b_s[l])
        q, q_idx, kpe, kpo, w_t = _q_projections(xn, w, c_q, hd, c_qi, qi, c_ki, HEAD_DIM ** -0.5 * LOG2E, bsz)
        k_tok, v_t = _kv_proj(xn, w, c_kv, kv_norm_g[l], w_uk[l].reshape(lat, hd).astype(BF16),
                              w_uv[l].reshape(lat, hd).T.astype(BF16), bsz, seq, ATT_TK)
        bias, w_oa_b, w_ob_b, w_out_b, w_fg_b, w_fu_b, w_fd_b = _indexer(
            q_idx, kpe, kpo, w_t, ATT_TQ, ATT_TK,
            [w_oa[l], w_ob[l], w_out[l], w_ff_gate[l], w_ff_up[l], w_ff_down[l]])
        y_b = _attention(q, k_tok, v_t, bias, ATT_TQ, ATT_TK)
        mixed = _merge(xn, y_a, y_b, w, c_g, c_g + d, w_oa_b, w_ob_b)
        h, hn = _out_proj(h, mixed, w_out_b, norm2_g[l])
        h = _ffn(hn, h, w_fg_b, w_fu_b, w_fd_b, final_g, final_norm=(l == depth - 1))
    return h.reshape(bsz, seq, d)
```

```python
import functools
import math

import jax
import jax.numpy as jnp
from jax import lax
from jax.experimental import pallas as pl
from jax.experimental.pallas import tpu as pltpu

EPS = 1e-6
CHUNK = 64
A_GROUPS = 8
A_BLOCK = 128
LANES = 128
BF16_SUBLANES = 16
HEAD_DIM = 128
IDX_DIM = 64
IDX_HEADS = 16
TOPK_MAX = 256
ONES_ROWS = 16
LOG2E = 1.4426950408889634

VMEM_LIMIT_BYTES = 56 * 1024 * 1024
INT_MIN = -2147483648
ATT_TQ = 256
ATT_TK = 512
NEG_BIG = -1e30

F32 = jnp.float32
BF16 = jnp.bfloat16


def _params(*sem, flags=None):
    return pltpu.CompilerParams(dimension_semantics=sem, vmem_limit_bytes=VMEM_LIMIT_BYTES, flags=flags)


def _nt_dot(a, b):
    return lax.dot_general(a, b, (((1,), (1,)), ((), ())), preferred_element_type=F32)


def _dot(a, b):
    return jnp.dot(a, b, preferred_element_type=F32)


def _rms(x, g):
    return x * lax.rsqrt(jnp.mean(x * x, axis=-1, keepdims=True) + EPS) * g


def _row_specs(cols, row0, n):
    bh = math.gcd(row0, n) if row0 else n
    assert bh % LANES == 0
    return [pl.BlockSpec((bh, cols), functools.partial(lambda i, ib: (ib, 0), ib=row0 // bh + c)) for c in range(n // bh)]


def _qproj_kernel(xn_ref, *refs, n_q, n_qi, q_scale):
    wq, wqi, wkw = refs[:n_q], refs[n_q:n_q + n_qi], refs[n_q + n_qi]
    q_ref, qi_ref, kpe_ref, kpo_ref, wt_ref = refs[n_q + n_qi + 1:]
    xn = xn_ref[...]
    for c, w_ref in enumerate(wq):
        bw = w_ref.shape[0]
        q_ref[:, c * bw:(c + 1) * bw] = (_nt_dot(xn, w_ref[...]) * q_scale).astype(q_ref.dtype)
    for c, w_ref in enumerate(wqi):
        bw = w_ref.shape[0]
        qi_ref[:, c * bw:(c + 1) * bw] = _nt_dot(xn, w_ref[...]).astype(qi_ref.dtype)
    kw = _nt_dot(xn, wkw[...])
    lane = lax.broadcasted_iota(jnp.int32, kw.shape, 1)
    kpe_ref[...] = jnp.where(lane < IDX_DIM, kw, 0.0).astype(kpe_ref.dtype)
    kpo_ref[...] = jnp.where(lane >= LANES - IDX_DIM, pltpu.roll(kw, LANES - IDX_DIM, 1), 0.0).astype(kpo_ref.dtype)
    wt_ref[...] = kw.T[IDX_DIM:IDX_DIM + IDX_HEADS, :]


def _q_projections(xn, w, c_q, n_q, c_qi, n_qi, c_kw, q_scale, bsz, tm=512):
    m, d = xn.shape
    seq = m // bsz
    nt = seq // tm
    assert 2 * IDX_DIM == LANES
    pad_spec = pl.BlockSpec((None, tm, LANES), lambda i: (i // nt, i % nt, 0))
    q_specs, qi_specs, kw_specs = _row_specs(d, c_q, n_q), _row_specs(d, c_qi, n_qi), _row_specs(d, c_kw, LANES)
    row = lambda n: pl.BlockSpec((tm, n), lambda i: (i, 0))
    return pl.pallas_call(
        functools.partial(_qproj_kernel, n_q=len(q_specs), n_qi=len(qi_specs), q_scale=q_scale),
        out_shape=(jax.ShapeDtypeStruct((m, n_q), BF16), jax.ShapeDtypeStruct((m, n_qi), BF16),
                   jax.ShapeDtypeStruct((bsz, seq, LANES), BF16), jax.ShapeDtypeStruct((bsz, seq, LANES), BF16),
                   jax.ShapeDtypeStruct((bsz, IDX_HEADS, seq), F32)),
        grid=(m // tm,),
        in_specs=[row(d)] + q_specs + qi_specs + kw_specs,
        out_specs=(row(n_q), row(n_qi), pad_spec, pad_spec,
                   pl.BlockSpec((None, IDX_HEADS, tm), lambda i: (i // nt, 0, i % nt))),
        compiler_params=_params("parallel"),
        name="q_proj",
    )(xn, *([w] * (len(q_specs) + len(qi_specs) + 1)))


def _branch_a_kernel(x_ref, g1_ref, w_ref, lng_ref, lnb_ref, ws_ref, bst_ref, o_ref, xn_ref, z_ref, *, tm, tn, width):
    j = pl.program_id(1)

    @pl.when(j == 0)
    def _():
        xn_ref[...] = _rms(x_ref[...], g1_ref[...]).astype(xn_ref.dtype)

    z_ref[j] = jax.nn.gelu(_nt_dot(xn_ref[...], w_ref[...]))

    @pl.when(j == pl.num_programs(1) - 1)
    def _():
        per_half = width // tn
        gw = width // A_GROUPS
        row = lax.broadcasted_iota(jnp.int32, (A_BLOCK, A_BLOCK), 0)
        col = lax.broadcasted_iota(jnp.int32, (A_BLOCK, A_BLOCK), 1)
        causal = (col // CHUNK) <= (row // CHUNK)
        for r in range(tm // A_BLOCK):
            rows = slice(r * A_BLOCK, (r + 1) * A_BLOCK)
            v = jnp.concatenate([z_ref[per_half + c, rows, :] for c in range(per_half)], axis=-1)
            vc = v - jnp.mean(v, axis=-1, keepdims=True)
            vn = vc * lax.rsqrt(jnp.mean(vc * vc, axis=-1, keepdims=True) + EPS)
            vn = (vn * lng_ref[...] + lnb_ref[...]).astype(BF16)
            for g in range(A_GROUPS):
                wm = jnp.where(causal, ws_ref[g], 0.0).astype(BF16)
                sv = _dot(wm, vn[:, g * gw:(g + 1) * gw]) + bst_ref[:, g:g + 1]
                c, off = divmod(g * gw, tn)
                u = z_ref[c, rows, off:off + gw]
                o_ref[rows, g * gw:(g + 1) * gw] = (u * sv).astype(o_ref.dtype)


def _branch_a(x, g1, w_za, ln_g, ln_b, w_s, b_s, tm=512, tn=2048):
    m, d = x.shape
    width = ln_g.shape[0]
    assert (width // A_GROUPS) <= tn and tn % (width // A_GROUPS) == 0
    kern = functools.partial(_branch_a_kernel, tm=tm, tn=tn, width=width)
    return pl.pallas_call(
        kern,
        out_shape=(jax.ShapeDtypeStruct((m, width), BF16), jax.ShapeDtypeStruct((m, d), BF16)),
        grid=(m // tm, 2 * width // tn),
        in_specs=[
            pl.BlockSpec((tm, d), lambda i, j: (i, 0)),
            pl.BlockSpec((1, d), lambda i, j: (0, 0)),
            pl.BlockSpec((tn, d), lambda i, j: (j, 0)),
            pl.BlockSpec((1, width), lambda i, j: (0, 0)),
            pl.BlockSpec((1, width), lambda i, j: (0, 0)),
            pl.BlockSpec((A_GROUPS, A_BLOCK, A_BLOCK), lambda i, j: (0, 0, 0)),
            pl.BlockSpec((A_BLOCK, A_GROUPS), lambda i, j: (0, 0)),
        ],
        out_specs=(pl.BlockSpec((tm, width), lambda i, j: (i, 0)), pl.BlockSpec((tm, d), lambda i, j: (i, 0))),
        scratch_shapes=[pltpu.VMEM((2 * width // tn, tm, tn), F32)],
        compiler_params=_params("parallel", "arbitrary"),
        name="branch_a",
    )(x, g1.reshape(1, d), w_za, ln_g.reshape(1, width), ln_b.reshape(1, width), w_s, b_s.T)


def _kv_kernel(xn_ref, wc_ref, g_ref, wuk_ref, wuvt_ref, k_ref, vt_ref):
    c = _nt_dot(xn_ref[...], wc_ref[...])
    cn = _rms(c, g_ref[...]).astype(BF16)
    k_ref[...] = _dot(cn, wuk_ref[...]).astype(k_ref.dtype)
    vt = _nt_dot(wuvt_ref[...], cn).astype(vt_ref.dtype)
    rows = HEAD_DIM + ONES_ROWS
    for h in range(vt.shape[0] // HEAD_DIM):
        vt_ref[h * rows:h * rows + HEAD_DIM, :] = vt[h * HEAD_DIM:(h + 1) * HEAD_DIM]
        vt_ref[h * rows + HEAD_DIM:(h + 1) * rows, :] = jnp.ones((ONES_ROWS, vt.shape[1]), vt_ref.dtype)


def _kv_proj(xn, w_c, col0, g, w_uk, w_uv_t, bsz, seq, tk):
    m, d = xn.shape
    lat, hd = w_uk.shape
    assert col0 % lat == 0
    j0 = col0 // lat
    nt = seq // tk
    vrows = hd // HEAD_DIM * (HEAD_DIM + ONES_ROWS)
    return pl.pallas_call(
        _kv_kernel,
        out_shape=(jax.ShapeDtypeStruct((m, hd), BF16), jax.ShapeDtypeStruct((bsz, nt, vrows, tk), BF16)),
        grid=(m // tk,),
        in_specs=[
            pl.BlockSpec((tk, d), lambda i: (i, 0)),
            pl.BlockSpec((lat, d), lambda i: (j0, 0)),
            pl.BlockSpec((1, lat), lambda i: (0, 0)),
            pl.BlockSpec((lat, hd), lambda i: (0, 0)),
            pl.BlockSpec((hd, lat), lambda i: (0, 0)),
        ],
        out_specs=(pl.BlockSpec((tk, hd), lambda i: (i, 0)),
                   pl.BlockSpec((None, None, vrows, tk), lambda i: (i // nt, i % nt, 0, 0))),
        compiler_params=_params("parallel"),
        name="kv_proj",
    )(xn, w_c, g.reshape(1, lat), w_uk, w_uv_t)


def _n_key_tiles(qt, tq, tk):
    return lax.div(qt * tq + tq + tk - 1, tk)


def _indexer_kernel(qi_ref, kpe_ref, kpo_ref, wt_ref, *refs, tq, tk, k_sel, n_cast):
    cast_in, o_ref, cast_out, sc_ref = refs[:n_cast], refs[n_cast], refs[n_cast + 1:2 * n_cast + 1], refs[-1]
    for w_ref, wb_ref in zip(cast_in, cast_out):
        wb_ref[...] = w_ref[...].astype(wb_ref.dtype)

    qt = pl.program_id(1)
    n_kt = _n_key_tiles(qt, tq, tk)
    nk = sc_ref.shape[0]
    neg_inf = -jnp.inf

    def count(pred):
        rows = 16
        def body(kt, cnt):
            hit = jnp.where(pred(sc_ref[kt], kt), 1.0, 0.0)
            return cnt + jnp.sum(hit.reshape(tk // rows, rows, tq), axis=0)
        cnt = lax.fori_loop(0, n_kt, body, jnp.zeros((rows, tq), F32))
        return jnp.sum(cnt, axis=0, keepdims=True)

    def key_pos(kt):
        return kt * tk + lax.broadcasted_iota(jnp.int32, (tk, tq), 0)

    wt = wt_ref[...] * (IDX_DIM ** -0.5 * IDX_HEADS ** -0.5)
    q_chunk = (qt * tq + lax.broadcasted_iota(jnp.int32, (1, tq), 1)) // CHUNK

    def score_tile(kt):
        k0 = pl.multiple_of(kt * tk, tk)
        ke = kpe_ref[pl.ds(k0, tk), :]
        ko = kpo_ref[pl.ds(k0, tk), :]
        acc = jnp.zeros((tk, tq), F32)
        for j in range(IDX_HEADS // 2):
            qp = qi_ref[:, j * 128:(j + 1) * 128]
            acc = acc + wt[2 * j:2 * j + 1, :] * jnp.maximum(_nt_dot(ke, qp), 0.0)
            acc = acc + wt[2 * j + 1:2 * j + 2, :] * jnp.maximum(_nt_dot(ko, qp), 0.0)
        k_chunk = (k0 + lax.broadcasted_iota(jnp.int32, (tk, 1), 0)) // CHUNK
        sc_ref[kt] = jnp.where(k_chunk <= q_chunk, acc, neg_inf)

    def score_pair(i, carry):
        score_tile(2 * i)
        score_tile(2 * i + 1)
        return carry
    lax.fori_loop(0, lax.div(n_kt, 2), score_pair, 0)

    @pl.when(lax.rem(n_kt, 2) == 1)
    def _():
        score_tile(n_kt - 1)

    def key_to_float(key):
        return lax.bitcast_convert_type(key ^ ((key >> 31) & 0x7FFFFFFF), F32)

    def bit_body(i, carry):
        tau, n_tau = carry
        cand = tau ^ jnp.left_shift(jnp.int32(1), 31 - i)
        cand_f = key_to_float(cand)
        n = count(lambda s, kt: s >= cand_f)
        accept = n >= k_sel
        return jnp.where(accept, cand, tau), jnp.where(accept, n, n_tau)
    every = jnp.broadcast_to((n_kt * tk).astype(F32), (1, tq))
    tau, n_tau = lax.fori_loop(0, 32, bit_body, (jnp.full((1, tq), INT_MIN, jnp.int32), every))
    key_neg_inf = INT_MIN + 0x7FFFFF
    tau_f = jnp.where(tau <= key_neg_inf, neg_inf, key_to_float(tau))
    has_ties = jnp.max(n_tau) > k_sel

    def write_mask(sel):
        def body(kt, carry):
            s = sc_ref[kt]
            keep = sel(s, kt) & (s > neg_inf)
            o_ref[kt] = jnp.where(keep, 0.0, NEG_BIG).astype(o_ref.dtype)
            return carry
        lax.fori_loop(0, n_kt, body, 0)

    @pl.when(jnp.logical_not(has_ties))
    def _():
        write_mask(lambda s, kt: s >= tau_f)

    @pl.when(has_ties)
    def _():
        need = k_sel - count(lambda s, kt: s > tau_f)

        def idx_body(i, cut):
            cand = cut + jnp.left_shift(jnp.int32(1), 12 - i)
            n = count(lambda s, kt: (s == tau_f) & (key_pos(kt) < cand))
            return jnp.where(n <= need, cand, cut)
        cut = lax.fori_loop(0, 13, idx_body, jnp.zeros((1, tq), jnp.int32))
        write_mask(lambda s, kt: (s > tau_f) | ((s == tau_f) & (key_pos(kt) < cut)))

    def fill_body(kt, carry):
        o_ref[kt] = jnp.full((tk, tq), NEG_BIG, o_ref.dtype)
        return carry
    lax.fori_loop(n_kt, nk, fill_body, 0)


def _indexer(q_idx, kpe, kpo, w_t, tq, tk, cast_weights):
    bsz, seq, _ = kpe.shape
    nq, nk = seq // tq, seq // tk
    k_sel = min(TOPK_MAX, seq // 4)
    assert tq >= k_sel and tq % CHUNK == 0 and tk % tq == 0
    qi_cols = q_idx.shape[1]
    steps = bsz * nq
    assert all(w.shape[0] % (steps * BF16_SUBLANES) == 0 for w in cast_weights)
    slab = lambda w: pl.BlockSpec((w.shape[0] // steps, w.shape[1]), lambda b, t: (b * nq + t, 0))
    n_cast = len(cast_weights)
    return pl.pallas_call(
        functools.partial(_indexer_kernel, tq=tq, tk=tk, k_sel=k_sel, n_cast=n_cast),
        out_shape=[jax.ShapeDtypeStruct((bsz, nq, nk, tk, tq), F32)]
        + [jax.ShapeDtypeStruct(w.shape, BF16) for w in cast_weights],
        grid=(bsz, nq),
        in_specs=[
            pl.BlockSpec((tq, qi_cols), lambda b, t: (b * nq + t, 0)),
            pl.BlockSpec((None, seq, 2 * IDX_DIM), lambda b, t: (b, 0, 0)),
            pl.BlockSpec((None, seq, 2 * IDX_DIM), lambda b, t: (b, 0, 0)),
            pl.BlockSpec((None, IDX_HEADS, tq), lambda b, t: (b, 0, t)),
        ] + [slab(w) for w in cast_weights],
        out_specs=[pl.BlockSpec((None, None, nk, tk, tq), lambda b, t: (b, t, 0, 0, 0))]
        + [slab(w) for w in cast_weights],
        scratch_shapes=[pltpu.VMEM((nk, tk, tq), F32)],
        compiler_params=_params("parallel", "parallel"),
        name="indexer",
    )(q_idx, kpe, kpo, w_t, *cast_weights)


def _attention_kernel(q_ref, k_ref, vt_ref, bias_ref, o_ref, m_ref, acc_ref, sa_ref, sb_ref, ma_ref, mb_ref,
                      *, tq, tk, group):
    qt = pl.program_id(2)
    n_kt = _n_key_tiles(qt, tq, tk)
    rows = HEAD_DIM + ONES_ROWS
    m_ref[...] = jnp.full(m_ref.shape, NEG_BIG, F32)
    acc_ref[...] = jnp.zeros(acc_ref.shape, F32)

    def scores(kt, s_ref, mx_ref):
        k0 = pl.multiple_of(kt * tk, tk)
        for g in range(group):
            cols = slice(g * HEAD_DIM, (g + 1) * HEAD_DIM)
            s = _nt_dot(k_ref[pl.ds(k0, tk), cols], q_ref[:, cols]) + bias_ref[kt]
            s_ref[g] = s
            mx_ref[g] = jnp.max(s, axis=0, keepdims=True)

    def consume(kt, s_ref, mx_ref):
        for g in range(group):
            m_old = m_ref[g]
            m_new = jnp.maximum(m_old, mx_ref[g])
            m_ref[g] = m_new
            p = jnp.exp2(s_ref[g] - m_new).astype(BF16)
            pv = _dot(vt_ref[kt, g * rows:(g + 1) * rows, :], p)
            acc_ref[g] = jnp.exp2(m_old - m_new) * acc_ref[g] + pv

    scores(0, sa_ref, ma_ref)
    n_pairs = lax.div(n_kt - 1, 2)

    def pair_body(i, carry):
        kt = 2 * i
        scores(kt + 1, sb_ref, mb_ref)
        consume(kt, sa_ref, ma_ref)
        scores(kt + 2, sa_ref, ma_ref)
        consume(kt + 1, sb_ref, mb_ref)
        return carry
    lax.fori_loop(0, n_pairs, pair_body, 0)
    kt = 2 * n_pairs

    @pl.when(n_kt - kt == 1)
    def _():
        consume(kt, sa_ref, ma_ref)

    @pl.when(n_kt - kt == 2)
    def _():
        scores(kt + 1, sb_ref, mb_ref)
        consume(kt, sa_ref, ma_ref)
        consume(kt + 1, sb_ref, mb_ref)

    for g in range(group):
        acc = acc_ref[g]
        out = acc[:HEAD_DIM] / acc[HEAD_DIM:HEAD_DIM + 1]
        o_ref[:, g * HEAD_DIM:(g + 1) * HEAD_DIM] = out.T.astype(o_ref.dtype)


def _attention(q, k, vt, bias, tq, tk, group=8):
    bsz, nk, vrows, _ = vt.shape
    rows = HEAD_DIM + ONES_ROWS
    hd = vrows // rows * HEAD_DIM
    seq = nk * tk
    nq = seq // tq
    gw = group * HEAD_DIM
    return pl.pallas_call(
        functools.partial(_attention_kernel, tq=tq, tk=tk, group=group),
        out_shape=jax.ShapeDtypeStruct((bsz * seq, hd), BF16),
        grid=(bsz, hd // gw, nq),
        in_specs=[
            pl.BlockSpec((tq, gw), lambda b, g, t: (b * nq + t, g)),
            pl.BlockSpec((None, seq, gw), lambda b, g, t: (b, 0, g)),
            pl.BlockSpec((None, nk, group * rows, tk), lambda b, g, t: (b, 0, g, 0)),
            pl.BlockSpec((None, None, nk, tk, tq), lambda b, g, t: (b, t, 0, 0, 0)),
        ],
        out_specs=pl.BlockSpec((tq, gw), lambda b, g, t: (b * nq + t, g)),
        scratch_shapes=[pltpu.VMEM((group, 1, tq), F32), pltpu.VMEM((group, rows, tq), F32),
                        pltpu.VMEM((group, tk, tq), F32), pltpu.VMEM((group, tk, tq), F32),
                        pltpu.VMEM((group, 1, tq), F32), pltpu.VMEM((group, 1, tq), F32)],
        compiler_params=_params("parallel", "parallel", "arbitrary"),
        name="attention",
    )(q, k.reshape(bsz, seq, hd), vt, bias)


def _merge_kernel(xn_ref, ya_ref, yb_ref, wga_ref, wgb_ref, woa_ref, wob_ref, o_ref):
    xn = xn_ref[...]
    ga = jax.nn.sigmoid(_nt_dot(xn, wga_ref[...]))
    gb = jax.nn.sigmoid(_nt_dot(xn, wgb_ref[...]))
    o_ref[...] = (ga * _dot(ya_ref[...], woa_ref[...]) + gb * _dot(yb_ref[...], wob_ref[...])).astype(o_ref.dtype)


def _merge(xn, ya, yb, w_t, row_ga, row_gb, w_oa, w_ob, tm=1024, tn=512):
    m, d = xn.shape
    row = lambda k: pl.BlockSpec((tm, k), lambda i, j: (i, 0))
    colw = lambda k: pl.BlockSpec((k, tn), lambda i, j: (0, j))
    assert row_ga % BF16_SUBLANES == 0 and row_gb % BF16_SUBLANES == 0
    gate = lambda r0: pl.BlockSpec((pl.Element(tn), pl.Element(d)),
                                   lambda i, j: (pl.multiple_of(r0 + j * tn, BF16_SUBLANES), 0))
    return pl.pallas_call(
        _merge_kernel,
        out_shape=jax.ShapeDtypeStruct((m, d), BF16),
        grid=(m // tm, d // tn),
        in_specs=[row(d), row(ya.shape[1]), row(yb.shape[1]),
                  gate(row_ga), gate(row_gb), colw(ya.shape[1]), colw(yb.shape[1])],
        out_specs=pl.BlockSpec((tm, tn), lambda i, j: (i, j)),
        compiler_params=_params("parallel", "arbitrary"),
        name="merge",
    )(xn, ya, yb, w_t, w_t, w_oa, w_ob)


def _out_kernel(x_ref, mix_ref, w_ref, g_ref, h_ref, hn_ref):
    h = x_ref[...] + _dot(mix_ref[...], w_ref[...])
    h_ref[...] = h
    hn_ref[...] = _rms(h, g_ref[...]).astype(hn_ref.dtype)


def _out_proj(x, mixed, w_out, g, tm=512):
    m, d = x.shape
    row = pl.BlockSpec((tm, d), lambda i: (i, 0))
    return pl.pallas_call(
        _out_kernel,
        out_shape=(jax.ShapeDtypeStruct((m, d), F32), jax.ShapeDtypeStruct((m, d), BF16)),
        grid=(m // tm,),
        in_specs=[row, row, pl.BlockSpec((d, d), lambda i: (0, 0)), pl.BlockSpec((1, d), lambda i: (0, 0))],
        out_specs=(row, row),
        compiler_params=_params("parallel"),
        name="out_proj",
    )(x, mixed, w_out, g.reshape(1, d))


def _ffn_kernel(hn_ref, h_ref, wg_ref, wu_ref, wd_ref, g_ref, o_ref, acc_ref, *, final_norm):
    j = pl.program_id(1)

    @pl.when(j == 0)
    def _():
        acc_ref[...] = jnp.zeros(acc_ref.shape, F32)

    hn = hn_ref[...]
    t = (jax.nn.silu(_dot(hn, wg_ref[...])) * _dot(hn, wu_ref[...])).astype(BF16)
    acc_ref[...] += _dot(t, wd_ref[...])

    @pl.when(j == pl.num_programs(1) - 1)
    def _():
        h2 = h_ref[...] + acc_ref[...]
        o_ref[...] = _rms(h2, g_ref[...]) if final_norm else h2


def _ffn(hn, h, w_gate, w_up, w_down, g, final_norm, tm=512, tf=512):
    m, d = hn.shape
    f = w_gate.shape[1]
    row = pl.BlockSpec((tm, d), lambda i, j: (i, 0))
    return pl.pallas_call(
        functools.partial(_ffn_kernel, final_norm=final_norm),
        out_shape=jax.ShapeDtypeStruct((m, d), F32),
        grid=(m // tm, f // tf),
        in_specs=[row, row,
                  pl.BlockSpec((d, tf), lambda i, j: (0, j)),
                  pl.BlockSpec((d, tf), lambda i, j: (0, j)),
                  pl.BlockSpec((tf, d), lambda i, j: (j, 0)),
                  pl.BlockSpec((1, d), lambda i, j: (0, 0))],
        out_specs=row,
        scratch_shapes=[pltpu.VMEM((tm, d), F32)],
        compiler_params=_params("parallel", "arbitrary"),
        name="ffn",
    )(hn, h, w_gate, w_up, w_down, g.reshape(1, d))


def kernel(x, norm1_g, w_in, a_ln_g, a_ln_b, a_w_s, a_b_s, kv_norm_g, w_uk, w_uv, w_oa, w_ob, w_out, norm2_g,
           w_ff_gate, w_ff_up, w_ff_down, final_g):
    bsz, seq, d = x.shape
    m = bsz * seq
    depth = norm1_g.shape[0]
    a_width = a_ln_g.shape[1]
    lat = kv_norm_g.shape[1]
    heads, head_dim = w_uk.shape[2], w_uk.shape[3]
    assert head_dim == HEAD_DIM
    hd = heads * head_dim
    qi = IDX_HEADS * IDX_DIM
    c_q = 2 * a_width
    c_kv = c_q + hd
    c_qi = c_kv + lat
    c_ki = c_qi + qi
    c_wi = c_ki + IDX_DIM
    c_g = c_wi + IDX_HEADS

    h = x.reshape(m, d)
    for l in range(depth):
        w = w_in[l].T.astype(BF16)

        y_a, xn = _branch_a(h, norm1_g[l], w, a_ln_g[l], a_ln_b[l], a_w_s[l], a_b_s[l])
        q, q_idx, kpe, kpo, w_t = _q_projections(xn, w, c_q, hd, c_qi, qi, c_ki, HEAD_DIM ** -0.5 * LOG2E, bsz)
        k_tok, v_t = _kv_proj(xn, w, c_kv, kv_norm_g[l], w_uk[l].reshape(lat, hd).astype(BF16),
                              w_uv[l].reshape(lat, hd).T.astype(BF16), bsz, seq, ATT_TK)
        bias, w_oa_b, w_ob_b, w_out_b, w_fg_b, w_fu_b, w_fd_b = _indexer(
            q_idx, kpe, kpo, w_t, ATT_TQ, ATT_TK,
            [w_oa[l], w_ob[l], w_out[l], w_ff_gate[l], w_ff_up[l], w_ff_down[l]])
        y_b = _attention(q, k_tok, v_t, bias, ATT_TQ, ATT_TK)
        mixed = _merge(xn, y_a, y_b, w, c_g, c_g + d, w_oa_b, w_ob_b)
        h, hn = _out_proj(h, mixed, w_out_b, norm2_g[l])
        h = _ffn(hn, h, w_fg_b, w_fu_b, w_fd_b, final_g, final_norm=(l == depth - 1))
    return h.reshape(bsz, seq, d)
```

```python
import functools
import math

import jax
import jax.numpy as jnp
from jax import lax
from jax.experimental import pallas as pl
from jax.experimental.pallas import tpu as pltpu

EPS = 1e-6
CHUNK = 64
A_GROUPS = 8
A_BLOCK = 128
LANES = 128
BF16_SUBLANES = 16
HEAD_DIM = 128
IDX_DIM = 64
IDX_HEADS = 16
TOPK_MAX = 256
ONES_ROWS = 16
LOG2E = 1.4426950408889634

VMEM_LIMIT_BYTES = 56 * 1024 * 1024
INT_MIN = -2147483648
ATT_TQ = 256
ATT_TK = 512
UNROLL = 4
NEG_BIG = -1e30

F32 = jnp.float32
BF16 = jnp.bfloat16


def _params(*sem, flags=None):
    return pltpu.CompilerParams(dimension_semantics=sem, vmem_limit_bytes=VMEM_LIMIT_BYTES, flags=flags)


def _nt_dot(a, b):
    return lax.dot_general(a, b, (((1,), (1,)), ((), ())), preferred_element_type=F32)


def _dot(a, b):
    return jnp.dot(a, b, preferred_element_type=F32)


def _rms(x, g):
    return x * lax.rsqrt(jnp.mean(x * x, axis=-1, keepdims=True) + EPS) * g


def _row_specs(cols, row0, n):
    bh = math.gcd(row0, n) if row0 else n
    assert bh % LANES == 0
    return [pl.BlockSpec((bh, cols), functools.partial(lambda i, ib: (ib, 0), ib=row0 // bh + c)) for c in range(n // bh)]


def _qproj_kernel(xn_ref, *refs, n_q, n_qi, q_scale):
    wq, wqi, wkw = refs[:n_q], refs[n_q:n_q + n_qi], refs[n_q + n_qi]
    q_ref, qi_ref, kpe_ref, kpo_ref, wt_ref = refs[n_q + n_qi + 1:]
    xn = xn_ref[...]
    for c, w_ref in enumerate(wq):
        bw = w_ref.shape[0]
        q_ref[:, c * bw:(c + 1) * bw] = (_nt_dot(xn, w_ref[...]) * q_scale).astype(q_ref.dtype)
    for c, w_ref in enumerate(wqi):
        bw = w_ref.shape[0]
        qi_ref[:, c * bw:(c + 1) * bw] = _nt_dot(xn, w_ref[...]).astype(qi_ref.dtype)
    kw = _nt_dot(xn, wkw[...])
    lane = lax.broadcasted_iota(jnp.int32, kw.shape, 1)
    kpe_ref[...] = jnp.where(lane < IDX_DIM, kw, 0.0).astype(kpe_ref.dtype)
    kpo_ref[...] = jnp.where(lane >= LANES - IDX_DIM, pltpu.roll(kw, LANES - IDX_DIM, 1), 0.0).astype(kpo_ref.dtype)
    wt_ref[...] = kw.T[IDX_DIM:IDX_DIM + IDX_HEADS, :]


def _q_projections(xn, w, c_q, n_q, c_qi, n_qi, c_kw, q_scale, bsz, tm=512):
    m, d = xn.shape
    seq = m // bsz
    nt = seq // tm
    assert 2 * IDX_DIM == LANES
    pad_spec = pl.BlockSpec((None, tm, LANES), lambda i: (i // nt, i % nt, 0))
    q_specs, qi_specs, kw_specs = _row_specs(d, c_q, n_q), _row_specs(d, c_qi, n_qi), _row_specs(d, c_kw, LANES)
    row = lambda n: pl.BlockSpec((tm, n), lambda i: (i, 0))
    return pl.pallas_call(
        functools.partial(_qproj_kernel, n_q=len(q_specs), n_qi=len(qi_specs), q_scale=q_scale),
        out_shape=(jax.ShapeDtypeStruct((m, n_q), BF16), jax.ShapeDtypeStruct((m, n_qi), BF16),
                   jax.ShapeDtypeStruct((bsz, seq, LANES), BF16), jax.ShapeDtypeStruct((bsz, seq, LANES), BF16),
                   jax.ShapeDtypeStruct((bsz, IDX_HEADS, seq), F32)),
        grid=(m // tm,),
        in_specs=[row(d)] + q_specs + qi_specs + kw_specs,
        out_specs=(row(n_q), row(n_qi), pad_spec, pad_spec,
                   pl.BlockSpec((None, IDX_HEADS, tm), lambda i: (i // nt, 0, i % nt))),
        compiler_params=_params("parallel"),
        name="q_proj",
    )(xn, *([w] * (len(q_specs) + len(qi_specs) + 1)))


def _branch_a_kernel(x_ref, g1_ref, w_ref, lng_ref, lnb_ref, ws_ref, bst_ref, o_ref, xn_ref, z_ref, *, tm, tn, width):
    j = pl.program_id(1)

    @pl.when(j == 0)
    def _():
        xn_ref[...] = _rms(x_ref[...], g1_ref[...]).astype(xn_ref.dtype)

    z_ref[j] = jax.nn.gelu(_nt_dot(xn_ref[...], w_ref[...]))

    @pl.when(j == pl.num_programs(1) - 1)
    def _():
        per_half = width // tn
        gw = width // A_GROUPS
        row = lax.broadcasted_iota(jnp.int32, (A_BLOCK, A_BLOCK), 0)
        col = lax.broadcasted_iota(jnp.int32, (A_BLOCK, A_BLOCK), 1)
        causal = (col // CHUNK) <= (row // CHUNK)
        for r in range(tm // A_BLOCK):
            rows = slice(r * A_BLOCK, (r + 1) * A_BLOCK)
            v = jnp.concatenate([z_ref[per_half + c, rows, :] for c in range(per_half)], axis=-1)
            vc = v - jnp.mean(v, axis=-1, keepdims=True)
            vn = vc * lax.rsqrt(jnp.mean(vc * vc, axis=-1, keepdims=True) + EPS)
            vn = (vn * lng_ref[...] + lnb_ref[...]).astype(BF16)
            for g in range(A_GROUPS):
                wm = jnp.where(causal, ws_ref[g], 0.0).astype(BF16)
                sv = _dot(wm, vn[:, g * gw:(g + 1) * gw]) + bst_ref[:, g:g + 1]
                c, off = divmod(g * gw, tn)
                u = z_ref[c, rows, off:off + gw]
                o_ref[rows, g * gw:(g + 1) * gw] = (u * sv).astype(o_ref.dtype)


def _branch_a(x, g1, w_za, ln_g, ln_b, w_s, b_s, tm=512, tn=2048):
    m, d = x.shape
    width = ln_g.shape[0]
    assert (width // A_GROUPS) <= tn and tn % (width // A_GROUPS) == 0
    kern = functools.partial(_branch_a_kernel, tm=tm, tn=tn, width=width)
    return pl.pallas_call(
        kern,
        out_shape=(jax.ShapeDtypeStruct((m, width), BF16), jax.ShapeDtypeStruct((m, d), BF16)),
        grid=(m // tm, 2 * width // tn),
        in_specs=[
            pl.BlockSpec((tm, d), lambda i, j: (i, 0)),
            pl.BlockSpec((1, d), lambda i, j: (0, 0)),
            pl.BlockSpec((tn, d), lambda i, j: (j, 0)),
            pl.BlockSpec((1, width), lambda i, j: (0, 0)),
            pl.BlockSpec((1, width), lambda i, j: (0, 0)),
            pl.BlockSpec((A_GROUPS, A_BLOCK, A_BLOCK), lambda i, j: (0, 0, 0)),
            pl.BlockSpec((A_BLOCK, A_GROUPS), lambda i, j: (0, 0)),
        ],
        out_specs=(pl.BlockSpec((tm, width), lambda i, j: (i, 0)), pl.BlockSpec((tm, d), lambda i, j: (i, 0))),
        scratch_shapes=[pltpu.VMEM((2 * width // tn, tm, tn), F32)],
        compiler_params=_params("parallel", "arbitrary"),
        name="branch_a",
    )(x, g1.reshape(1, d), w_za, ln_g.reshape(1, width), ln_b.reshape(1, width), w_s, b_s.T)


def _kv_kernel(xn_ref, wc_ref, g_ref, wuk_ref, wuvt_ref, k_ref, vt_ref):
    c = _nt_dot(xn_ref[...], wc_ref[...])
    cn = _rms(c, g_ref[...]).astype(BF16)
    k_ref[...] = _dot(cn, wuk_ref[...]).astype(k_ref.dtype)
    vt = _nt_dot(wuvt_ref[...], cn).astype(vt_ref.dtype)
    rows = HEAD_DIM + ONES_ROWS
    for h in range(vt.shape[0] // HEAD_DIM):
        vt_ref[h * rows:h * rows + HEAD_DIM, :] = vt[h * HEAD_DIM:(h + 1) * HEAD_DIM]
        vt_ref[h * rows + HEAD_DIM:(h + 1) * rows, :] = jnp.ones((ONES_ROWS, vt.shape[1]), vt_ref.dtype)


def _kv_proj(xn, w_c, col0, g, w_uk, w_uv_t, bsz, seq, tk):
    m, d = xn.shape
    lat, hd = w_uk.shape
    assert col0 % lat == 0
    j0 = col0 // lat
    nt = seq // tk
    vrows = hd // HEAD_DIM * (HEAD_DIM + ONES_ROWS)
    return pl.pallas_call(
        _kv_kernel,
        out_shape=(jax.ShapeDtypeStruct((m, hd), BF16), jax.ShapeDtypeStruct((bsz, nt, vrows, tk), BF16)),
        grid=(m // tk,),
        in_specs=[
            pl.BlockSpec((tk, d), lambda i: (i, 0)),
            pl.BlockSpec((lat, d), lambda i: (j0, 0)),
            pl.BlockSpec((1, lat), lambda i: (0, 0)),
            pl.BlockSpec((lat, hd), lambda i: (0, 0)),
            pl.BlockSpec((hd, lat), lambda i: (0, 0)),
        ],
        out_specs=(pl.BlockSpec((tk, hd), lambda i: (i, 0)),
                   pl.BlockSpec((None, None, vrows, tk), lambda i: (i // nt, i % nt, 0, 0))),
        compiler_params=_params("parallel"),
        name="kv_proj",
    )(xn, w_c, g.reshape(1, lat), w_uk, w_uv_t)


def _n_key_tiles(qt, tq, tk):
    return lax.div(qt * tq + tq + tk - 1, tk)


def _indexer_kernel(qi_ref, kpe_ref, kpo_ref, wt_ref, *refs, tq, tk, k_sel, n_cast):
    cast_in, o_ref, cast_out, sc_ref = refs[:n_cast], refs[n_cast], refs[n_cast + 1:2 * n_cast + 1], refs[-1]
    for w_ref, wb_ref in zip(cast_in, cast_out):
        wb_ref[...] = w_ref[...].astype(wb_ref.dtype)

    qt = pl.program_id(1)
    n_kt = _n_key_tiles(qt, tq, tk)
    nk = sc_ref.shape[0]
    neg_inf = -jnp.inf

    def count(pred):
        rows = 16
        def body(kt, cnt):
            hit = jnp.where(pred(sc_ref[kt], kt), 1.0, 0.0)
            return cnt + jnp.sum(hit.reshape(tk // rows, rows, tq), axis=0)
        cnt = lax.fori_loop(0, n_kt, body, jnp.zeros((rows, tq), F32))
        return jnp.sum(cnt, axis=0, keepdims=True)

    def key_pos(kt):
        return kt * tk + lax.broadcasted_iota(jnp.int32, (tk, tq), 0)

    wt = wt_ref[...] * (IDX_DIM ** -0.5 * IDX_HEADS ** -0.5)
    q_chunk = (qt * tq + lax.broadcasted_iota(jnp.int32, (1, tq), 1)) // CHUNK

    def score_tile(kt):
        k0 = pl.multiple_of(kt * tk, tk)
        ke = kpe_ref[pl.ds(k0, tk), :]
        ko = kpo_ref[pl.ds(k0, tk), :]
        acc = jnp.zeros((tk, tq), F32)
        for j in range(IDX_HEADS // 2):
            qp = qi_ref[:, j * 128:(j + 1) * 128]
            acc = acc + wt[2 * j:2 * j + 1, :] * jnp.maximum(_nt_dot(ke, qp), 0.0)
            acc = acc + wt[2 * j + 1:2 * j + 2, :] * jnp.maximum(_nt_dot(ko, qp), 0.0)
        k_chunk = (k0 + lax.broadcasted_iota(jnp.int32, (tk, 1), 0)) // CHUNK
        sc_ref[kt] = jnp.where(k_chunk <= q_chunk, acc, neg_inf)

    def score_pair(i, carry):
        score_tile(2 * i)
        score_tile(2 * i + 1)
        return carry
    lax.fori_loop(0, lax.div(n_kt, 2), score_pair, 0)

    @pl.when(lax.rem(n_kt, 2) == 1)
    def _():
        score_tile(n_kt - 1)

    def key_to_float(key):
        return lax.bitcast_convert_type(key ^ ((key >> 31) & 0x7FFFFFFF), F32)

    def bit_body(i, carry):
        tau, n_tau = carry
        cand = tau ^ jnp.left_shift(jnp.int32(1), 31 - i)
        cand_f = key_to_float(cand)
        n = count(lambda s, kt: s >= cand_f)
        accept = n >= k_sel
        return jnp.where(accept, cand, tau), jnp.where(accept, n, n_tau)
    every = jnp.broadcast_to((n_kt * tk).astype(F32), (1, tq))
    tau, n_tau = lax.fori_loop(0, 32, bit_body, (jnp.full((1, tq), INT_MIN, jnp.int32), every))
    key_neg_inf = INT_MIN + 0x7FFFFF
    tau_f = jnp.where(tau <= key_neg_inf, neg_inf, key_to_float(tau))
    has_ties = jnp.max(n_tau) > k_sel

    def write_mask(sel):
        def body(kt, carry):
            s = sc_ref[kt]
            keep = sel(s, kt) & (s > neg_inf)
            o_ref[kt] = jnp.where(keep, 0.0, NEG_BIG).astype(o_ref.dtype)
            return carry
        lax.fori_loop(0, n_kt, body, 0)

    @pl.when(jnp.logical_not(has_ties))
    def _():
        write_mask(lambda s, kt: s >= tau_f)

    @pl.when(has_ties)
    def _():
        need = k_sel - count(lambda s, kt: s > tau_f)

        def idx_body(i, cut):
            cand = cut + jnp.left_shift(jnp.int32(1), 12 - i)
            n = count(lambda s, kt: (s == tau_f) & (key_pos(kt) < cand))
            return jnp.where(n <= need, cand, cut)
        cut = lax.fori_loop(0, 13, idx_body, jnp.zeros((1, tq), jnp.int32))
        write_mask(lambda s, kt: (s > tau_f) | ((s == tau_f) & (key_pos(kt) < cut)))

    def fill_body(kt, carry):
        o_ref[kt] = jnp.full((tk, tq), NEG_BIG, o_ref.dtype)
        return carry
    lax.fori_loop(n_kt, nk, fill_body, 0)


def _indexer(q_idx, kpe, kpo, w_t, tq, tk, cast_weights):
    bsz, seq, _ = kpe.shape
    nq, nk = seq // tq, seq // tk
    k_sel = min(TOPK_MAX, seq // 4)
    assert tq >= k_sel and tq % CHUNK == 0 and tk % tq == 0
    qi_cols = q_idx.shape[1]
    steps = bsz * nq
    assert all(n % (steps * BF16_SUBLANES) == 0 and row0 % BF16_SUBLANES == 0 for _, row0, n in cast_weights)

    def slab_in(w, row0, n):
        r = n // steps
        return pl.BlockSpec((pl.Element(r), pl.Element(w.shape[1])),
                            lambda b, t: (pl.multiple_of(row0 + (b * nq + t) * r, BF16_SUBLANES), 0))
    slab_out = lambda w, row0, n: pl.BlockSpec((n // steps, w.shape[1]), lambda b, t: (b * nq + t, 0))
    n_cast = len(cast_weights)
    return pl.pallas_call(
        functools.partial(_indexer_kernel, tq=tq, tk=tk, k_sel=k_sel, n_cast=n_cast),
        out_shape=[jax.ShapeDtypeStruct((bsz, nq, nk, tk, tq), F32)]
        + [jax.ShapeDtypeStruct((n, w.shape[1]), BF16) for w, _, n in cast_weights],
        grid=(bsz, nq),
        in_specs=[
            pl.BlockSpec((tq, qi_cols), lambda b, t: (b * nq + t, 0)),
            pl.BlockSpec((None, seq, 2 * IDX_DIM), lambda b, t: (b, 0, 0)),
            pl.BlockSpec((None, seq, 2 * IDX_DIM), lambda b, t: (b, 0, 0)),
            pl.BlockSpec((None, IDX_HEADS, tq), lambda b, t: (b, 0, t)),
        ] + [slab_in(*c) for c in cast_weights],
        out_specs=[pl.BlockSpec((None, None, nk, tk, tq), lambda b, t: (b, t, 0, 0, 0))]
        + [slab_out(*c) for c in cast_weights],
        scratch_shapes=[pltpu.VMEM((nk, tk, tq), F32)],
        compiler_params=_params("parallel", "parallel"),
        name="indexer",
    )(q_idx, kpe, kpo, w_t, *[w for w, _, _ in cast_weights])


def _attention_kernel(q_ref, k_ref, vt_ref, bias_ref, o_ref, m_ref, acc_ref, sa_ref, sb_ref, ma_ref, mb_ref,
                      *, tq, tk, group):
    qt = pl.program_id(2)
    n_kt = _n_key_tiles(qt, tq, tk)
    rows = HEAD_DIM + ONES_ROWS
    m_ref[...] = jnp.full(m_ref.shape, NEG_BIG, F32)
    acc_ref[...] = jnp.zeros(acc_ref.shape, F32)

    def scores(kt, s_ref, mx_ref):
        k0 = pl.multiple_of(kt * tk, tk)
        for g in range(group):
            cols = slice(g * HEAD_DIM, (g + 1) * HEAD_DIM)
            s = _nt_dot(k_ref[pl.ds(k0, tk), cols], q_ref[:, cols]) + bias_ref[kt]
            s_ref[g] = s
            mx_ref[g] = jnp.max(s, axis=0, keepdims=True)

    def consume(kt, s_ref, mx_ref):
        for g in range(group):
            m_old = m_ref[g]
            m_new = jnp.maximum(m_old, mx_ref[g])
            m_ref[g] = m_new
            p = jnp.exp2(s_ref[g] - m_new).astype(BF16)
            pv = _dot(vt_ref[kt, g * rows:(g + 1) * rows, :], p)
            acc_ref[g] = jnp.exp2(m_old - m_new) * acc_ref[g] + pv

    bufs = ((sa_ref, ma_ref), (sb_ref, mb_ref))

    def run(kt0, count, prefetch):
        for c in range(count):
            if c + 1 < count or prefetch:
                scores(kt0 + c + 1, *bufs[(c + 1) % 2])
            consume(kt0 + c, *bufs[c % 2])

    scores(0, sa_ref, ma_ref)
    n_quads = lax.div(n_kt - 1, UNROLL)

    def quad_body(i, carry):
        run(UNROLL * i, UNROLL, prefetch=True)
        return carry
    lax.fori_loop(0, n_quads, quad_body, 0)
    kt = UNROLL * n_quads
    for rest in range(1, UNROLL + 1):
        @pl.when(n_kt - kt == rest)
        def _(rest=rest):
            run(kt, rest, prefetch=False)

    for g in range(group):
        acc = acc_ref[g]
        out = acc[:HEAD_DIM] / acc[HEAD_DIM:HEAD_DIM + 1]
        o_ref[:, g * HEAD_DIM:(g + 1) * HEAD_DIM] = out.T.astype(o_ref.dtype)


def _attention(q, k, vt, bias, tq, tk, group=8):
    bsz, nk, vrows, _ = vt.shape
    rows = HEAD_DIM + ONES_ROWS
    hd = vrows // rows * HEAD_DIM
    seq = nk * tk
    nq = seq // tq
    gw = group * HEAD_DIM
    return pl.pallas_call(
        functools.partial(_attention_kernel, tq=tq, tk=tk, group=group),
        out_shape=jax.ShapeDtypeStruct((bsz * seq, hd), BF16),
        grid=(bsz, hd // gw, nq),
        in_specs=[
            pl.BlockSpec((tq, gw), lambda b, g, t: (b * nq + t, g)),
            pl.BlockSpec((None, seq, gw), lambda b, g, t: (b, 0, g)),
            pl.BlockSpec((None, nk, group * rows, tk), lambda b, g, t: (b, 0, g, 0)),
            pl.BlockSpec((None, None, nk, tk, tq), lambda b, g, t: (b, t, 0, 0, 0)),
        ],
        out_specs=pl.BlockSpec((tq, gw), lambda b, g, t: (b * nq + t, g)),
        scratch_shapes=[pltpu.VMEM((group, 1, tq), F32), pltpu.VMEM((group, rows, tq), F32),
                        pltpu.VMEM((group, tk, tq), F32), pltpu.VMEM((group, tk, tq), F32),
                        pltpu.VMEM((group, 1, tq), F32), pltpu.VMEM((group, 1, tq), F32)],
        compiler_params=_params("parallel", "parallel", "arbitrary"),
        name="attention",
    )(q, k.reshape(bsz, seq, hd), vt, bias)


def _merge_kernel(xn_ref, ya_ref, yb_ref, wga_ref, wgb_ref, woa_ref, wob_ref, o_ref):
    xn = xn_ref[...]
    ga = jax.nn.sigmoid(_nt_dot(xn, wga_ref[...]))
    gb = jax.nn.sigmoid(_nt_dot(xn, wgb_ref[...]))
    o_ref[...] = (ga * _dot(ya_ref[...], woa_ref[...]) + gb * _dot(yb_ref[...], wob_ref[...])).astype(o_ref.dtype)


def _merge(xn, ya, yb, w_gates, w_oa, w_ob, tm=1024, tn=512):
    m, d = xn.shape
    row = lambda k: pl.BlockSpec((tm, k), lambda i, j: (i, 0))
    colw = lambda k: pl.BlockSpec((k, tn), lambda i, j: (0, j))
    gate = lambda j0: pl.BlockSpec((tn, d), lambda i, j: (j0 + j, 0))
    return pl.pallas_call(
        _merge_kernel,
        out_shape=jax.ShapeDtypeStruct((m, d), BF16),
        grid=(m // tm, d // tn),
        in_specs=[row(d), row(ya.shape[1]), row(yb.shape[1]),
                  gate(0), gate(d // tn), colw(ya.shape[1]), colw(yb.shape[1])],
        out_specs=pl.BlockSpec((tm, tn), lambda i, j: (i, j)),
        compiler_params=_params("parallel", "arbitrary"),
        name="merge",
    )(xn, ya, yb, w_gates, w_gates, w_oa, w_ob)


def _out_kernel(x_ref, mix_ref, w_ref, g_ref, h_ref, hn_ref):
    h = x_ref[...] + _dot(mix_ref[...], w_ref[...])
    h_ref[...] = h
    hn_ref[...] = _rms(h, g_ref[...]).astype(hn_ref.dtype)


def _out_proj(x, mixed, w_out, g, tm=512):
    m, d = x.shape
    row = pl.BlockSpec((tm, d), lambda i: (i, 0))
    return pl.pallas_call(
        _out_kernel,
        out_shape=(jax.ShapeDtypeStruct((m, d), F32), jax.ShapeDtypeStruct((m, d), BF16)),
        grid=(m // tm,),
        in_specs=[row, row, pl.BlockSpec((d, d), lambda i: (0, 0)), pl.BlockSpec((1, d), lambda i: (0, 0))],
        out_specs=(row, row),
        compiler_params=_params("parallel"),
        name="out_proj",
    )(x, mixed, w_out, g.reshape(1, d))


def _ffn_kernel(hn_ref, h_ref, wg_ref, wu_ref, wd_ref, g_ref, o_ref, acc_ref, *, final_norm):
    j = pl.program_id(1)

    @pl.when(j == 0)
    def _():
        acc_ref[...] = jnp.zeros(acc_ref.shape, F32)

    hn = hn_ref[...]
    t = (jax.nn.silu(_dot(hn, wg_ref[...])) * _dot(hn, wu_ref[...])).astype(BF16)
    acc_ref[...] += _dot(t, wd_ref[...])

    @pl.when(j == pl.num_programs(1) - 1)
    def _():
        h2 = h_ref[...] + acc_ref[...]
        o_ref[...] = _rms(h2, g_ref[...]) if final_norm else h2


def _ffn(hn, h, w_gate, w_up, w_down, g, final_norm, tm=512, tf=512):
    m, d = hn.shape
    f = w_gate.shape[1]
    row = pl.BlockSpec((tm, d), lambda i, j: (i, 0))
    return pl.pallas_call(
        functools.partial(_ffn_kernel, final_norm=final_norm),
        out_shape=jax.ShapeDtypeStruct((m, d), F32),
        grid=(m // tm, f // tf),
        in_specs=[row, row,
                  pl.BlockSpec((d, tf), lambda i, j: (0, j)),
                  pl.BlockSpec((d, tf), lambda i, j: (0, j)),
                  pl.BlockSpec((tf, d), lambda i, j: (j, 0)),
                  pl.BlockSpec((1, d), lambda i, j: (0, 0))],
        out_specs=row,
        scratch_shapes=[pltpu.VMEM((tm, d), F32)],
        compiler_params=_params("parallel", "arbitrary"),
        name="ffn",
    )(hn, h, w_gate, w_up, w_down, g.reshape(1, d))


def kernel(x, norm1_g, w_in, a_ln_g, a_ln_b, a_w_s, a_b_s, kv_norm_g, w_uk, w_uv, w_oa, w_ob, w_out, norm2_g,
           w_ff_gate, w_ff_up, w_ff_down, final_g):
    bsz, seq, d = x.shape
    m = bsz * seq
    depth = norm1_g.shape[0]
    a_width = a_ln_g.shape[1]
    lat = kv_norm_g.shape[1]
    heads, head_dim = w_uk.shape[2], w_uk.shape[3]
    assert head_dim == HEAD_DIM
    hd = heads * head_dim
    qi = IDX_HEADS * IDX_DIM
    c_q = 2 * a_width
    c_kv = c_q + hd
    c_qi = c_kv + lat
    c_ki = c_qi + qi
    c_wi = c_ki + IDX_DIM
    c_g = c_wi + IDX_HEADS

    h = x.reshape(m, d)
    for l in range(depth):
        w_t32 = w_in[l].T
        w = w_t32[:c_ki + LANES].astype(BF16)

        y_a, xn = _branch_a(h, norm1_g[l], w, a_ln_g[l], a_ln_b[l], a_w_s[l], a_b_s[l])
        q, q_idx, kpe, kpo, w_t = _q_projections(xn, w, c_q, hd, c_qi, qi, c_ki, HEAD_DIM ** -0.5 * LOG2E, bsz)
        k_tok, v_t = _kv_proj(xn, w, c_kv, kv_norm_g[l], w_uk[l].reshape(lat, hd).astype(BF16),
                              w_uv[l].reshape(lat, hd).T.astype(BF16), bsz, seq, ATT_TK)
        bias, w_gates, w_oa_b, w_ob_b, w_out_b, w_fg_b, w_fu_b, w_fd_b = _indexer(
            q_idx, kpe, kpo, w_t, ATT_TQ, ATT_TK,
            [(w_t32, c_g, 2 * d)] + [(a[l], 0, a[l].shape[0]) for a in (w_oa, w_ob, w_out, w_ff_gate, w_ff_up, w_ff_down)])
        y_b = _attention(q, k_tok, v_t, bias, ATT_TQ, ATT_TK)
        mixed = _merge(xn, y_a, y_b, w_gates, w_oa_b, w_ob_b)
        h, hn = _out_proj(h, mixed, w_out_b, norm2_g[l])
        h = _ffn(hn, h, w_fg_b, w_fu_b, w_fd_b, final_g, final_norm=(l == depth - 1))
    return h.reshape(bsz, seq, d)
```

```python
import functools
import math

import jax
import jax.numpy as jnp
from jax import lax
from jax.experimental import pallas as pl
from jax.experimental.pallas import tpu as pltpu

EPS = 1e-6
CHUNK = 64
A_GROUPS = 8
A_BLOCK = 128
LANES = 128
BF16_SUBLANES = 16
HEAD_DIM = 128
IDX_DIM = 64
IDX_HEADS = 16
TOPK_MAX = 256
ONES_ROWS = 16
LOG2E = 1.4426950408889634

VMEM_LIMIT_BYTES = 56 * 1024 * 1024
INT_MIN = -2147483648
ATT_TQ = 256
ATT_TK = 512
UNROLL = 2
NEG_BIG = -1e30

F32 = jnp.float32
BF16 = jnp.bfloat16


def _params(*sem, flags=None):
    return pltpu.CompilerParams(dimension_semantics=sem, vmem_limit_bytes=VMEM_LIMIT_BYTES, flags=flags)


def _nt_dot(a, b):
    return lax.dot_general(a, b, (((1,), (1,)), ((), ())), preferred_element_type=F32)


def _dot(a, b):
    return jnp.dot(a, b, preferred_element_type=F32)


def _rms(x, g):
    return x * lax.rsqrt(jnp.mean(x * x, axis=-1, keepdims=True) + EPS) * g


def _row_specs(cols, row0, n):
    bh = math.gcd(row0, n) if row0 else n
    assert bh % LANES == 0
    return [pl.BlockSpec((bh, cols), functools.partial(lambda i, ib: (ib, 0), ib=row0 // bh + c)) for c in range(n // bh)]


def _qproj_kernel(xn_ref, *refs, n_q, n_qi, q_scale):
    wq, wqi, wkw = refs[:n_q], refs[n_q:n_q + n_qi], refs[n_q + n_qi]
    q_ref, qi_ref, kpe_ref, kpo_ref, wt_ref = refs[n_q + n_qi + 1:]
    xn = xn_ref[...]
    for c, w_ref in enumerate(wq):
        bw = w_ref.shape[0]
        q_ref[:, c * bw:(c + 1) * bw] = (_nt_dot(xn, w_ref[...]) * q_scale).astype(q_ref.dtype)
    for c, w_ref in enumerate(wqi):
        bw = w_ref.shape[0]
        qi_ref[:, c * bw:(c + 1) * bw] = _nt_dot(xn, w_ref[...]).astype(qi_ref.dtype)
    kw = _nt_dot(xn, wkw[...])
    lane = lax.broadcasted_iota(jnp.int32, kw.shape, 1)
    kpe_ref[...] = jnp.where(lane < IDX_DIM, kw, 0.0).astype(kpe_ref.dtype)
    kpo_ref[...] = jnp.where(lane >= LANES - IDX_DIM, pltpu.roll(kw, LANES - IDX_DIM, 1), 0.0).astype(kpo_ref.dtype)
    wt_ref[...] = kw.T[IDX_DIM:IDX_DIM + IDX_HEADS, :]


def _q_projections(xn, w, c_q, n_q, c_qi, n_qi, c_kw, q_scale, bsz, tm=512):
    m, d = xn.shape
    seq = m // bsz
    nt = seq // tm
    assert 2 * IDX_DIM == LANES
    pad_spec = pl.BlockSpec((None, tm, LANES), lambda i: (i // nt, i % nt, 0))
    q_specs, qi_specs, kw_specs = _row_specs(d, c_q, n_q), _row_specs(d, c_qi, n_qi), _row_specs(d, c_kw, LANES)
    row = lambda n: pl.BlockSpec((tm, n), lambda i: (i, 0))
    return pl.pallas_call(
        functools.partial(_qproj_kernel, n_q=len(q_specs), n_qi=len(qi_specs), q_scale=q_scale),
        out_shape=(jax.ShapeDtypeStruct((m, n_q), BF16), jax.ShapeDtypeStruct((m, n_qi), BF16),
                   jax.ShapeDtypeStruct((bsz, seq, LANES), BF16), jax.ShapeDtypeStruct((bsz, seq, LANES), BF16),
                   jax.ShapeDtypeStruct((bsz, IDX_HEADS, seq), F32)),
        grid=(m // tm,),
        in_specs=[row(d)] + q_specs + qi_specs + kw_specs,
        out_specs=(row(n_q), row(n_qi), pad_spec, pad_spec,
                   pl.BlockSpec((None, IDX_HEADS, tm), lambda i: (i // nt, 0, i % nt))),
        compiler_params=_params("parallel"),
        name="q_proj",
    )(xn, *([w] * (len(q_specs) + len(qi_specs) + 1)))


def _branch_a_kernel(x_ref, g1_ref, w_ref, lng_ref, lnb_ref, ws_ref, bst_ref, o_ref, xn_ref, z_ref, *, tm, tn, width):
    j = pl.program_id(1)

    @pl.when(j == 0)
    def _():
        xn_ref[...] = _rms(x_ref[...], g1_ref[...]).astype(xn_ref.dtype)

    z_ref[j] = jax.nn.gelu(_nt_dot(xn_ref[...], w_ref[...]))

    @pl.when(j == pl.num_programs(1) - 1)
    def _():
        per_half = width // tn
        gw = width // A_GROUPS
        row = lax.broadcasted_iota(jnp.int32, (A_BLOCK, A_BLOCK), 0)
        col = lax.broadcasted_iota(jnp.int32, (A_BLOCK, A_BLOCK), 1)
        causal = (col // CHUNK) <= (row // CHUNK)
        for r in range(tm // A_BLOCK):
            rows = slice(r * A_BLOCK, (r + 1) * A_BLOCK)
            v = jnp.concatenate([z_ref[per_half + c, rows, :] for c in range(per_half)], axis=-1)
            vc = v - jnp.mean(v, axis=-1, keepdims=True)
            vn = vc * lax.rsqrt(jnp.mean(vc * vc, axis=-1, keepdims=True) + EPS)
            vn = (vn * lng_ref[...] + lnb_ref[...]).astype(BF16)
            for g in range(A_GROUPS):
                wm = jnp.where(causal, ws_ref[g], 0.0).astype(BF16)
                sv = _dot(wm, vn[:, g * gw:(g + 1) * gw]) + bst_ref[:, g:g + 1]
                c, off = divmod(g * gw, tn)
                u = z_ref[c, rows, off:off + gw]
                o_ref[rows, g * gw:(g + 1) * gw] = (u * sv).astype(o_ref.dtype)


def _branch_a(x, g1, w_za, ln_g, ln_b, w_s, b_s, tm=512, tn=2048):
    m, d = x.shape
    width = ln_g.shape[0]
    assert (width // A_GROUPS) <= tn and tn % (width // A_GROUPS) == 0
    kern = functools.partial(_branch_a_kernel, tm=tm, tn=tn, width=width)
    return pl.pallas_call(
        kern,
        out_shape=(jax.ShapeDtypeStruct((m, width), BF16), jax.ShapeDtypeStruct((m, d), BF16)),
        grid=(m // tm, 2 * width // tn),
        in_specs=[
            pl.BlockSpec((tm, d), lambda i, j: (i, 0)),
            pl.BlockSpec((1, d), lambda i, j: (0, 0)),
            pl.BlockSpec((tn, d), lambda i, j: (j, 0)),
            pl.BlockSpec((1, width), lambda i, j: (0, 0)),
            pl.BlockSpec((1, width), lambda i, j: (0, 0)),
            pl.BlockSpec((A_GROUPS, A_BLOCK, A_BLOCK), lambda i, j: (0, 0, 0)),
            pl.BlockSpec((A_BLOCK, A_GROUPS), lambda i, j: (0, 0)),
        ],
        out_specs=(pl.BlockSpec((tm, width), lambda i, j: (i, 0)), pl.BlockSpec((tm, d), lambda i, j: (i, 0))),
        scratch_shapes=[pltpu.VMEM((2 * width // tn, tm, tn), F32)],
        compiler_params=_params("parallel", "arbitrary"),
        name="branch_a",
    )(x, g1.reshape(1, d), w_za, ln_g.reshape(1, width), ln_b.reshape(1, width), w_s, b_s.T)


def _kv_kernel(xn_ref, wc_ref, g_ref, wuk_ref, wuvt_ref, k_ref, vt_ref):
    c = _nt_dot(xn_ref[...], wc_ref[...])
    cn = _rms(c, g_ref[...]).astype(BF16)
    k_ref[...] = _dot(cn, wuk_ref[...]).astype(k_ref.dtype)
    vt = _nt_dot(wuvt_ref[...], cn).astype(vt_ref.dtype)
    rows = HEAD_DIM + ONES_ROWS
    for h in range(vt.shape[0] // HEAD_DIM):
        vt_ref[h * rows:h * rows + HEAD_DIM, :] = vt[h * HEAD_DIM:(h + 1) * HEAD_DIM]
        vt_ref[h * rows + HEAD_DIM:(h + 1) * rows, :] = jnp.ones((ONES_ROWS, vt.shape[1]), vt_ref.dtype)


def _kv_proj(xn, w_c, col0, g, w_uk, w_uv_t, bsz, seq, tk):
    m, d = xn.shape
    lat, hd = w_uk.shape
    assert col0 % lat == 0
    j0 = col0 // lat
    nt = seq // tk
    vrows = hd // HEAD_DIM * (HEAD_DIM + ONES_ROWS)
    return pl.pallas_call(
        _kv_kernel,
        out_shape=(jax.ShapeDtypeStruct((m, hd), BF16), jax.ShapeDtypeStruct((bsz, nt, vrows, tk), BF16)),
        grid=(m // tk,),
        in_specs=[
            pl.BlockSpec((tk, d), lambda i: (i, 0)),
            pl.BlockSpec((lat, d), lambda i: (j0, 0)),
            pl.BlockSpec((1, lat), lambda i: (0, 0)),
            pl.BlockSpec((lat, hd), lambda i: (0, 0)),
            pl.BlockSpec((hd, lat), lambda i: (0, 0)),
        ],
        out_specs=(pl.BlockSpec((tk, hd), lambda i: (i, 0)),
                   pl.BlockSpec((None, None, vrows, tk), lambda i: (i // nt, i % nt, 0, 0))),
        compiler_params=_params("parallel"),
        name="kv_proj",
    )(xn, w_c, g.reshape(1, lat), w_uk, w_uv_t)


def _n_key_tiles(qt, tq, tk):
    return lax.div(qt * tq + tq + tk - 1, tk)


def _indexer_kernel(qi_ref, kpe_ref, kpo_ref, wt_ref, *refs, tq, tk, k_sel, n_cast):
    cast_in, o_ref, cast_out, sc_ref = refs[:n_cast], refs[n_cast], refs[n_cast + 1:2 * n_cast + 1], refs[-1]
    for w_ref, wb_ref in zip(cast_in, cast_out):
        wb_ref[...] = w_ref[...].astype(wb_ref.dtype)

    qt = pl.program_id(1)
    n_kt = _n_key_tiles(qt, tq, tk)
    nk = sc_ref.shape[0]
    neg_inf = -jnp.inf

    def count(pred):
        rows = 16
        def body(kt, cnt):
            hit = jnp.where(pred(sc_ref[kt], kt), 1.0, 0.0)
            return cnt + jnp.sum(hit.reshape(tk // rows, rows, tq), axis=0)
        cnt = lax.fori_loop(0, n_kt, body, jnp.zeros((rows, tq), F32))
        return jnp.sum(cnt, axis=0, keepdims=True)

    def key_pos(kt):
        return kt * tk + lax.broadcasted_iota(jnp.int32, (tk, tq), 0)

    wt = wt_ref[...] * (IDX_DIM ** -0.5 * IDX_HEADS ** -0.5)
    q_chunk = (qt * tq + lax.broadcasted_iota(jnp.int32, (1, tq), 1)) // CHUNK

    def score_tile(kt):
        k0 = pl.multiple_of(kt * tk, tk)
        ke = kpe_ref[pl.ds(k0, tk), :]
        ko = kpo_ref[pl.ds(k0, tk), :]
        acc = jnp.zeros((tk, tq), F32)
        for j in range(IDX_HEADS // 2):
            qp = qi_ref[:, j * 128:(j + 1) * 128]
            acc = acc + wt[2 * j:2 * j + 1, :] * jnp.maximum(_nt_dot(ke, qp), 0.0)
            acc = acc + wt[2 * j + 1:2 * j + 2, :] * jnp.maximum(_nt_dot(ko, qp), 0.0)
        k_chunk = (k0 + lax.broadcasted_iota(jnp.int32, (tk, 1), 0)) // CHUNK
        sc_ref[kt] = jnp.where(k_chunk <= q_chunk, acc, neg_inf)

    def score_pair(i, carry):
        score_tile(2 * i)
        score_tile(2 * i + 1)
        return carry
    lax.fori_loop(0, lax.div(n_kt, 2), score_pair, 0)

    @pl.when(lax.rem(n_kt, 2) == 1)
    def _():
        score_tile(n_kt - 1)

    def key_to_float(key):
        return lax.bitcast_convert_type(key ^ ((key >> 31) & 0x7FFFFFFF), F32)

    def bit_body(i, carry):
        tau, n_tau = carry
        cand = tau ^ jnp.left_shift(jnp.int32(1), 31 - i)
        cand_f = key_to_float(cand)
        n = count(lambda s, kt: s >= cand_f)
        accept = n >= k_sel
        return jnp.where(accept, cand, tau), jnp.where(accept, n, n_tau)
    every = jnp.broadcast_to((n_kt * tk).astype(F32), (1, tq))
    tau, n_tau = lax.fori_loop(0, 32, bit_body, (jnp.full((1, tq), INT_MIN, jnp.int32), every))
    key_neg_inf = INT_MIN + 0x7FFFFF
    tau_f = jnp.where(tau <= key_neg_inf, neg_inf, key_to_float(tau))
    has_ties = jnp.max(n_tau) > k_sel

    def write_mask(sel):
        def body(kt, carry):
            s = sc_ref[kt]
            keep = sel(s, kt) & (s > neg_inf)
            o_ref[kt] = jnp.where(keep, 0.0, NEG_BIG).astype(o_ref.dtype)
            return carry
        lax.fori_loop(0, n_kt, body, 0)

    @pl.when(jnp.logical_not(has_ties))
    def _():
        write_mask(lambda s, kt: s >= tau_f)

    @pl.when(has_ties)
    def _():
        need = k_sel - count(lambda s, kt: s > tau_f)

        def idx_body(i, cut):
            cand = cut + jnp.left_shift(jnp.int32(1), 12 - i)
            n = count(lambda s, kt: (s == tau_f) & (key_pos(kt) < cand))
            return jnp.where(n <= need, cand, cut)
        cut = lax.fori_loop(0, 13, idx_body, jnp.zeros((1, tq), jnp.int32))
        write_mask(lambda s, kt: (s > tau_f) | ((s == tau_f) & (key_pos(kt) < cut)))

    def fill_body(kt, carry):
        o_ref[kt] = jnp.full((tk, tq), NEG_BIG, o_ref.dtype)
        return carry
    lax.fori_loop(n_kt, nk, fill_body, 0)


def _indexer(q_idx, kpe, kpo, w_t, tq, tk, cast_weights):
    bsz, seq, _ = kpe.shape
    nq, nk = seq // tq, seq // tk
    k_sel = min(TOPK_MAX, seq // 4)
    assert tq >= k_sel and tq % CHUNK == 0 and tk % tq == 0
    qi_cols = q_idx.shape[1]
    steps = bsz * nq
    assert all(n % (steps * BF16_SUBLANES) == 0 and row0 % BF16_SUBLANES == 0 for _, row0, n in cast_weights)

    def slab_in(w, row0, n):
        r = n // steps
        return pl.BlockSpec((pl.Element(r), pl.Element(w.shape[1])),
                            lambda b, t: (pl.multiple_of(row0 + (b * nq + t) * r, BF16_SUBLANES), 0))
    slab_out = lambda w, row0, n: pl.BlockSpec((n // steps, w.shape[1]), lambda b, t: (b * nq + t, 0))
    n_cast = len(cast_weights)
    return pl.pallas_call(
        functools.partial(_indexer_kernel, tq=tq, tk=tk, k_sel=k_sel, n_cast=n_cast),
        out_shape=[jax.ShapeDtypeStruct((bsz, nq, nk, tk, tq), F32)]
        + [jax.ShapeDtypeStruct((n, w.shape[1]), BF16) for w, _, n in cast_weights],
        grid=(bsz, nq),
        in_specs=[
            pl.BlockSpec((tq, qi_cols), lambda b, t: (b * nq + t, 0)),
            pl.BlockSpec((None, seq, 2 * IDX_DIM), lambda b, t: (b, 0, 0)),
            pl.BlockSpec((None, seq, 2 * IDX_DIM), lambda b, t: (b, 0, 0)),
            pl.BlockSpec((None, IDX_HEADS, tq), lambda b, t: (b, 0, t)),
        ] + [slab_in(*c) for c in cast_weights],
        out_specs=[pl.BlockSpec((None, None, nk, tk, tq), lambda b, t: (b, t, 0, 0, 0))]
        + [slab_out(*c) for c in cast_weights],
        scratch_shapes=[pltpu.VMEM((nk, tk, tq), F32)],
        compiler_params=_params("parallel", "parallel"),
        name="indexer",
    )(q_idx, kpe, kpo, w_t, *[w for w, _, _ in cast_weights])


def _attention_kernel(q_ref, k_ref, vt_ref, bias_ref, o_ref, m_ref, acc_ref, sa_ref, sb_ref, ma_ref, mb_ref,
                      *, tq, tk, group):
    qt = pl.program_id(2)
    n_kt = _n_key_tiles(qt, tq, tk)
    rows = HEAD_DIM + ONES_ROWS
    m_ref[...] = jnp.full(m_ref.shape, NEG_BIG, F32)
    acc_ref[...] = jnp.zeros(acc_ref.shape, F32)

    def scores(kt, s_ref, mx_ref):
        k0 = pl.multiple_of(kt * tk, tk)
        for g in range(group):
            cols = slice(g * HEAD_DIM, (g + 1) * HEAD_DIM)
            s = _nt_dot(k_ref[pl.ds(k0, tk), cols], q_ref[:, cols]) + bias_ref[kt]
            s_ref[g] = s
            mx_ref[g] = jnp.max(s, axis=0, keepdims=True)

    def consume(kt, s_ref, mx_ref):
        for g in range(group):
            m_old = m_ref[g]
            m_new = jnp.maximum(m_old, mx_ref[g])
            m_ref[g] = m_new
            p = jnp.exp2(s_ref[g] - m_new).astype(BF16)
            pv = _dot(vt_ref[kt, g * rows:(g + 1) * rows, :], p)
            acc_ref[g] = jnp.exp2(m_old - m_new) * acc_ref[g] + pv

    bufs = ((sa_ref, ma_ref), (sb_ref, mb_ref))

    def run(kt0, count, prefetch):
        for c in range(count):
            if c + 1 < count or prefetch:
                scores(kt0 + c + 1, *bufs[(c + 1) % 2])
            consume(kt0 + c, *bufs[c % 2])

    scores(0, sa_ref, ma_ref)
    n_trips = lax.div(n_kt - 1, UNROLL)

    def trip_body(i, carry):
        run(UNROLL * i, UNROLL, prefetch=True)
        return carry
    lax.fori_loop(0, n_trips, trip_body, 0)
    kt = UNROLL * n_trips
    for rest in range(1, UNROLL + 1):
        @pl.when(n_kt - kt == rest)
        def _(rest=rest):
            run(kt, rest, prefetch=False)

    for g in range(group):
        acc = acc_ref[g]
        out = acc[:HEAD_DIM] / acc[HEAD_DIM:HEAD_DIM + 1]
        o_ref[:, g * HEAD_DIM:(g + 1) * HEAD_DIM] = out.T.astype(o_ref.dtype)


def _attention(q, k, vt, bias, tq, tk, group=8):
    bsz, nk, vrows, _ = vt.shape
    rows = HEAD_DIM + ONES_ROWS
    hd = vrows // rows * HEAD_DIM
    seq = nk * tk
    nq = seq // tq
    gw = group * HEAD_DIM
    return pl.pallas_call(
        functools.partial(_attention_kernel, tq=tq, tk=tk, group=group),
        out_shape=jax.ShapeDtypeStruct((bsz * seq, hd), BF16),
        grid=(bsz, hd // gw, nq),
        in_specs=[
            pl.BlockSpec((tq, gw), lambda b, g, t: (b * nq + t, g)),
            pl.BlockSpec((None, seq, gw), lambda b, g, t: (b, 0, g)),
            pl.BlockSpec((None, nk, group * rows, tk), lambda b, g, t: (b, 0, g, 0)),
            pl.BlockSpec((None, None, nk, tk, tq), lambda b, g, t: (b, t, 0, 0, 0)),
        ],
        out_specs=pl.BlockSpec((tq, gw), lambda b, g, t: (b * nq + t, g)),
        scratch_shapes=[pltpu.VMEM((group, 1, tq), F32), pltpu.VMEM((group, rows, tq), F32),
                        pltpu.VMEM((group, tk, tq), F32), pltpu.VMEM((group, tk, tq), F32),
                        pltpu.VMEM((group, 1, tq), F32), pltpu.VMEM((group, 1, tq), F32)],
        compiler_params=_params("parallel", "parallel", "arbitrary"),
        name="attention",
    )(q, k.reshape(bsz, seq, hd), vt, bias)


def _merge_kernel(xn_ref, ya_ref, yb_ref, wga_ref, wgb_ref, woa_ref, wob_ref, o_ref):
    xn = xn_ref[...]
    ga = jax.nn.sigmoid(_nt_dot(xn, wga_ref[...]))
    gb = jax.nn.sigmoid(_nt_dot(xn, wgb_ref[...]))
    o_ref[...] = (ga * _dot(ya_ref[...], woa_ref[...]) + gb * _dot(yb_ref[...], wob_ref[...])).astype(o_ref.dtype)


def _merge(xn, ya, yb, w_gates, w_oa, w_ob, tm=1024, tn=512):
    m, d = xn.shape
    row = lambda k: pl.BlockSpec((tm, k), lambda i, j: (i, 0))
    colw = lambda k: pl.BlockSpec((k, tn), lambda i, j: (0, j))
    gate = lambda j0: pl.BlockSpec((tn, d), lambda i, j: (j0 + j, 0))
    return pl.pallas_call(
        _merge_kernel,
        out_shape=jax.ShapeDtypeStruct((m, d), BF16),
        grid=(m // tm, d // tn),
        in_specs=[row(d), row(ya.shape[1]), row(yb.shape[1]),
                  gate(0), gate(d // tn), colw(ya.shape[1]), colw(yb.shape[1])],
        out_specs=pl.BlockSpec((tm, tn), lambda i, j: (i, j)),
        compiler_params=_params("parallel", "arbitrary"),
        name="merge",
    )(xn, ya, yb, w_gates, w_gates, w_oa, w_ob)


def _out_kernel(x_ref, mix_ref, w_ref, g_ref, h_ref, hn_ref):
    h = x_ref[...] + _dot(mix_ref[...], w_ref[...])
    h_ref[...] = h
    hn_ref[...] = _rms(h, g_ref[...]).astype(hn_ref.dtype)


def _out_proj(x, mixed, w_out, g, tm=512):
    m, d = x.shape
    row = pl.BlockSpec((tm, d), lambda i: (i, 0))
    return pl.pallas_call(
        _out_kernel,
        out_shape=(jax.ShapeDtypeStruct((m, d), F32), jax.ShapeDtypeStruct((m, d), BF16)),
        grid=(m // tm,),
        in_specs=[row, row, pl.BlockSpec((d, d), lambda i: (0, 0)), pl.BlockSpec((1, d), lambda i: (0, 0))],
        out_specs=(row, row),
        compiler_params=_params("parallel"),
        name="out_proj",
    )(x, mixed, w_out, g.reshape(1, d))


def _ffn_kernel(hn_ref, h_ref, wg_ref, wu_ref, wd_ref, g_ref, o_ref, acc_ref, *, final_norm):
    j = pl.program_id(1)

    @pl.when(j == 0)
    def _():
        acc_ref[...] = h_ref[...]

    hn = hn_ref[...]
    t = (jax.nn.silu(_dot(hn, wg_ref[...])) * _dot(hn, wu_ref[...])).astype(BF16)
    acc_ref[...] += _dot(t, wd_ref[...])

    @pl.when(j == pl.num_programs(1) - 1)
    def _():
        o_ref[...] = _rms(acc_ref[...], g_ref[...]) if final_norm else acc_ref[...]


def _ffn(hn, h, w_gate, w_up, w_down, g, final_norm, tm=512, tf=512):
    m, d = hn.shape
    f = w_gate.shape[1]
    row = pl.BlockSpec((tm, d), lambda i, j: (i, 0))
    return pl.pallas_call(
        functools.partial(_ffn_kernel, final_norm=final_norm),
        out_shape=jax.ShapeDtypeStruct((m, d), F32),
        grid=(m // tm, f // tf),
        in_specs=[row, row,
                  pl.BlockSpec((d, tf), lambda i, j: (0, j)),
                  pl.BlockSpec((d, tf), lambda i, j: (0, j)),
                  pl.BlockSpec((tf, d), lambda i, j: (j, 0)),
                  pl.BlockSpec((1, d), lambda i, j: (0, 0))],
        out_specs=row,
        scratch_shapes=[pltpu.VMEM((tm, d), F32)],
        compiler_params=_params("parallel", "arbitrary"),
        name="ffn",
    )(hn, h, w_gate, w_up, w_down, g.reshape(1, d))


def kernel(x, norm1_g, w_in, a_ln_g, a_ln_b, a_w_s, a_b_s, kv_norm_g, w_uk, w_uv, w_oa, w_ob, w_out, norm2_g,
           w_ff_gate, w_ff_up, w_ff_down, final_g):
    bsz, seq, d = x.shape
    m = bsz * seq
    depth = norm1_g.shape[0]
    a_width = a_ln_g.shape[1]
    lat = kv_norm_g.shape[1]
    heads, head_dim = w_uk.shape[2], w_uk.shape[3]
    assert head_dim == HEAD_DIM
    hd = heads * head_dim
    qi = IDX_HEADS * IDX_DIM
    c_q = 2 * a_width
    c_kv = c_q + hd
    c_qi = c_kv + lat
    c_ki = c_qi + qi
    c_wi = c_ki + IDX_DIM
    c_g = c_wi + IDX_HEADS

    h = x.reshape(m, d)
    for l in range(depth):
        w_t32 = w_in[l].T
        w = w_t32[:c_ki + LANES].astype(BF16)

        y_a, xn = _branch_a(h, norm1_g[l], w, a_ln_g[l], a_ln_b[l], a_w_s[l], a_b_s[l])
        q, q_idx, kpe, kpo, w_t = _q_projections(xn, w, c_q, hd, c_qi, qi, c_ki, HEAD_DIM ** -0.5 * LOG2E, bsz)
        k_tok, v_t = _kv_proj(xn, w, c_kv, kv_norm_g[l], w_uk[l].reshape(lat, hd).astype(BF16),
                              w_uv[l].reshape(lat, hd).T.astype(BF16), bsz, seq, ATT_TK)
        bias, w_gates, w_oa_b, w_ob_b, w_out_b, w_fg_b, w_fu_b, w_fd_b = _indexer(
            q_idx, kpe, kpo, w_t, ATT_TQ, ATT_TK,
            [(w_t32, c_g, 2 * d)] + [(a[l], 0, a[l].shape[0]) for a in (w_oa, w_ob, w_out, w_ff_gate, w_ff_up, w_ff_down)])
        y_b = _attention(q, k_tok, v_t, bias, ATT_TQ, ATT_TK)
        mixed = _merge(xn, y_a, y_b, w_gates, w_oa_b, w_ob_b)
        h, hn = _out_proj(h, mixed, w_out_b, norm2_g[l])
        h = _ffn(hn, h, w_fg_b, w_fu_b, w_fd_b, final_g, final_norm=(l == depth - 1))
    return h.reshape(bsz, seq, d)
```

```python
import functools
import math

import jax
import jax.numpy as jnp
from jax import lax
from jax.experimental import pallas as pl
from jax.experimental.pallas import tpu as pltpu

EPS = 1e-6
CHUNK = 64
A_GROUPS = 8
A_BLOCK = 128
LANES = 128
BF16_SUBLANES = 16
HEAD_DIM = 128
IDX_DIM = 64
IDX_HEADS = 16
TOPK_MAX = 256
ONES_ROWS = 16
LOG2E = 1.4426950408889634

VMEM_LIMIT_BYTES = 56 * 1024 * 1024
INT_MIN = -2147483648
ATT_TQ = 256
ATT_TK = 512
UNROLL = 2
NEG_BIG = -1e30

F32 = jnp.float32
BF16 = jnp.bfloat16


def _params(*sem, flags=None):
    return pltpu.CompilerParams(dimension_semantics=sem, vmem_limit_bytes=VMEM_LIMIT_BYTES, flags=flags)


def _nt_dot(a, b):
    return lax.dot_general(a, b, (((1,), (1,)), ((), ())), preferred_element_type=F32)


def _dot(a, b):
    return jnp.dot(a, b, preferred_element_type=F32)


def _rms(x, g):
    return x * lax.rsqrt(jnp.mean(x * x, axis=-1, keepdims=True) + EPS) * g


def _row_specs(cols, row0, n):
    bh = math.gcd(row0, n) if row0 else n
    assert bh % LANES == 0
    return [pl.BlockSpec((bh, cols), functools.partial(lambda i, ib: (ib, 0), ib=row0 // bh + c)) for c in range(n // bh)]


def _qproj_kernel(xn_ref, *refs, n_q, n_qi, q_scale):
    wq, wqi, wkw = refs[:n_q], refs[n_q:n_q + n_qi], refs[n_q + n_qi]
    q_ref, qi_ref, kpe_ref, kpo_ref, wt_ref = refs[n_q + n_qi + 1:]
    xn = xn_ref[...]
    for c, w_ref in enumerate(wq):
        bw = w_ref.shape[0]
        q_ref[:, c * bw:(c + 1) * bw] = (_nt_dot(xn, w_ref[...]) * q_scale).astype(q_ref.dtype)
    for c, w_ref in enumerate(wqi):
        bw = w_ref.shape[0]
        qi_ref[:, c * bw:(c + 1) * bw] = _nt_dot(xn, w_ref[...]).astype(qi_ref.dtype)
    kw = _nt_dot(xn, wkw[...])
    lane = lax.broadcasted_iota(jnp.int32, kw.shape, 1)
    kpe_ref[...] = jnp.where(lane < IDX_DIM, kw, 0.0).astype(kpe_ref.dtype)
    kpo_ref[...] = jnp.where(lane >= LANES - IDX_DIM, pltpu.roll(kw, LANES - IDX_DIM, 1), 0.0).astype(kpo_ref.dtype)
    wt_ref[...] = kw.T[IDX_DIM:IDX_DIM + IDX_HEADS, :]


def _q_projections(xn, w, c_q, n_q, c_qi, n_qi, c_kw, q_scale, bsz, tm=512):
    m, d = xn.shape
    seq = m // bsz
    nt = seq // tm
    assert 2 * IDX_DIM == LANES
    pad_spec = pl.BlockSpec((None, tm, LANES), lambda i: (i // nt, i % nt, 0))
    q_specs, qi_specs, kw_specs = _row_specs(d, c_q, n_q), _row_specs(d, c_qi, n_qi), _row_specs(d, c_kw, LANES)
    row = lambda n: pl.BlockSpec((tm, n), lambda i: (i, 0))
    return pl.pallas_call(
        functools.partial(_qproj_kernel, n_q=len(q_specs), n_qi=len(qi_specs), q_scale=q_scale),
        out_shape=(jax.ShapeDtypeStruct((m, n_q), BF16), jax.ShapeDtypeStruct((m, n_qi), BF16),
                   jax.ShapeDtypeStruct((bsz, seq, LANES), BF16), jax.ShapeDtypeStruct((bsz, seq, LANES), BF16),
                   jax.ShapeDtypeStruct((bsz, IDX_HEADS, seq), F32)),
        grid=(m // tm,),
        in_specs=[row(d)] + q_specs + qi_specs + kw_specs,
        out_specs=(row(n_q), row(n_qi), pad_spec, pad_spec,
                   pl.BlockSpec((None, IDX_HEADS, tm), lambda i: (i // nt, 0, i % nt))),
        compiler_params=_params("parallel"),
        name="q_proj",
    )(xn, *([w] * (len(q_specs) + len(qi_specs) + 1)))


def _branch_a_kernel(x_ref, g1_ref, w_ref, lng_ref, lnb_ref, ws_ref, bst_ref, o_ref, xn_ref, z_ref, *, tm, tn, width):
    j = pl.program_id(1)

    @pl.when(j == 0)
    def _():
        xn_ref[...] = _rms(x_ref[...], g1_ref[...]).astype(xn_ref.dtype)

    z_ref[j] = jax.nn.gelu(_nt_dot(xn_ref[...], w_ref[...]))

    @pl.when(j == pl.num_programs(1) - 1)
    def _():
        per_half = width // tn
        gw = width // A_GROUPS
        row = lax.broadcasted_iota(jnp.int32, (A_BLOCK, A_BLOCK), 0)
        col = lax.broadcasted_iota(jnp.int32, (A_BLOCK, A_BLOCK), 1)
        causal = (col // CHUNK) <= (row // CHUNK)
        for r in range(tm // A_BLOCK):
            rows = slice(r * A_BLOCK, (r + 1) * A_BLOCK)
            v = jnp.concatenate([z_ref[per_half + c, rows, :] for c in range(per_half)], axis=-1)
            vc = v - jnp.mean(v, axis=-1, keepdims=True)
            vn = vc * lax.rsqrt(jnp.mean(vc * vc, axis=-1, keepdims=True) + EPS)
            vn = (vn * lng_ref[...] + lnb_ref[...]).astype(BF16)
            for g in range(A_GROUPS):
                wm = jnp.where(causal, ws_ref[g], 0.0).astype(BF16)
                sv = _dot(wm, vn[:, g * gw:(g + 1) * gw]) + bst_ref[:, g:g + 1]
                c, off = divmod(g * gw, tn)
                u = z_ref[c, rows, off:off + gw]
                o_ref[rows, g * gw:(g + 1) * gw] = (u * sv).astype(o_ref.dtype)


def _branch_a(x, g1, w_za, ln_g, ln_b, w_s, b_s, tm=512, tn=2048):
    m, d = x.shape
    width = ln_g.shape[0]
    assert (width // A_GROUPS) <= tn and tn % (width // A_GROUPS) == 0
    kern = functools.partial(_branch_a_kernel, tm=tm, tn=tn, width=width)
    return pl.pallas_call(
        kern,
        out_shape=(jax.ShapeDtypeStruct((m, width), BF16), jax.ShapeDtypeStruct((m, d), BF16)),
        grid=(m // tm, 2 * width // tn),
        in_specs=[
            pl.BlockSpec((tm, d), lambda i, j: (i, 0)),
            pl.BlockSpec((1, d), lambda i, j: (0, 0)),
            pl.BlockSpec((tn, d), lambda i, j: (j, 0)),
            pl.BlockSpec((1, width), lambda i, j: (0, 0)),
            pl.BlockSpec((1, width), lambda i, j: (0, 0)),
            pl.BlockSpec((A_GROUPS, A_BLOCK, A_BLOCK), lambda i, j: (0, 0, 0)),
            pl.BlockSpec((A_BLOCK, A_GROUPS), lambda i, j: (0, 0)),
        ],
        out_specs=(pl.BlockSpec((tm, width), lambda i, j: (i, 0)), pl.BlockSpec((tm, d), lambda i, j: (i, 0))),
        scratch_shapes=[pltpu.VMEM((2 * width // tn, tm, tn), F32)],
        compiler_params=_params("parallel", "arbitrary"),
        name="branch_a",
    )(x, g1.reshape(1, d), w_za, ln_g.reshape(1, width), ln_b.reshape(1, width), w_s, b_s.T)


def _kv_kernel(xn_ref, wc_ref, g_ref, wuk_ref, wuvt_ref, k_ref, vt_ref):
    c = _nt_dot(xn_ref[...], wc_ref[...])
    cn = _rms(c, g_ref[...]).astype(BF16)
    k_ref[...] = _dot(cn, wuk_ref[...]).astype(k_ref.dtype)
    vt = _nt_dot(wuvt_ref[...], cn).astype(vt_ref.dtype)
    rows = HEAD_DIM + ONES_ROWS
    for h in range(vt.shape[0] // HEAD_DIM):
        vt_ref[h * rows:h * rows + HEAD_DIM, :] = vt[h * HEAD_DIM:(h + 1) * HEAD_DIM]
        vt_ref[h * rows + HEAD_DIM:(h + 1) * rows, :] = jnp.ones((ONES_ROWS, vt.shape[1]), vt_ref.dtype)


def _kv_proj(xn, w_c, col0, g, w_uk, w_uv_t, bsz, seq, tk):
    m, d = xn.shape
    lat, hd = w_uk.shape
    assert col0 % lat == 0
    j0 = col0 // lat
    nt = seq // tk
    vrows = hd // HEAD_DIM * (HEAD_DIM + ONES_ROWS)
    return pl.pallas_call(
        _kv_kernel,
        out_shape=(jax.ShapeDtypeStruct((m, hd), BF16), jax.ShapeDtypeStruct((bsz, nt, vrows, tk), BF16)),
        grid=(m // tk,),
        in_specs=[
            pl.BlockSpec((tk, d), lambda i: (i, 0)),
            pl.BlockSpec((lat, d), lambda i: (j0, 0)),
            pl.BlockSpec((1, lat), lambda i: (0, 0)),
            pl.BlockSpec((lat, hd), lambda i: (0, 0)),
            pl.BlockSpec((hd, lat), lambda i: (0, 0)),
        ],
        out_specs=(pl.BlockSpec((tk, hd), lambda i: (i, 0)),
                   pl.BlockSpec((None, None, vrows, tk), lambda i: (i // nt, i % nt, 0, 0))),
        compiler_params=_params("parallel"),
        name="kv_proj",
    )(xn, w_c, g.reshape(1, lat), w_uk, w_uv_t)


def _n_key_tiles(qt, tq, tk):
    return lax.div(qt * tq + tq + tk - 1, tk)


def _indexer_kernel(qi_ref, kpe_ref, kpo_ref, wt_ref, *refs, tq, tk, k_sel, n_cast):
    cast_in, o_ref, cast_out, sc_ref = refs[:n_cast], refs[n_cast], refs[n_cast + 1:2 * n_cast + 1], refs[-1]
    for w_ref, wb_ref in zip(cast_in, cast_out):
        wb_ref[...] = w_ref[...].astype(wb_ref.dtype)

    qt = pl.program_id(1)
    n_kt = _n_key_tiles(qt, tq, tk)
    nk = sc_ref.shape[0]
    neg_inf = -jnp.inf

    def count(pred):
        rows = 16
        def body(kt, cnt):
            hit = jnp.where(pred(sc_ref[kt], kt), 1.0, 0.0)
            return cnt + jnp.sum(hit.reshape(tk // rows, rows, tq), axis=0)
        cnt = lax.fori_loop(0, n_kt, body, jnp.zeros((rows, tq), F32))
        return jnp.sum(cnt, axis=0, keepdims=True)

    def key_pos(kt):
        return kt * tk + lax.broadcasted_iota(jnp.int32, (tk, tq), 0)

    wt = wt_ref[...] * (IDX_DIM ** -0.5 * IDX_HEADS ** -0.5)
    q_chunk = (qt * tq + lax.broadcasted_iota(jnp.int32, (1, tq), 1)) // CHUNK

    def score_tile(kt):
        k0 = pl.multiple_of(kt * tk, tk)
        ke = kpe_ref[pl.ds(k0, tk), :]
        ko = kpo_ref[pl.ds(k0, tk), :]
        acc = jnp.zeros((tk, tq), F32)
        for j in range(IDX_HEADS // 2):
            qp = qi_ref[:, j * 128:(j + 1) * 128]
            acc = acc + wt[2 * j:2 * j + 1, :] * jnp.maximum(_nt_dot(ke, qp), 0.0)
            acc = acc + wt[2 * j + 1:2 * j + 2, :] * jnp.maximum(_nt_dot(ko, qp), 0.0)
        k_chunk = (k0 + lax.broadcasted_iota(jnp.int32, (tk, 1), 0)) // CHUNK
        sc_ref[kt] = jnp.where(k_chunk <= q_chunk, acc, neg_inf)

    def score_pair(i, carry):
        score_tile(2 * i)
        score_tile(2 * i + 1)
        return carry
    lax.fori_loop(0, lax.div(n_kt, 2), score_pair, 0)

    @pl.when(lax.rem(n_kt, 2) == 1)
    def _():
        score_tile(n_kt - 1)

    def key_to_float(key):
        return lax.bitcast_convert_type(key ^ ((key >> 31) & 0x7FFFFFFF), F32)

    def bit_body(i, carry):
        tau, n_tau = carry
        cand = tau ^ jnp.left_shift(jnp.int32(1), 31 - i)
        cand_f = key_to_float(cand)
        n = count(lambda s, kt: s >= cand_f)
        accept = n >= k_sel
        return jnp.where(accept, cand, tau), jnp.where(accept, n, n_tau)
    every = jnp.broadcast_to((n_kt * tk).astype(F32), (1, tq))
    tau, n_tau = lax.fori_loop(0, 32, bit_body, (jnp.full((1, tq), INT_MIN, jnp.int32), every))
    key_neg_inf = INT_MIN + 0x7FFFFF
    tau_f = jnp.where(tau <= key_neg_inf, neg_inf, key_to_float(tau))
    has_ties = jnp.max(n_tau) > k_sel

    def write_mask(sel):
        def body(kt, carry):
            o_ref[kt] = jnp.where(sel(sc_ref[kt], kt), 0.0, NEG_BIG).astype(o_ref.dtype)
            return carry
        lax.fori_loop(0, n_kt, body, 0)

    @pl.when(jnp.logical_not(has_ties))
    def _():
        tau_c = jnp.maximum(tau_f, jnp.finfo(F32).min)
        write_mask(lambda s, kt: s >= tau_c)

    @pl.when(has_ties)
    def _():
        need = k_sel - count(lambda s, kt: s > tau_f)

        def idx_body(i, cut):
            cand = cut + jnp.left_shift(jnp.int32(1), 12 - i)
            n = count(lambda s, kt: (s == tau_f) & (key_pos(kt) < cand))
            return jnp.where(n <= need, cand, cut)
        cut = lax.fori_loop(0, 13, idx_body, jnp.zeros((1, tq), jnp.int32))
        write_mask(lambda s, kt: ((s > tau_f) | ((s == tau_f) & (key_pos(kt) < cut))) & (s > neg_inf))

    def fill_body(kt, carry):
        o_ref[kt] = jnp.full((tk, tq), NEG_BIG, o_ref.dtype)
        return carry
    lax.fori_loop(n_kt, nk, fill_body, 0)


def _indexer(q_idx, kpe, kpo, w_t, tq, tk, cast_weights):
    bsz, seq, _ = kpe.shape
    nq, nk = seq // tq, seq // tk
    k_sel = min(TOPK_MAX, seq // 4)
    assert tq >= k_sel and tq % CHUNK == 0 and tk % tq == 0
    qi_cols = q_idx.shape[1]
    steps = bsz * nq
    assert all(n % (steps * BF16_SUBLANES) == 0 and row0 % BF16_SUBLANES == 0 for _, row0, n in cast_weights)

    def slab_in(w, row0, n):
        r = n // steps
        return pl.BlockSpec((pl.Element(r), pl.Element(w.shape[1])),
                            lambda b, t: (pl.multiple_of(row0 + (b * nq + t) * r, BF16_SUBLANES), 0))
    slab_out = lambda w, row0, n: pl.BlockSpec((n // steps, w.shape[1]), lambda b, t: (b * nq + t, 0))
    n_cast = len(cast_weights)
    return pl.pallas_call(
        functools.partial(_indexer_kernel, tq=tq, tk=tk, k_sel=k_sel, n_cast=n_cast),
        out_shape=[jax.ShapeDtypeStruct((bsz, nq, nk, tk, tq), F32)]
        + [jax.ShapeDtypeStruct((n, w.shape[1]), BF16) for w, _, n in cast_weights],
        grid=(bsz, nq),
        in_specs=[
            pl.BlockSpec((tq, qi_cols), lambda b, t: (b * nq + t, 0)),
            pl.BlockSpec((None, seq, 2 * IDX_DIM), lambda b, t: (b, 0, 0)),
            pl.BlockSpec((None, seq, 2 * IDX_DIM), lambda b, t: (b, 0, 0)),
            pl.BlockSpec((None, IDX_HEADS, tq), lambda b, t: (b, 0, t)),
        ] + [slab_in(*c) for c in cast_weights],
        out_specs=[pl.BlockSpec((None, None, nk, tk, tq), lambda b, t: (b, t, 0, 0, 0))]
        + [slab_out(*c) for c in cast_weights],
        scratch_shapes=[pltpu.VMEM((nk, tk, tq), F32)],
        compiler_params=_params("parallel", "parallel"),
        name="indexer",
    )(q_idx, kpe, kpo, w_t, *[w for w, _, _ in cast_weights])


def _attention_kernel(q_ref, k_ref, vt_ref, bias_ref, o_ref, m_ref, acc_ref, sa_ref, sb_ref, ma_ref, mb_ref,
                      *, tq, tk, group):
    qt = pl.program_id(2)
    n_kt = _n_key_tiles(qt, tq, tk)
    rows = HEAD_DIM + ONES_ROWS
    m_ref[...] = jnp.full(m_ref.shape, NEG_BIG, F32)
    acc_ref[...] = jnp.zeros(acc_ref.shape, F32)

    def scores(kt, s_ref, mx_ref):
        k0 = pl.multiple_of(kt * tk, tk)
        for g in range(group):
            cols = slice(g * HEAD_DIM, (g + 1) * HEAD_DIM)
            s = _nt_dot(k_ref[pl.ds(k0, tk), cols], q_ref[:, cols]) + bias_ref[kt]
            s_ref[g] = s
            mx_ref[g] = jnp.max(s, axis=0, keepdims=True)

    def consume(kt, s_ref, mx_ref):
        for g in range(group):
            m_old = m_ref[g]
            m_new = jnp.maximum(m_old, mx_ref[g])
            m_ref[g] = m_new
            p = jnp.exp2(s_ref[g] - m_new).astype(BF16)
            pv = _dot(vt_ref[kt, g * rows:(g + 1) * rows, :], p)
            acc_ref[g] = jnp.exp2(m_old - m_new) * acc_ref[g] + pv

    bufs = ((sa_ref, ma_ref), (sb_ref, mb_ref))

    def run(kt0, count, prefetch):
        for c in range(count):
            if c + 1 < count or prefetch:
                scores(kt0 + c + 1, *bufs[(c + 1) % 2])
            consume(kt0 + c, *bufs[c % 2])

    scores(0, sa_ref, ma_ref)
    n_trips = lax.div(n_kt - 1, UNROLL)

    def trip_body(i, carry):
        run(UNROLL * i, UNROLL, prefetch=True)
        return carry
    lax.fori_loop(0, n_trips, trip_body, 0)
    kt = UNROLL * n_trips
    for rest in range(1, UNROLL + 1):
        @pl.when(n_kt - kt == rest)
        def _(rest=rest):
            run(kt, rest, prefetch=False)

    for g in range(group):
        acc = acc_ref[g]
        out = acc[:HEAD_DIM] / acc[HEAD_DIM:HEAD_DIM + 1]
        o_ref[:, g * HEAD_DIM:(g + 1) * HEAD_DIM] = out.T.astype(o_ref.dtype)


def _attention(q, k, vt, bias, tq, tk, group=8):
    bsz, nk, vrows, _ = vt.shape
    rows = HEAD_DIM + ONES_ROWS
    hd = vrows // rows * HEAD_DIM
    seq = nk * tk
    nq = seq // tq
    gw = group * HEAD_DIM
    return pl.pallas_call(
        functools.partial(_attention_kernel, tq=tq, tk=tk, group=group),
        out_shape=jax.ShapeDtypeStruct((bsz * seq, hd), BF16),
        grid=(bsz, hd // gw, nq),
        in_specs=[
            pl.BlockSpec((tq, gw), lambda b, g, t: (b * nq + t, g)),
            pl.BlockSpec((None, seq, gw), lambda b, g, t: (b, 0, g)),
            pl.BlockSpec((None, nk, group * rows, tk), lambda b, g, t: (b, 0, g, 0)),
            pl.BlockSpec((None, None, nk, tk, tq), lambda b, g, t: (b, t, 0, 0, 0)),
        ],
        out_specs=pl.BlockSpec((tq, gw), lambda b, g, t: (b * nq + t, g)),
        scratch_shapes=[pltpu.VMEM((group, 1, tq), F32), pltpu.VMEM((group, rows, tq), F32),
                        pltpu.VMEM((group, tk, tq), F32), pltpu.VMEM((group, tk, tq), F32),
                        pltpu.VMEM((group, 1, tq), F32), pltpu.VMEM((group, 1, tq), F32)],
        compiler_params=_params("parallel", "parallel", "arbitrary"),
        name="attention",
    )(q, k.reshape(bsz, seq, hd), vt, bias)


def _merge_kernel(xn_ref, ya_ref, yb_ref, wga_ref, wgb_ref, woa_ref, wob_ref, o_ref):
    xn = xn_ref[...]
    ga = jax.nn.sigmoid(_nt_dot(xn, wga_ref[...]))
    gb = jax.nn.sigmoid(_nt_dot(xn, wgb_ref[...]))
    o_ref[...] = (ga * _dot(ya_ref[...], woa_ref[...]) + gb * _dot(yb_ref[...], wob_ref[...])).astype(o_ref.dtype)


def _merge(xn, ya, yb, w_gates, w_oa, w_ob, tm=1024, tn=512):
    m, d = xn.shape
    row = lambda k: pl.BlockSpec((tm, k), lambda i, j: (i, 0))
    colw = lambda k: pl.BlockSpec((k, tn), lambda i, j: (0, j))
    gate = lambda j0: pl.BlockSpec((tn, d), lambda i, j: (j0 + j, 0))
    return pl.pallas_call(
        _merge_kernel,
        out_shape=jax.ShapeDtypeStruct((m, d), BF16),
        grid=(m // tm, d // tn),
        in_specs=[row(d), row(ya.shape[1]), row(yb.shape[1]),
                  gate(0), gate(d // tn), colw(ya.shape[1]), colw(yb.shape[1])],
        out_specs=pl.BlockSpec((tm, tn), lambda i, j: (i, j)),
        compiler_params=_params("parallel", "arbitrary"),
        name="merge",
    )(xn, ya, yb, w_gates, w_gates, w_oa, w_ob)


def _out_kernel(x_ref, mix_ref, w_ref, g_ref, h_ref, hn_ref):
    h = x_ref[...] + _dot(mix_ref[...], w_ref[...])
    h_ref[...] = h
    hn_ref[...] = _rms(h, g_ref[...]).astype(hn_ref.dtype)


def _out_proj(x, mixed, w_out, g, tm=512):
    m, d = x.shape
    row = pl.BlockSpec((tm, d), lambda i: (i, 0))
    return pl.pallas_call(
        _out_kernel,
        out_shape=(jax.ShapeDtypeStruct((m, d), F32), jax.ShapeDtypeStruct((m, d), BF16)),
        grid=(m // tm,),
        in_specs=[row, row, pl.BlockSpec((d, d), lambda i: (0, 0)), pl.BlockSpec((1, d), lambda i: (0, 0))],
        out_specs=(row, row),
        compiler_params=_params("parallel"),
        name="out_proj",
    )(x, mixed, w_out, g.reshape(1, d))


def _ffn_kernel(hn_ref, h_ref, wg_ref, wu_ref, wd_ref, g_ref, o_ref, acc_ref, *, final_norm):
    j = pl.program_id(1)

    @pl.when(j == 0)
    def _():
        acc_ref[...] = h_ref[...]

    hn = hn_ref[...]
    t = (jax.nn.silu(_dot(hn, wg_ref[...])) * _dot(hn, wu_ref[...])).astype(BF16)
    acc_ref[...] += _dot(t, wd_ref[...])

    @pl.when(j == pl.num_programs(1) - 1)
    def _():
        o_ref[...] = _rms(acc_ref[...], g_ref[...]) if final_norm else acc_ref[...]


def _ffn(hn, h, w_gate, w_up, w_down, g, final_norm, tm=512, tf=512):
    m, d = hn.shape
    f = w_gate.shape[1]
    row = pl.BlockSpec((tm, d), lambda i, j: (i, 0))
    return pl.pallas_call(
        functools.partial(_ffn_kernel, final_norm=final_norm),
        out_shape=jax.ShapeDtypeStruct((m, d), F32),
        grid=(m // tm, f // tf),
        in_specs=[row, row,
                  pl.BlockSpec((d, tf), lambda i, j: (0, j)),
                  pl.BlockSpec((d, tf), lambda i, j: (0, j)),
                  pl.BlockSpec((tf, d), lambda i, j: (j, 0)),
                  pl.BlockSpec((1, d), lambda i, j: (0, 0))],
        out_specs=row,
        scratch_shapes=[pltpu.VMEM((tm, d), F32)],
        compiler_params=_params("parallel", "arbitrary"),
        name="ffn",
    )(hn, h, w_gate, w_up, w_down, g.reshape(1, d))


def kernel(x, norm1_g, w_in, a_ln_g, a_ln_b, a_w_s, a_b_s, kv_norm_g, w_uk, w_uv, w_oa, w_ob, w_out, norm2_g,
           w_ff_gate, w_ff_up, w_ff_down, final_g):
    bsz, seq, d = x.shape
    m = bsz * seq
    depth = norm1_g.shape[0]
    a_width = a_ln_g.shape[1]
    lat = kv_norm_g.shape[1]
    heads, head_dim = w_uk.shape[2], w_uk.shape[3]
    assert head_dim == HEAD_DIM
    hd = heads * head_dim
    qi = IDX_HEADS * IDX_DIM
    c_q = 2 * a_width
    c_kv = c_q + hd
    c_qi = c_kv + lat
    c_ki = c_qi + qi
    c_wi = c_ki + IDX_DIM
    c_g = c_wi + IDX_HEADS

    h = x.reshape(m, d)
    for l in range(depth):
        w_t32 = w_in[l].T
        w = w_t32[:c_ki + LANES].astype(BF16)

        y_a, xn = _branch_a(h, norm1_g[l], w, a_ln_g[l], a_ln_b[l], a_w_s[l], a_b_s[l])
        q, q_idx, kpe, kpo, w_t = _q_projections(xn, w, c_q, hd, c_qi, qi, c_ki, HEAD_DIM ** -0.5 * LOG2E, bsz)
        k_tok, v_t = _kv_proj(xn, w, c_kv, kv_norm_g[l], w_uk[l].reshape(lat, hd).astype(BF16),
                              w_uv[l].reshape(lat, hd).T.astype(BF16), bsz, seq, ATT_TK)
        bias, w_gates, w_oa_b, w_ob_b, w_out_b, w_fg_b, w_fu_b, w_fd_b = _indexer(
            q_idx, kpe, kpo, w_t, ATT_TQ, ATT_TK,
            [(w_t32, c_g, 2 * d)] + [(a[l], 0, a[l].shape[0]) for a in (w_oa, w_ob, w_out, w_ff_gate, w_ff_up, w_ff_down)])
        y_b = _attention(q, k_tok, v_t, bias, ATT_TQ, ATT_TK)
        mixed = _merge(xn, y_a, y_b, w_gates, w_oa_b, w_ob_b)
        h, hn = _out_proj(h, mixed, w_out_b, norm2_g[l])
        h = _ffn(hn, h, w_fg_b, w_fu_b, w_fd_b, final_g, final_norm=(l == depth - 1))
    return h.reshape(bsz, seq, d)
```
